```python
import math
import jax, jax.numpy as jnp
from jax import lax
import numpy as np

D_MODEL = 1024
BATCH = 4
SEQ = 8192
DEPTH = 1

HEAD_DIM = 64
HEADS_PER_GROUP = 4
ATTN_GROUPS = ((128, 1), (512, 4), (2048, 16))
N_ATTN_HEADS = HEADS_PER_GROUP * len(ATTN_GROUPS)
ATTN_WIDTH = N_ATTN_HEADS * HEAD_DIM
KEYS_PER_QUERY = ATTN_GROUPS[0][0] // ATTN_GROUPS[0][1] + 1
MAX_WINDOW = max(w for w, _ in ATTN_GROUPS)
Q_BLOCK = 128
NEG_INF = -1e30
CONV_CH = 512
CONV_WIDTH = 31
N_BRANCHES = 2
IN_COLS = 2 * CONV_CH + 3 * ATTN_WIDTH + N_BRANCHES * D_MODEL
REL_BUCKETS = 32
REL_MAX_DISTANCE = 2048
PEER_HEADS = 8
PEER_N_KEYS = 128
PEER_N_EXPERTS = PEER_N_KEYS * PEER_N_KEYS
PEER_KEY_DIM = 256
PEER_TOPK = 16
TOKEN_CHUNK = 128
EPS = 1e-6

kernel_name = "hybrid_conv_dilattn_peer_block"


def rmsnorm(x, g):
    x32 = x.astype(jnp.float32)
    y = x32 * lax.rsqrt(jnp.mean(x32 * x32, axis=-1, keepdims=True) + EPS)
    return y.astype(x.dtype) * g


def layernorm(x, g, b):
    x32 = x.astype(jnp.float32)
    mu = jnp.mean(x32, axis=-1, keepdims=True)
    var = jnp.mean(jnp.square(x32 - mu), axis=-1, keepdims=True)
    y = (x32 - mu) * lax.rsqrt(var + EPS)
    return y.astype(x.dtype) * g + b


def t5_causal_bucket(distance):
    n = distance.astype(jnp.int32)
    max_exact = REL_BUCKETS // 2
    nf = jnp.maximum(n, 1).astype(jnp.float32)
    large = max_exact + (jnp.log(nf / max_exact) / math.log(REL_MAX_DISTANCE / max_exact)
                         * (REL_BUCKETS - max_exact)).astype(jnp.int32)
    large = jnp.minimum(large, REL_BUCKETS - 1)
    return jnp.where(n < max_exact, n, large)


def conformer_conv(glu_in, b_glu, dw_w, dw_b, ln_g, ln_b, w_proj, b_proj):
    a, gate = jnp.split(glu_in + b_glu, 2, axis=-1)
    u = a * jax.nn.sigmoid(gate)
    u = lax.conv_general_dilated(u, dw_w, window_strides=(1,),
                                 padding=[(CONV_WIDTH - 1, 0)],
                                 dimension_numbers=('NWC', 'WIO', 'NWC'),
                                 feature_group_count=CONV_CH) + dw_b
    u = jax.nn.silu(layernorm(u, ln_g, ln_b))
    return u @ w_proj + b_proj


def dilated_attention(q, k, v, rel_bias):
    B, S = q.shape[0], q.shape[1]
    pad = ((0, 0), (MAX_WINDOW, 0), (0, 0), (0, 0))
    k_pad, v_pad = jnp.pad(k, pad), jnp.pad(v, pad)
    scale = HEAD_DIM ** -0.5
    steps = jnp.arange(KEYS_PER_QUERY)
    offs = jnp.arange(Q_BLOCK)
    k_groups, v_groups, biases = [], [], []
    for g, (w, d) in enumerate(ATTN_GROUPS):
        hs = slice(g * HEADS_PER_GROUP, (g + 1) * HEADS_PER_GROUP)
        k_groups.append(k_pad[:, :, hs])
        v_groups.append(v_pad[:, :, hs])
        biases.append(rel_bias[t5_causal_bucket(steps * d)][:, hs].T)

    def block(blk):
        t0 = blk * Q_BLOCK
        qb = lax.dynamic_slice_in_dim(q, t0, Q_BLOCK, axis=1)
        outs, lses = [], []
        for g, (w, d) in enumerate(ATTN_GROUPS):
            hs = slice(g * HEADS_PER_GROUP, (g + 1) * HEADS_PER_GROUP)
            pos = t0 + offs[:, None] - d * steps[None, :]
            valid = pos >= 0
            idx = pos + MAX_WINDOW
            kg = jnp.take(k_groups[g], idx, axis=1)
            vg = jnp.take(v_groups[g], idx, axis=1)
            s = jnp.einsum('bqhd,bqkhd->bhqk', qb[:, :, hs], kg).astype(jnp.float32) * scale
            s = s + biases[g][None, :, None, :].astype(jnp.float32)
            s = jnp.where(valid[None, None], s, NEG_INF)
            lse = jax.nn.logsumexp(s, axis=-1, keepdims=True)
            p = jnp.exp(s - lse).astype(v.dtype)
            outs.append(jnp.einsum('bhqk,bqkhd->bqhd', p, vg))
            lses.append(lse[..., 0])
        wgt = jax.nn.softmax(jnp.stack(lses, axis=0), axis=0)
        wgt = jnp.transpose(wgt, (0, 1, 3, 2))[..., None].astype(v.dtype)
        o = jnp.concatenate([outs[g] * wgt[g] for g in range(len(ATTN_GROUPS))], axis=2)
        return o.reshape(B, Q_BLOCK, ATTN_WIDTH)

    o = lax.map(block, jnp.arange(S // Q_BLOCK))
    return jnp.transpose(o, (1, 0, 2, 3)).reshape(B, S, ATTN_WIDTH)


def hybrid_mixer(xn, w_in, conv_b_glu, conv_dw_w, conv_dw_b, conv_ln_g, conv_ln_b,
                 conv_w_proj, conv_b_proj, q_norm_g, k_norm_g, rel_bias, attn_w_proj, mix_w_out):
    B, S, _ = xn.shape
    proj = xn @ w_in
    c0 = 2 * CONV_CH
    glu_in, q, k, v, gates = jnp.split(
        proj, [c0, c0 + ATTN_WIDTH, c0 + 2 * ATTN_WIDTH, c0 + 3 * ATTN_WIDTH], axis=-1)
    conv_out = conformer_conv(glu_in, conv_b_glu, conv_dw_w, conv_dw_b,
                              conv_ln_g, conv_ln_b, conv_w_proj, conv_b_proj)
    heads = (B, S, N_ATTN_HEADS, HEAD_DIM)
    q = rmsnorm(q.reshape(heads), q_norm_g)
    k = rmsnorm(k.reshape(heads), k_norm_g)
    v = v.reshape(heads)
    attn_out = dilated_attention(q, k, v, rel_bias) @ attn_w_proj
    g_conv, g_attn = jnp.split(gates, 2, axis=-1)
    merged = jax.nn.sigmoid(g_conv) * conv_out + jax.nn.sigmoid(g_attn) * attn_out
    return merged @ mix_w_out


def peer(xn, w_q, sub_keys, u_tab, v_tab):
    B, S, D = xn.shape
    half = PEER_KEY_DIM // 2

    def chunk(xc):
        q = (xc @ w_q).reshape(TOKEN_CHUNK, PEER_HEADS, 2, half)
        s = jnp.einsum('thpd,hpnd->thpn', q, sub_keys).astype(jnp.float32)
        top_s, top_i = lax.top_k(s, PEER_TOPK)
        cand = top_s[:, :, 0, :, None] + top_s[:, :, 1, None, :]
        best_s, best_c = lax.top_k(cand.reshape(TOKEN_CHUNK, PEER_HEADS, PEER_TOPK * PEER_TOPK), PEER_TOPK)
        i1 = jnp.take_along_axis(top_i[:, :, 0], best_c // PEER_TOPK, axis=-1)
        i2 = jnp.take_along_axis(top_i[:, :, 1], best_c % PEER_TOPK, axis=-1)
        e = i1 * PEER_N_KEYS + i2
        gate = jax.nn.softmax(best_s, axis=-1).astype(xc.dtype)
        h = jnp.einsum('thkd,td->thk', u_tab[e], xc)
        act = jax.nn.gelu(h, approximate=False) * gate
        return jnp.einsum('thk,thkd->td', act, v_tab[e])

    y = lax.map(chunk, xn.reshape(-1, TOKEN_CHUNK, D))
    return y.reshape(B, S, D)


def setup_inputs(seed: int = 0) -> dict:
    key = jax.random.key(seed)
    ks = jax.random.split(key, 20)
    f32 = jnp.float32
    nrm = lambda k, shape, s: jax.random.normal(k, shape, f32) * s
    L, D = DEPTH, D_MODEL
    return {
        "x": nrm(ks[0], (BATCH, SEQ, D), 1.0),
        "w_in": nrm(ks[1], (L, D, IN_COLS), D ** -0.5),
        "conv_b_glu": nrm(ks[2], (L, 2 * CONV_CH), 0.02),
        "conv_dw_w": nrm(ks[3], (L, CONV_WIDTH, 1, CONV_CH), CONV_WIDTH ** -0.5),
        "conv_dw_b": nrm(ks[4], (L, CONV_CH), 0.02),
        "conv_ln_g": 1.0 + nrm(ks[5], (L, CONV_CH), 0.02),
        "conv_ln_b": nrm(ks[6], (L, CONV_CH), 0.02),
        "conv_w_proj": nrm(ks[7], (L, CONV_CH, D), CONV_CH ** -0.5),
        "conv_b_proj": nrm(ks[8], (L, D), 0.02),
        "q_norm_g": 1.0 + nrm(ks[9], (L, HEAD_DIM), 0.02),
        "k_norm_g": 1.0 + nrm(ks[10], (L, HEAD_DIM), 0.02),
        "rel_bias": nrm(ks[11], (REL_BUCKETS, N_ATTN_HEADS), 0.5),
        "attn_w_proj": nrm(ks[12], (L, ATTN_WIDTH, D), ATTN_WIDTH ** -0.5),
        "mix_w_out": nrm(ks[13], (L, D, D), D ** -0.5),
        "norm1_g": 1.0 + nrm(ks[14], (L, D), 0.02),
        "norm2_g": 1.0 + nrm(ks[15], (L, D), 0.02),
        "peer_w_q": nrm(ks[16], (L, D, PEER_HEADS * PEER_KEY_DIM), D ** -0.5),
        "peer_sub_keys": nrm(ks[17], (L, PEER_HEADS, 2, PEER_N_KEYS, PEER_KEY_DIM // 2), (PEER_KEY_DIM // 2) ** -0.5),
        "peer_u": nrm(ks[18], (L, PEER_N_EXPERTS, D), D ** -0.5),
        "peer_v": nrm(ks[19], (L, PEER_N_EXPERTS, D), 0.5),
    }


def reference(x, w_in, conv_b_glu, conv_dw_w, conv_dw_b, conv_ln_g, conv_ln_b, conv_w_proj,
              conv_b_proj, q_norm_g, k_norm_g, rel_bias, attn_w_proj, mix_w_out, norm1_g,
              norm2_g, peer_w_q, peer_sub_keys, peer_u, peer_v):
    for l in range(DEPTH):
        xn = rmsnorm(x, norm1_g[l])
        x = x + hybrid_mixer(xn, w_in[l], conv_b_glu[l], conv_dw_w[l], conv_dw_b[l],
                             conv_ln_g[l], conv_ln_b[l], conv_w_proj[l], conv_b_proj[l],
                             q_norm_g[l], k_norm_g[l], rel_bias, attn_w_proj[l], mix_w_out[l])
        hn = rmsnorm(x, norm2_g[l])
        x = x + peer(hn, peer_w_q[l], peer_sub_keys[l], peer_u[l], peer_v[l])
    return x
```

```python
import functools
import math

import numpy as np
import jax
import jax.numpy as jnp
from jax import lax
from jax.experimental import pallas as pl
from jax.experimental.pallas import tpu as pltpu

F32 = jnp.float32
BF16 = jnp.bfloat16

HEAD_DIM = 64
HEADS_PER_GROUP = 4
ATTN_GROUPS = ((128, 1), (512, 4), (2048, 16))
N_ATTN_HEADS = HEADS_PER_GROUP * len(ATTN_GROUPS)
ATTN_WIDTH = N_ATTN_HEADS * HEAD_DIM
GROUP_WIDTH = HEADS_PER_GROUP * HEAD_DIM
KEY_STEPS = 128
NEG_INF = -1e30
CONV_CH = 512
CONV_WIDTH = 31
CONV_HALO = 32
REL_BUCKETS = 32
REL_MAX_DISTANCE = 2048
PEER_HEADS = 8
PEER_N_KEYS = 128
PEER_TOPK = 16
EPS = 1e-6

LANES = 128
VMEM_LIMIT = 56 * 1024 * 1024

_NT = (((1,), (1,)), ((), ()))


def _params(n_axes):
    return pltpu.CompilerParams(dimension_semantics=("arbitrary",) * n_axes,
                                vmem_limit_bytes=VMEM_LIMIT)


def _full(shape):
    n = len(shape)
    return pl.BlockSpec(shape, lambda *_: (0,) * n)


def _inproj_kernel(x_ref, g1_ref, wglu_ref, bglu_ref, wq_ref, wk_ref, wv_ref, qg_ref, kg_ref,
                   pm_ref, wg_ref, u_ref, q_ref, k_ref, v_ref, gate_ref):
    x = x_ref[...]
    ms = jnp.mean(x * x, axis=-1, keepdims=True)
    xn = (x * lax.rsqrt(ms + EPS) * g1_ref[...]).astype(BF16)

    glu = jnp.dot(xn, wglu_ref[...], preferred_element_type=F32) + bglu_ref[...]
    u_ref[...] = glu[:, :CONV_CH] * jax.nn.sigmoid(glu[:, CONV_CH:])

    def head_rmsnorm(w_ref, g_ref):
        y = jnp.dot(xn, w_ref[...], preferred_element_type=F32)
        msq = jnp.dot((y * y).astype(BF16), pm_ref[...], preferred_element_type=F32)
        return y * lax.rsqrt(msq + EPS) * g_ref[...]

    q_ref[...] = (head_rmsnorm(wq_ref, qg_ref) * (HEAD_DIM ** -0.5)).astype(BF16)
    k_ref[...] = head_rmsnorm(wk_ref, kg_ref).astype(BF16)
    v_ref[...] = jnp.dot(xn, wv_ref[...], preferred_element_type=F32).astype(BF16)
    gate_ref[...] = jax.nn.sigmoid(
        jnp.dot(xn, wg_ref[...], preferred_element_type=F32)).astype(BF16)


def _inproj(x2, g1, wglu, bglu, wq, wk, wv, qg, kg, pm, wg, tm):
    n, d = x2.shape
    row = lambda w: pl.BlockSpec((tm, w), lambda i: (i, 0))
    return pl.pallas_call(
        _inproj_kernel,
        grid=(n // tm,),
        in_specs=[row(d), _full(g1.shape), _full(wglu.shape), _full(bglu.shape), _full(wq.shape),
                  _full(wk.shape), _full(wv.shape), _full(qg.shape), _full(kg.shape),
                  _full(pm.shape), _full(wg.shape)],
        out_specs=[row(CONV_CH), row(ATTN_WIDTH), row(ATTN_WIDTH), row(ATTN_WIDTH), row(2 * d)],
        out_shape=[jax.ShapeDtypeStruct((n, CONV_CH), F32),
                   jax.ShapeDtypeStruct((n, ATTN_WIDTH), BF16),
                   jax.ShapeDtypeStruct((n, ATTN_WIDTH), BF16),
                   jax.ShapeDtypeStruct((n, ATTN_WIDTH), BF16),
                   jax.ShapeDtypeStruct((n, 2 * d), BF16)],
        compiler_params=_params(1),
        name="inproj",
    )(x2, g1, wglu, bglu, wq, wk, wv, qg, kg, pm, wg)


def _attn_kernel(q_ref, kc_ref, kp_ref, vc_ref, vp_ref, bias_ref, o_ref, lse_ref, kbuf, vbuf,
                 *, nsub):
    pair = pl.program_id(2)
    first_blk = (pl.program_id(3) == 0).astype(jnp.int32)
    kbuf[0:KEY_STEPS, :] = kp_ref[...]
    kbuf[KEY_STEPS:, :] = kc_ref[...]
    vbuf[0:KEY_STEPS, :] = vp_ref[...]
    vbuf[KEY_STEPS:, :] = vc_ref[...]
    lane = lax.broadcasted_iota(jnp.int32, (1, LANES), 1)
    lo = lane < HEAD_DIM
    head_mask = (lo.astype(BF16), (~lo).astype(BF16))
    for sub in range(nsub):
        r0 = sub * KEY_STEPS
        qs = q_ref[r0:r0 + KEY_STEPS, :]
        kw = kbuf[r0:r0 + 2 * KEY_STEPS, :]
        vw = vbuf[r0:r0 + 2 * KEY_STEPS, :]
        outs, lses = [], []
        for hh in range(2):
            s = lax.dot_general(qs * head_mask[hh], kw, _NT, preferred_element_type=F32)
            variant = first_blk if sub == 0 else 0
            s = s + bias_ref[variant, pair * 2 + hh]
            m = jnp.max(s, axis=1, keepdims=True)
            p = jnp.exp(s - m)
            l = jnp.sum(p, axis=1, keepdims=True)
            pv = jnp.dot(p.astype(BF16), vw, preferred_element_type=F32)
            outs.append(pv / l)
            lses.append(m + jnp.log(l))
        o_ref[r0:r0 + KEY_STEPS, :] = jnp.where(lo, outs[0], outs[1]).astype(BF16)
        lse_ref[r0:r0 + KEY_STEPS, :] = jnp.where(lo, lses[0], lses[1])


def _attn_group(qn, kn, v, bias, g, dil, batch, seq):
    sub_len = seq // dil
    tq = min(512, sub_len)
    nsub = tq // KEY_STEPS
    pairs_per_row = ATTN_WIDTH // LANES
    view = lambda a: a.reshape(batch, sub_len, dil * ATTN_WIDTH)
    col = lambda r, p: r * pairs_per_row + 2 * g + p
    cur = pl.BlockSpec((None, tq, LANES), lambda b, r, p, i: (b, i, col(r, p)))
    prev = pl.BlockSpec((None, KEY_STEPS, LANES),
                        lambda b, r, p, i: (b, jnp.maximum(i * nsub - 1, 0), col(r, p)))
    out = pl.BlockSpec((None, tq, LANES), lambda b, r, p, i: (b, i, r * 2 + p))
    o, lse = pl.pallas_call(
        functools.partial(_attn_kernel, nsub=nsub),
        grid=(batch, dil, 2, sub_len // tq),
        in_specs=[cur, cur, prev, cur, prev, _full(bias.shape)],
        out_specs=[out, out],
        out_shape=[jax.ShapeDtypeStruct((batch, sub_len, dil * GROUP_WIDTH), BF16),
                   jax.ShapeDtypeStruct((batch, sub_len, dil * GROUP_WIDTH), F32)],
        scratch_shapes=[pltpu.VMEM((tq + KEY_STEPS, LANES), BF16),
                        pltpu.VMEM((tq + KEY_STEPS, LANES), BF16)],
        compiler_params=_params(4),
        name=f"attn_dil{dil}",
    )(view(qn), view(kn), view(kn), view(v), view(v), bias)
    return o.reshape(batch * seq, GROUP_WIDTH), lse.reshape(batch * seq, GROUP_WIDTH)


def _t5_causal_bucket(distance):
    n = distance.astype(jnp.int32)
    max_exact = REL_BUCKETS // 2
    nf = jnp.maximum(n, 1).astype(F32)
    large = max_exact + (jnp.log(nf / max_exact) / math.log(REL_MAX_DISTANCE / max_exact)
                         * (REL_BUCKETS - max_exact)).astype(jnp.int32)
    large = jnp.minimum(large, REL_BUCKETS - 1)
    return jnp.where(n < max_exact, n, large)


def _band_bias(rel_bias, g, dil):
    steps = jnp.arange(KEY_STEPS + 1)
    per_step = rel_bias[_t5_causal_bucket(steps * dil)][:, g * HEADS_PER_GROUP:(g + 1) * HEADS_PER_GROUP].T
    a = np.arange(KEY_STEPS)[:, None]
    c = np.arange(2 * KEY_STEPS)[None, :]
    j = a + KEY_STEPS - c
    valid = (j >= 0) & (j <= KEY_STEPS)
    band = per_step[:, np.clip(j, 0, KEY_STEPS)].astype(F32)
    normal = jnp.where(valid[None], band, NEG_INF)
    at_start = jnp.where((valid & (c >= KEY_STEPS))[None], band, NEG_INF)
    return jnp.stack([normal, at_start])


def _mixer_kernel(u_ref, halo_ref, o0_ref, o1_ref, o2_ref, l0_ref, l1_ref, l2_ref, gate_ref, x_ref,
                  dw_ref, dwb_ref, lng_ref, lnb_ref, wc_ref, bc_ref, wa_ref, wo_ref,
                  h_ref, uext, *, tm, tiles_per_seq):
    d = x_ref.shape[1]
    at_seq_start = pl.program_id(0) % tiles_per_seq == 0
    uext[0:CONV_HALO, :] = jnp.where(at_seq_start, 0.0, halo_ref[...])
    uext[CONV_HALO:, :] = u_ref[...]
    acc = jnp.zeros((tm, CONV_CH), F32)
    for w in range(CONV_WIDTH):
        off = CONV_HALO - (CONV_WIDTH - 1) + w
        acc = acc + uext[off:off + tm, :] * dw_ref[w:w + 1, :]
    c = acc + dwb_ref[...]
    mu = jnp.mean(c, axis=-1, keepdims=True)
    cc = c - mu
    var = jnp.mean(cc * cc, axis=-1, keepdims=True)
    c = cc * lax.rsqrt(var + EPS) * lng_ref[...] + lnb_ref[...]
    c = c * jax.nn.sigmoid(c)
    conv_out = jnp.dot(c.astype(BF16), wc_ref[...], preferred_element_type=F32) + bc_ref[...]

    lses = (l0_ref[...], l1_ref[...], l2_ref[...])
    outs = (o0_ref, o1_ref, o2_ref)
    m = jnp.maximum(jnp.maximum(lses[0], lses[1]), lses[2])
    es = [jnp.exp(l - m) for l in lses]
    inv = 1.0 / (es[0] + es[1] + es[2])
    attn_out = jnp.zeros((tm, d), F32)
    for g in range(3):
        og = (outs[g][...].astype(F32) * (es[g] * inv)).astype(BF16)
        attn_out = attn_out + jnp.dot(og, wa_ref[g * GROUP_WIDTH:(g + 1) * GROUP_WIDTH, :],
                                      preferred_element_type=F32)

    merged = (gate_ref[:, :d].astype(F32) * conv_out + gate_ref[:, d:].astype(F32) * attn_out)
    h_ref[...] = x_ref[...] + jnp.dot(merged.astype(BF16), wo_ref[...], preferred_element_type=F32)


def _mixer(u, outs, lses, gates, x2, dw, dwb, lng, lnb, wc, bc, wa, wo, tm, seq):
    n, d = x2.shape
    row = lambda w: pl.BlockSpec((tm, w), lambda i: (i, 0))
    halo = pl.BlockSpec((CONV_HALO, CONV_CH),
                        lambda i: (jnp.maximum(i * (tm // CONV_HALO) - 1, 0), 0))
    return pl.pallas_call(
        functools.partial(_mixer_kernel, tm=tm, tiles_per_seq=seq // tm),
        grid=(n // tm,),
        in_specs=[row(CONV_CH), halo] + [row(GROUP_WIDTH)] * 6 + [row(2 * d), row(d)]
                 + [_full(a.shape) for a in (dw, dwb, lng, lnb, wc, bc, wa, wo)],
        out_specs=row(d),
        out_shape=jax.ShapeDtypeStruct((n, d), F32),
        scratch_shapes=[pltpu.VMEM((tm + CONV_HALO, CONV_CH), F32)],
        compiler_params=_params(1),
        name="mixer_out",
    )(u, u, *outs, *lses, gates, x2, dw, dwb, lng, lnb, wc, bc, wa, wo)


def _topk_rounds(s_ref, n, tt, on_round):
    iota = lax.broadcasted_iota(jnp.int32, (n, tt), 0).astype(F32)

    def body(r, carry):
        s = s_ref[...]
        m = jnp.max(s, axis=0, keepdims=True)
        idx = jnp.min(jnp.where(s == m, iota, float(n)), axis=0, keepdims=True)
        sel = iota == idx
        s_ref[...] = jnp.where(sel, -jnp.inf, s)
        on_round(r, m, idx, sel)
        return carry

    lax.fori_loop(0, PEER_TOPK, body, 0)


def _peer_prep_kernel(h_ref, g2_ref, wqt_ref, keys_ref,
                      hn_ref, rk2_ref, b_ref, cnt_ref, a_ref,
                      qt_ref, s1_ref, s2_ref, w_ref, rk1_ref, ss1_ref, ss2_ref, cand_ref,
                      best_ref, cnt16_ref, *, tt):
    h = h_ref[...]
    ms = jnp.mean(h * h, axis=-1, keepdims=True)
    hn = (h * lax.rsqrt(ms + EPS) * g2_ref[...]).astype(BF16)
    hn_ref[...] = hn
    qt_ref[...] = lax.dot_general(wqt_ref[...], hn, _NT, preferred_element_type=F32)
    iota16 = lax.broadcasted_iota(jnp.int32, (PEER_TOPK, tt), 0).astype(F32)
    unranked = float(PEER_N_KEYS)

    def head_body(hd, carry):
        def scores(p, dst_ref):
            base = pl.multiple_of((hd * 2 + p) * PEER_N_KEYS, PEER_N_KEYS)
            qhp = qt_ref[pl.ds(base, PEER_N_KEYS), :].astype(BF16)
            dst_ref[...] = jnp.dot(keys_ref[hd, p], qhp, preferred_element_type=F32)

        scores(0, s1_ref)
        w_ref[...] = s1_ref[...]
        rk1_ref[...] = jnp.full((PEER_N_KEYS, tt), unranked, F32)

        def round1(r, m, idx, sel):
            ss1_ref[pl.ds(r, 1), :] = m
            rk1_ref[...] = jnp.where(sel, r.astype(F32), rk1_ref[...])
        _topk_rounds(w_ref, PEER_N_KEYS, tt, round1)

        scores(1, s2_ref)
        w_ref[...] = s2_ref[...]
        rk2_ref[hd] = jnp.full((PEER_N_KEYS, tt), unranked, F32)

        def round2(r, m, idx, sel):
            ss2_ref[pl.ds(r, 1), :] = m
            rk2_ref[hd] = jnp.where(sel, r.astype(F32), rk2_ref[hd])
        _topk_rounds(w_ref, PEER_N_KEYS, tt, round2)

        for r1 in range(PEER_TOPK):
            cand_ref[r1 * PEER_TOPK:(r1 + 1) * PEER_TOPK, :] = ss1_ref[r1:r1 + 1, :] + ss2_ref[...]
        cnt16_ref[...] = jnp.zeros((PEER_TOPK, tt), F32)

        def round3(r, m, idx, sel):
            best_ref[pl.ds(r, 1), :] = m
            row = jnp.floor(idx * (1.0 / PEER_TOPK))
            cnt16_ref[...] = cnt16_ref[...] + (iota16 == row).astype(F32)
        _topk_rounds(cand_ref, PEER_TOPK * PEER_TOPK, tt, round3)

        best = best_ref[...]
        z = jnp.sum(jnp.exp(best - best[0:1, :]), axis=0, keepdims=True)
        a_ref[hd] = jnp.exp(s1_ref[...] - ss1_ref[0:1, :])
        b_ref[hd] = jnp.exp(s2_ref[...] - ss2_ref[0:1, :]) / z
        rk1 = rk1_ref[...]
        cnt = jnp.zeros((PEER_N_KEYS, tt), F32)
        for r1 in range(PEER_TOPK):
            cnt = jnp.where(rk1 == float(r1), cnt16_ref[r1:r1 + 1, :], cnt)
        cnt_ref[hd] = cnt
        return carry

    lax.fori_loop(0, PEER_HEADS, head_body, 0)


def _peer_prep(h2, g2, wqt, keys, tt):
    n, d = h2.shape
    nt = n // tt
    state = pl.BlockSpec((None, PEER_HEADS, PEER_N_KEYS, tt), lambda i: (i, 0, 0, 0))
    state_shape = jax.ShapeDtypeStruct((nt, PEER_HEADS, PEER_N_KEYS, tt), F32)
    kk = PEER_TOPK * PEER_TOPK
    return pl.pallas_call(
        functools.partial(_peer_prep_kernel, tt=tt),
        grid=(nt,),
        in_specs=[pl.BlockSpec((tt, d), lambda i: (i, 0)), _full(g2.shape), _full(wqt.shape),
                  _full(keys.shape)],
        out_specs=[pl.BlockSpec((tt, d), lambda i: (i, 0)), state, state, state, state],
        out_shape=[jax.ShapeDtypeStruct((n, d), BF16)] + [state_shape] * 4,
        scratch_shapes=[pltpu.VMEM((wqt.shape[0], tt), F32),
                        pltpu.VMEM((PEER_N_KEYS, tt), F32),
                        pltpu.VMEM((PEER_N_KEYS, tt), F32),
                        pltpu.VMEM((PEER_N_KEYS, tt), F32),
                        pltpu.VMEM((PEER_N_KEYS, tt), F32),
                        pltpu.VMEM((PEER_TOPK, tt), F32),
                        pltpu.VMEM((PEER_TOPK, tt), F32),
                        pltpu.VMEM((kk, tt), F32),
                        pltpu.VMEM((PEER_TOPK, tt), F32),
                        pltpu.VMEM((PEER_TOPK, tt), F32)],
        compiler_params=_params(1),
        name="peer_prep",
    )(h2, g2, wqt, keys)


def _peer_main_kernel(hn_ref, rk2_ref, b_ref, cnt_ref, a_ref, u_ref, vt_ref, res_ref,
                      y_ref, yt_ref, act_ref, *, keys_per_chunk):
    j = pl.program_id(1)

    @pl.when(j == 0)
    def _():
        yt_ref[...] = jnp.zeros(yt_ref.shape, F32)

    hn = hn_ref[...]
    ht = lax.dot_general(u_ref[...], hn, _NT, preferred_element_type=F32)
    for k in range(keys_per_chunk):
        i1 = j * keys_per_chunk + k
        hk = ht[k * PEER_N_KEYS:(k + 1) * PEER_N_KEYS, :]
        gate = jnp.zeros(hk.shape, F32)
        for hd in range(PEER_HEADS):
            cnt_row = cnt_ref[hd, pl.ds(i1, 1), :]
            a_row = a_ref[hd, pl.ds(i1, 1), :]
            gate = gate + jnp.where(rk2_ref[hd] < cnt_row, b_ref[hd] * a_row, 0.0)
        act = 0.5 * hk * (1.0 + lax.erf(hk * math.sqrt(0.5))) * gate
        act_ref[k * PEER_N_KEYS:(k + 1) * PEER_N_KEYS, :] = act.astype(BF16)
    yt_ref[...] += jnp.dot(vt_ref[...], act_ref[...], preferred_element_type=F32)

    @pl.when(j == pl.num_programs(1) - 1)
    def _():
        y_ref[...] = res_ref[...] + yt_ref[...].T


def _peer_main(hn, rk2, b, cnt, a, u_bf, vt_bf, h2, tt, chunk):
    n, d = h2.shape
    n_exp = u_bf.shape[0]
    state = pl.BlockSpec((None, PEER_HEADS, PEER_N_KEYS, tt), lambda i, j: (i, 0, 0, 0))
    tok = pl.BlockSpec((tt, d), lambda i, j: (i, 0))
    return pl.pallas_call(
        functools.partial(_peer_main_kernel, keys_per_chunk=chunk // PEER_N_KEYS),
        grid=(n // tt, n_exp // chunk),
        in_specs=[tok, state, state, state, state,
                  pl.BlockSpec((chunk, d), lambda i, j: (j, 0)),
                  pl.BlockSpec((d, chunk), lambda i, j: (0, j)),
                  tok],
        out_specs=tok,
        out_shape=jax.ShapeDtypeStruct((n, d), F32),
        scratch_shapes=[pltpu.VMEM((d, tt), F32), pltpu.VMEM((chunk, tt), BF16)],
        compiler_params=_params(2),
        name="peer_main",
    )(hn, rk2, b, cnt, a, u_bf, vt_bf, h2)


def _layer(x2, batch, seq, w_in, conv_b_glu, conv_dw_w, conv_dw_b, conv_ln_g, conv_ln_b, conv_w_proj,
           conv_b_proj, q_norm_g, k_norm_g, rel_bias, attn_w_proj, mix_w_out, norm1_g, norm2_g,
           peer_w_q, peer_sub_keys, peer_u, peer_v):
    n, d = x2.shape
    row = lambda v: v.reshape(1, -1).astype(F32)
    c0 = 2 * CONV_CH
    cuts = [c0, c0 + ATTN_WIDTH, c0 + 2 * ATTN_WIDTH, c0 + 3 * ATTN_WIDTH]
    wglu, wq, wk, wv, wg = [w.astype(BF16) for w in jnp.split(w_in, cuts, axis=-1)]
    head_of = np.arange(ATTN_WIDTH) // HEAD_DIM
    pm = jnp.asarray((head_of[:, None] == head_of[None, :]) / HEAD_DIM, BF16)
    tile_heads = lambda g: jnp.tile(g.astype(F32), N_ATTN_HEADS).reshape(1, ATTN_WIDTH)

    u, qn, kn, v, gates = _inproj(x2, row(norm1_g), wglu, row(conv_b_glu), wq, wk, wv,
                                  tile_heads(q_norm_g), tile_heads(k_norm_g), pm, wg, tm=256)

    outs, lses = [], []
    for g, (_, dil) in enumerate(ATTN_GROUPS):
        o, lse = _attn_group(qn, kn, v, _band_bias(rel_bias, g, dil), g, dil, batch, seq)
        outs.append(o)
        lses.append(lse)

    h2 = _mixer(u, outs, lses, gates, x2, conv_dw_w.reshape(CONV_WIDTH, CONV_CH).astype(F32),
                row(conv_dw_b), row(conv_ln_g), row(conv_ln_b), conv_w_proj.astype(BF16),
                row(conv_b_proj), attn_w_proj.astype(BF16), mix_w_out.astype(BF16),
                tm=min(512, seq), seq=seq)

    tt = min(512, n)
    hn, rk2, b, cnt, a = _peer_prep(h2, row(norm2_g), peer_w_q.T.astype(BF16),
                                    peer_sub_keys.astype(BF16), tt)
    return _peer_main(hn, rk2, b, cnt, a, peer_u.astype(BF16), peer_v.T.astype(BF16), h2, tt,
                      chunk=512)


def kernel(x, w_in, conv_b_glu, conv_dw_w, conv_dw_b, conv_ln_g, conv_ln_b, conv_w_proj, conv_b_proj, q_norm_g, k_norm_g, rel_bias, attn_w_proj, mix_w_out, norm1_g, norm2_g, peer_w_q, peer_sub_keys, peer_u, peer_v):
    batch, seq, d = x.shape
    x2 = x.reshape(batch * seq, d)
    for l in range(w_in.shape[0]):
        x2 = _layer(x2, batch, seq, w_in[l], conv_b_glu[l], conv_dw_w[l], conv_dw_b[l], conv_ln_g[l],
                    conv_ln_b[l], conv_w_proj[l], conv_b_proj[l], q_norm_g[l], k_norm_g[l], rel_bias,
                    attn_w_proj[l], mix_w_out[l], norm1_g[l], norm2_g[l], peer_w_q[l],
                    peer_sub_keys[l], peer_u[l], peer_v[l])
    return x2.reshape(batch, seq, d)
```

```python
import functools
import math

import numpy as np
import jax
import jax.numpy as jnp
from jax import lax
from jax.experimental import pallas as pl
from jax.experimental.pallas import tpu as pltpu

F32 = jnp.float32
BF16 = jnp.bfloat16

HEAD_DIM = 64
HEADS_PER_GROUP = 4
ATTN_GROUPS = ((128, 1), (512, 4), (2048, 16))
N_ATTN_HEADS = HEADS_PER_GROUP * len(ATTN_GROUPS)
ATTN_WIDTH = N_ATTN_HEADS * HEAD_DIM
GROUP_WIDTH = HEADS_PER_GROUP * HEAD_DIM
KEY_STEPS = 128
NEG_INF = -1e30
CONV_CH = 512
CONV_WIDTH = 31
CONV_HALO = 32
REL_BUCKETS = 32
REL_MAX_DISTANCE = 2048
PEER_HEADS = 8
PEER_N_KEYS = 128
PEER_TOPK = 16
EPS = 1e-6

LANES = 128
MXU_COLS = 256
VMEM_LIMIT = 56 * 1024 * 1024

_NT = (((1,), (1,)), ((), ()))


def _params(n_axes):
    return pltpu.CompilerParams(dimension_semantics=("arbitrary",) * n_axes,
                                vmem_limit_bytes=VMEM_LIMIT)


def _full(shape):
    n = len(shape)
    return pl.BlockSpec(shape, lambda *_: (0,) * n)


def _inproj_kernel(x_ref, g1_ref, wglu_ref, bglu_ref, wq_ref, wk_ref, wv_ref, qg_ref, kg_ref,
                   pm_ref, wg_ref, u_ref, q_ref, k_ref, v_ref, gate_ref):
    x = x_ref[...]
    ms = jnp.mean(x * x, axis=-1, keepdims=True)
    xn = (x * lax.rsqrt(ms + EPS) * g1_ref[...]).astype(BF16)

    glu = jnp.dot(xn, wglu_ref[...], preferred_element_type=F32) + bglu_ref[...]
    u_ref[...] = glu[:, :CONV_CH] * jax.nn.sigmoid(glu[:, CONV_CH:])

    def head_rmsnorm(w_ref, g_ref):
        y = jnp.dot(xn, w_ref[...], preferred_element_type=F32)
        msq = jnp.dot((y * y).astype(BF16), pm_ref[...], preferred_element_type=F32)
        return y * lax.rsqrt(msq + EPS) * g_ref[...]

    q_ref[...] = (head_rmsnorm(wq_ref, qg_ref) * (HEAD_DIM ** -0.5)).astype(BF16)
    k_ref[...] = head_rmsnorm(wk_ref, kg_ref).astype(BF16)
    v_ref[...] = jnp.dot(xn, wv_ref[...], preferred_element_type=F32).astype(BF16)
    gate_ref[...] = jax.nn.sigmoid(
        jnp.dot(xn, wg_ref[...], preferred_element_type=F32)).astype(BF16)


def _inproj(x2, g1, wglu, bglu, wq, wk, wv, qg, kg, pm, wg, tm):
    n, d = x2.shape
    row = lambda w: pl.BlockSpec((tm, w), lambda i: (i, 0))
    return pl.pallas_call(
        _inproj_kernel,
        grid=(n // tm,),
        in_specs=[row(d), _full(g1.shape), _full(wglu.shape), _full(bglu.shape), _full(wq.shape),
                  _full(wk.shape), _full(wv.shape), _full(qg.shape), _full(kg.shape),
                  _full(pm.shape), _full(wg.shape)],
        out_specs=[row(CONV_CH), row(ATTN_WIDTH), row(ATTN_WIDTH), row(ATTN_WIDTH), row(2 * d)],
        out_shape=[jax.ShapeDtypeStruct((n, CONV_CH), F32),
                   jax.ShapeDtypeStruct((n, ATTN_WIDTH), BF16),
                   jax.ShapeDtypeStruct((n, ATTN_WIDTH), BF16),
                   jax.ShapeDtypeStruct((n, ATTN_WIDTH), BF16),
                   jax.ShapeDtypeStruct((n, 2 * d), BF16)],
        compiler_params=_params(1),
        name="inproj",
    )(x2, g1, wglu, bglu, wq, wk, wv, qg, kg, pm, wg)


def _attn_kernel(q_ref, kc_ref, kp_ref, vc_ref, vp_ref, bias_ref, o_ref, lse_ref, kbuf, vbuf,
                 *, nsub):
    pair = pl.program_id(2)
    first_blk = (pl.program_id(3) == 0).astype(jnp.int32)
    kbuf[0:KEY_STEPS, :] = kp_ref[...]
    kbuf[KEY_STEPS:, :] = kc_ref[...]
    vbuf[0:KEY_STEPS, :] = vp_ref[...]
    vbuf[KEY_STEPS:, :] = vc_ref[...]
    lane = lax.broadcasted_iota(jnp.int32, (1, LANES), 1)
    lo = lane < HEAD_DIM
    head_mask = (lo.astype(BF16), (~lo).astype(BF16))
    for sub in range(nsub):
        r0 = sub * KEY_STEPS
        qs = q_ref[r0:r0 + KEY_STEPS, :]
        kw = kbuf[r0:r0 + 2 * KEY_STEPS, :]
        vw = vbuf[r0:r0 + 2 * KEY_STEPS, :]
        outs, lses = [], []
        for hh in range(2):
            s = lax.dot_general(qs * head_mask[hh], kw, _NT, preferred_element_type=F32)
            variant = first_blk if sub == 0 else 0
            s = s + bias_ref[variant, pair * 2 + hh]
            m = jnp.max(s, axis=1, keepdims=True)
            p = jnp.exp(s - m)
            l = jnp.sum(p, axis=1, keepdims=True)
            pv = jnp.dot(p.astype(BF16), vw, preferred_element_type=F32)
            outs.append(pv / l)
            lses.append(m + jnp.log(l))
        o_ref[r0:r0 + KEY_STEPS, :] = jnp.where(lo, outs[0], outs[1]).astype(BF16)
        lse_ref[r0:r0 + KEY_STEPS, :] = jnp.where(lo, lses[0], lses[1])


def _attn_group(qn, kn, v, bias, g, dil, batch, seq):
    sub_len = seq // dil
    tq = min(512, sub_len)
    nsub = tq // KEY_STEPS
    pairs_per_row = ATTN_WIDTH // LANES
    view = lambda a: a.reshape(batch, sub_len, dil * ATTN_WIDTH)
    col = lambda r, p: r * pairs_per_row + 2 * g + p
    cur = pl.BlockSpec((None, tq, LANES), lambda b, r, p, i: (b, i, col(r, p)))
    prev = pl.BlockSpec((None, KEY_STEPS, LANES),
                        lambda b, r, p, i: (b, jnp.maximum(i * nsub - 1, 0), col(r, p)))
    out = pl.BlockSpec((None, tq, LANES), lambda b, r, p, i: (b, i, r * 2 + p))
    o, lse = pl.pallas_call(
        functools.partial(_attn_kernel, nsub=nsub),
        grid=(batch, dil, 2, sub_len // tq),
        in_specs=[cur, cur, prev, cur, prev, _full(bias.shape)],
        out_specs=[out, out],
        out_shape=[jax.ShapeDtypeStruct((batch, sub_len, dil * GROUP_WIDTH), BF16),
                   jax.ShapeDtypeStruct((batch, sub_len, dil * GROUP_WIDTH), F32)],
        scratch_shapes=[pltpu.VMEM((tq + KEY_STEPS, LANES), BF16),
                        pltpu.VMEM((tq + KEY_STEPS, LANES), BF16)],
        compiler_params=_params(4),
        name=f"attn_dil{dil}",
    )(view(qn), view(kn), view(kn), view(v), view(v), bias)
    return o.reshape(batch * seq, GROUP_WIDTH), lse.reshape(batch * seq, GROUP_WIDTH)


def _t5_causal_bucket(distance):
    n = distance.astype(jnp.int32)
    max_exact = REL_BUCKETS // 2
    nf = jnp.maximum(n, 1).astype(F32)
    large = max_exact + (jnp.log(nf / max_exact) / math.log(REL_MAX_DISTANCE / max_exact)
                         * (REL_BUCKETS - max_exact)).astype(jnp.int32)
    large = jnp.minimum(large, REL_BUCKETS - 1)
    return jnp.where(n < max_exact, n, large)


def _band_bias(rel_bias, g, dil):
    steps = jnp.arange(KEY_STEPS + 1)
    per_step = rel_bias[_t5_causal_bucket(steps * dil)][:, g * HEADS_PER_GROUP:(g + 1) * HEADS_PER_GROUP].T
    a = np.arange(KEY_STEPS)[:, None]
    c = np.arange(2 * KEY_STEPS)[None, :]
    j = a + KEY_STEPS - c
    valid = (j >= 0) & (j <= KEY_STEPS)
    band = per_step[:, np.clip(j, 0, KEY_STEPS)].astype(F32)
    normal = jnp.where(valid[None], band, NEG_INF)
    at_start = jnp.where((valid & (c >= KEY_STEPS))[None], band, NEG_INF)
    return jnp.stack([normal, at_start])


def _mixer_kernel(u_ref, halo_ref, o0_ref, o1_ref, o2_ref, l0_ref, l1_ref, l2_ref, gate_ref, x_ref,
                  dw_ref, dwb_ref, lng_ref, lnb_ref, wc_ref, bc_ref, wa_ref, wo_ref,
                  h_ref, uext, *, tm, tiles_per_seq):
    d = x_ref.shape[1]
    at_seq_start = pl.program_id(0) % tiles_per_seq == 0
    uext[0:CONV_HALO, :] = jnp.where(at_seq_start, 0.0, halo_ref[...])
    uext[CONV_HALO:, :] = u_ref[...]
    acc = jnp.zeros((tm, CONV_CH), F32)
    for w in range(CONV_WIDTH):
        off = CONV_HALO - (CONV_WIDTH - 1) + w
        acc = acc + uext[off:off + tm, :] * dw_ref[w:w + 1, :]
    c = acc + dwb_ref[...]
    mu = jnp.mean(c, axis=-1, keepdims=True)
    cc = c - mu
    var = jnp.mean(cc * cc, axis=-1, keepdims=True)
    c = cc * lax.rsqrt(var + EPS) * lng_ref[...] + lnb_ref[...]
    c = c * jax.nn.sigmoid(c)
    conv_out = jnp.dot(c.astype(BF16), wc_ref[...], preferred_element_type=F32) + bc_ref[...]

    lses = (l0_ref[...], l1_ref[...], l2_ref[...])
    outs = (o0_ref, o1_ref, o2_ref)
    m = jnp.maximum(jnp.maximum(lses[0], lses[1]), lses[2])
    es = [jnp.exp(l - m) for l in lses]
    inv = 1.0 / (es[0] + es[1] + es[2])
    attn_out = jnp.zeros((tm, d), F32)
    for g in range(3):
        og = (outs[g][...].astype(F32) * (es[g] * inv)).astype(BF16)
        attn_out = attn_out + jnp.dot(og, wa_ref[g * GROUP_WIDTH:(g + 1) * GROUP_WIDTH, :],
                                      preferred_element_type=F32)

    merged = (gate_ref[:, :d].astype(F32) * conv_out + gate_ref[:, d:].astype(F32) * attn_out)
    h_ref[...] = x_ref[...] + jnp.dot(merged.astype(BF16), wo_ref[...], preferred_element_type=F32)


def _mixer(u, outs, lses, gates, x2, dw, dwb, lng, lnb, wc, bc, wa, wo, tm, seq):
    n, d = x2.shape
    row = lambda w: pl.BlockSpec((tm, w), lambda i: (i, 0))
    halo = pl.BlockSpec((CONV_HALO, CONV_CH),
                        lambda i: (jnp.maximum(i * (tm // CONV_HALO) - 1, 0), 0))
    return pl.pallas_call(
        functools.partial(_mixer_kernel, tm=tm, tiles_per_seq=seq // tm),
        grid=(n // tm,),
        in_specs=[row(CONV_CH), halo] + [row(GROUP_WIDTH)] * 6 + [row(2 * d), row(d)]
                 + [_full(a.shape) for a in (dw, dwb, lng, lnb, wc, bc, wa, wo)],
        out_specs=row(d),
        out_shape=jax.ShapeDtypeStruct((n, d), F32),
        scratch_shapes=[pltpu.VMEM((tm + CONV_HALO, CONV_CH), F32)],
        compiler_params=_params(1),
        name="mixer_out",
    )(u, u, *outs, *lses, gates, x2, dw, dwb, lng, lnb, wc, bc, wa, wo)


_CAND_WIDTHS = tuple(PEER_TOPK // (r1 + 1) for r1 in range(PEER_TOPK))
_CAND_BLOCK_ROWS = 8
_UNRANKED = float(PEER_N_KEYS)


def _pack_rows(x):
    return pltpu.bitcast(x, jnp.uint32)


def _unpack_rows(x):
    return pltpu.bitcast(x, BF16)


def _extract_top(s, key_id, on_round):
    for r in range(PEER_TOPK):
        m = jnp.max(s, axis=0, keepdims=True)
        idx = jnp.min(jnp.where(s == m, key_id, 1e9), axis=0, keepdims=True)
        sel = key_id == idx
        s = jnp.where(sel, -jnp.inf, s)
        on_round(r, m, idx, sel)


def _rank_keys(s, key_id, iota16):
    state = [jnp.zeros(iota16.shape, F32), jnp.full(s.shape, _UNRANKED, F32)]

    def on_round(r, m, idx, sel):
        state[0] = jnp.where(iota16 == float(r), m, state[0])
        state[1] = jnp.where(sel, float(r), state[1])

    _extract_top(s, key_id, on_round)
    return state[0], state[1]


def _peer_prep_kernel(h_ref, g2_ref, wqt_ref, keys_ref,
                      hnt_ref, rk2_ref, b_ref, cnt_ref, a_ref,
                      qt_ref, s1_ref, s2_ref, *, tt):
    h = h_ref[...]
    ms = jnp.mean(h * h, axis=-1, keepdims=True)
    hnt = (h * lax.rsqrt(ms + EPS) * g2_ref[...]).T.astype(BF16)
    hnt_ref[...] = _pack_rows(hnt)
    qt_ref[...] = jnp.dot(wqt_ref[...], hnt, preferred_element_type=F32)

    key_id = lax.broadcasted_iota(jnp.int32, (PEER_N_KEYS, LANES), 0).astype(F32)
    iota16 = lax.broadcasted_iota(jnp.int32, (PEER_TOPK, LANES), 0).astype(F32)
    iota8 = lax.broadcasted_iota(jnp.int32, (_CAND_BLOCK_ROWS, LANES), 0).astype(F32)
    cand_id = jnp.concatenate(
        [iota16]
        + [iota8 + float(r1 * PEER_TOPK) for r1 in range(1, _CAND_BLOCK_ROWS)]
        + [(iota8 + float(_CAND_BLOCK_ROWS)) * float(PEER_TOPK)], axis=0)

    def head_body(hd, carry):
        for p, dst_ref in ((0, s1_ref), (1, s2_ref)):
            base = pl.multiple_of((hd * 2 + p) * PEER_N_KEYS, PEER_N_KEYS)
            qhp = qt_ref[pl.ds(base, PEER_N_KEYS), :].astype(BF16)
            dst_ref[...] = jnp.dot(keys_ref[hd, p], qhp, preferred_element_type=F32)

        def lane_group(c, carry2):
            lanes = pl.ds(pl.multiple_of(c * LANES, LANES), LANES)
            s1 = s1_ref[:, lanes]
            s2 = s2_ref[:, lanes]
            ss1, rk1 = _rank_keys(s1, key_id, iota16)
            ss2, rk2 = _rank_keys(s2, key_id, iota16)

            blocks = [ss1[0:1, :] + ss2]
            for r1 in range(1, _CAND_BLOCK_ROWS):
                blk = ss1[r1:r1 + 1, :] + ss2[0:_CAND_BLOCK_ROWS, :]
                blocks.append(jnp.where(iota8 < float(_CAND_WIDTHS[r1]), blk, -jnp.inf))
            blocks.append(ss1[_CAND_BLOCK_ROWS:, :] + ss2[0:1, :])
            cand = jnp.concatenate(blocks, axis=0)

            state = [jnp.zeros((PEER_TOPK, LANES), F32), jnp.zeros((PEER_TOPK, LANES), F32)]

            def on_round(r, m, idx, sel):
                state[0] = jnp.where(iota16 == float(r), m, state[0])
                row = jnp.floor(idx * (1.0 / PEER_TOPK))
                state[1] = state[1] + (iota16 == row).astype(F32)

            _extract_top(cand, cand_id, on_round)
            best, cnt16 = state
            z = jnp.sum(jnp.exp(best - best[0:1, :]), axis=0, keepdims=True)

            a_ref[hd, :, lanes] = jnp.exp(s1 - ss1[0:1, :])
            b_ref[hd, :, lanes] = _pack_rows((jnp.exp(s2 - ss2[0:1, :]) / z).astype(BF16))
            rk2_ref[hd, :, lanes] = _pack_rows(rk2.astype(BF16))
            cnt = jnp.zeros((PEER_N_KEYS, LANES), F32)
            for r1 in range(PEER_TOPK):
                cnt = jnp.where(rk1 == float(r1), cnt16[r1:r1 + 1, :], cnt)
            cnt_ref[hd, :, lanes] = cnt
            return carry2

        lax.fori_loop(0, tt // LANES, lane_group, 0)
        return carry

    lax.fori_loop(0, PEER_HEADS, head_body, 0)


def _peer_prep(h2, g2, wqt, keys, tt):
    n, d = h2.shape
    nt = n // tt
    state = lambda rows: pl.BlockSpec((None, PEER_HEADS, rows, tt), lambda i: (i, 0, 0, 0))
    shape = lambda rows, dt: jax.ShapeDtypeStruct((nt, PEER_HEADS, rows, tt), dt)
    half = PEER_N_KEYS // 2
    return pl.pallas_call(
        functools.partial(_peer_prep_kernel, tt=tt),
        grid=(nt,),
        in_specs=[pl.BlockSpec((tt, d), lambda i: (i, 0)), _full(g2.shape), _full(wqt.shape),
                  _full(keys.shape)],
        out_specs=[pl.BlockSpec((d // 2, tt), lambda i: (0, i)), state(half), state(half),
                   state(PEER_N_KEYS), state(PEER_N_KEYS)],
        out_shape=[jax.ShapeDtypeStruct((d // 2, n), jnp.uint32), shape(half, jnp.uint32),
                   shape(half, jnp.uint32), shape(PEER_N_KEYS, F32), shape(PEER_N_KEYS, F32)],
        scratch_shapes=[pltpu.VMEM((wqt.shape[0], tt), F32),
                        pltpu.VMEM((PEER_N_KEYS, tt), F32),
                        pltpu.VMEM((PEER_N_KEYS, tt), F32)],
        compiler_params=_params(1),
        name="peer_prep",
    )(h2, g2, wqt, keys)


def _peer_main_kernel(hnt_ref, rk2_ref, b_ref, cnt_ref, a_ref, u_ref, vt_ref, res_ref,
                      y_ref, yt_ref, ht0_ref, ht1_ref, act0_ref, act1_ref, rows_ref,
                      *, keys_per_chunk, tt, n_chunks):
    j = pl.program_id(1)

    @pl.when(j == 0)
    def _():
        yt_ref[...] = jnp.zeros(yt_ref.shape, F32)
        for ref in (ht0_ref, ht1_ref):
            ref[...] = jnp.zeros(ref.shape, F32)
        for ref in (act0_ref, act1_ref):
            ref[...] = jnp.zeros(ref.shape, jnp.uint32)

    half = PEER_N_KEYS // 2
    chunk_prev = jnp.clip(j - 1, 0, n_chunks - 1)

    def gate_block(c, ht_old, act_new):
        lanes = slice(c * LANES, (c + 1) * LANES)
        for k0 in range(0, keys_per_chunk, 2):
            ks = range(k0, min(k0 + 2, keys_per_chunk))
            gates = {k: jnp.zeros((PEER_N_KEYS, LANES), BF16) for k in ks}
            for hd in range(PEER_HEADS):
                rk2 = _unpack_rows(rk2_ref[hd, :, lanes])
                b = _unpack_rows(b_ref[hd, :, lanes])
                for k in ks:
                    cnt_row = rows_ref[k, hd:hd + 1, lanes].astype(BF16)
                    a_row = rows_ref[k, PEER_HEADS + hd:PEER_HEADS + hd + 1, lanes].astype(BF16)
                    gates[k] = gates[k] + jnp.where(rk2 < cnt_row, b * a_row, 0.0)
            for k in ks:
                x = ht_old[k * PEER_N_KEYS:(k + 1) * PEER_N_KEYS, lanes]
                gelu = 0.5 * x * (1.0 + lax.erf(x * math.sqrt(0.5)))
                act_new[k * half:(k + 1) * half, lanes] = _pack_rows(gelu.astype(BF16) * gates[k])

    def step(ht_new, ht_old, act_new, act_old):
        for k in range(keys_per_chunk):
            i1 = chunk_prev * keys_per_chunk + k
            for hd in range(PEER_HEADS):
                rows_ref[k, hd:hd + 1, :] = cnt_ref[hd, pl.ds(i1, 1), :]
                rows_ref[k, PEER_HEADS + hd:PEER_HEADS + hd + 1, :] = a_ref[hd, pl.ds(i1, 1), :]
        lanes_per_piece = MXU_COLS // LANES
        for piece in range(tt // MXU_COLS):
            cols = slice(piece * MXU_COLS, (piece + 1) * MXU_COLS)
            ht_new[:, cols] = jnp.dot(_unpack_rows(u_ref[...]), _unpack_rows(hnt_ref[:, cols]),
                                      preferred_element_type=F32)
            gate_block(piece * lanes_per_piece, ht_old, act_new)
            yt_ref[:, cols] += jnp.dot(_unpack_rows(vt_ref[...]), _unpack_rows(act_old[:, cols]),
                                       preferred_element_type=F32)
            for c in range(piece * lanes_per_piece + 1, (piece + 1) * lanes_per_piece):
                gate_block(c, ht_old, act_new)

    @pl.when(j % 2 == 0)
    def _():
        step(ht0_ref, ht1_ref, act1_ref, act0_ref)

    @pl.when(j % 2 == 1)
    def _():
        step(ht1_ref, ht0_ref, act0_ref, act1_ref)

    @pl.when(j == pl.num_programs(1) - 1)
    def _():
        y_ref[...] = res_ref[...] + yt_ref[...].T


def _peer_main(hnt, rk2, b, cnt, a, u_packed, vt_packed, h2, tt, chunk):
    n, d = h2.shape
    n_chunks = 2 * u_packed.shape[0] // chunk
    kpc = chunk // PEER_N_KEYS
    state = lambda a: pl.BlockSpec((None,) + a.shape[1:], lambda i, j: (i, 0, 0, 0))
    tok = pl.BlockSpec((tt, d), lambda i, j: (i, 0))
    return pl.pallas_call(
        functools.partial(_peer_main_kernel, keys_per_chunk=kpc, tt=tt, n_chunks=n_chunks),
        grid=(n // tt, n_chunks + 2),
        in_specs=[pl.BlockSpec((d // 2, tt), lambda i, j: (0, i)), state(rk2), state(b),
                  state(cnt), state(a),
                  pl.BlockSpec((chunk // 2, d), lambda i, j: (jnp.minimum(j, n_chunks - 1), 0)),
                  pl.BlockSpec((d // 2, chunk),
                               lambda i, j: (0, jnp.clip(j - 2, 0, n_chunks - 1))),
                  tok],
        out_specs=tok,
        out_shape=jax.ShapeDtypeStruct((n, d), F32),
        scratch_shapes=[pltpu.VMEM((d, tt), F32),
                        pltpu.VMEM((chunk, tt), F32),
                        pltpu.VMEM((chunk, tt), F32),
                        pltpu.VMEM((chunk // 2, tt), jnp.uint32),
                        pltpu.VMEM((chunk // 2, tt), jnp.uint32),
                        pltpu.VMEM((kpc, 2 * PEER_HEADS, tt), F32)],
        compiler_params=_params(2),
        name="peer_main",
    )(hnt, rk2, b, cnt, a, u_packed, vt_packed, h2)


def _pack_rows_xla(w):
    bits = lax.bitcast_convert_type(w.astype(BF16), jnp.uint16).astype(jnp.uint32)
    return bits[0::2] | (bits[1::2] << 16)


def _layer(x2, batch, seq, w_in, conv_b_glu, conv_dw_w, conv_dw_b, conv_ln_g, conv_ln_b, conv_w_proj,
           conv_b_proj, q_norm_g, k_norm_g, rel_bias, attn_w_proj, mix_w_out, norm1_g, norm2_g,
           peer_w_q, peer_sub_keys, peer_u, peer_v):
    n, d = x2.shape
    row = lambda v: v.reshape(1, -1).astype(F32)
    c0 = 2 * CONV_CH
    cuts = [c0, c0 + ATTN_WIDTH, c0 + 2 * ATTN_WIDTH, c0 + 3 * ATTN_WIDTH]
    wglu, wq, wk, wv, wg = [w.astype(BF16) for w in jnp.split(w_in, cuts, axis=-1)]
    head_of = np.arange(ATTN_WIDTH) // HEAD_DIM
    pm = jnp.asarray((head_of[:, None] == head_of[None, :]) / HEAD_DIM, BF16)
    tile_heads = lambda g: jnp.tile(g.astype(F32), N_ATTN_HEADS).reshape(1, ATTN_WIDTH)

    u, qn, kn, v, gates = _inproj(x2, row(norm1_g), wglu, row(conv_b_glu), wq, wk, wv,
                                  tile_heads(q_norm_g), tile_heads(k_norm_g), pm, wg, tm=256)

    outs, lses = [], []
    for g, (_, dil) in enumerate(ATTN_GROUPS):
        o, lse = _attn_group(qn, kn, v, _band_bias(rel_bias, g, dil), g, dil, batch, seq)
        outs.append(o)
        lses.append(lse)

    h2 = _mixer(u, outs, lses, gates, x2, conv_dw_w.reshape(CONV_WIDTH, CONV_CH).astype(F32),
                row(conv_dw_b), row(conv_ln_g), row(conv_ln_b), conv_w_proj.astype(BF16),
                row(conv_b_proj), attn_w_proj.astype(BF16), mix_w_out.astype(BF16),
                tm=min(512, seq), seq=seq)

    tt = min(512, n)
    hnt, rk2, b, cnt, a = _peer_prep(h2, row(norm2_g), peer_w_q.T.astype(BF16),
                                     peer_sub_keys.astype(BF16), tt)
    return _peer_main(hnt, rk2, b, cnt, a, _pack_rows_xla(peer_u), _pack_rows_xla(peer_v.T), h2, tt,
                      chunk=512)


def kernel(x, w_in, conv_b_glu, conv_dw_w, conv_dw_b, conv_ln_g, conv_ln_b, conv_w_proj, conv_b_proj, q_norm_g, k_norm_g, rel_bias, attn_w_proj, mix_w_out, norm1_g, norm2_g, peer_w_q, peer_sub_keys, peer_u, peer_v):
    batch, seq, d = x.shape
    x2 = x.reshape(batch * seq, d)
    for l in range(w_in.shape[0]):
        x2 = _layer(x2, batch, seq, w_in[l], conv_b_glu[l], conv_dw_w[l], conv_dw_b[l], conv_ln_g[l],
                    conv_ln_b[l], conv_w_proj[l], conv_b_proj[l], q_norm_g[l], k_norm_g[l], rel_bias,
                    attn_w_proj[l], mix_w_out[l], norm1_g[l], norm2_g[l], peer_w_q[l],
                    peer_sub_keys[l], peer_u[l], peer_v[l])
    return x2.reshape(batch, seq, d)
```

```python
import functools
import math

import numpy as np
import jax
import jax.numpy as jnp
from jax import lax
from jax.experimental import pallas as pl
from jax.experimental.pallas import tpu as pltpu

F32 = jnp.float32
BF16 = jnp.bfloat16

HEAD_DIM = 64
HEADS_PER_GROUP = 4
ATTN_GROUPS = ((128, 1), (512, 4), (2048, 16))
N_ATTN_HEADS = HEADS_PER_GROUP * len(ATTN_GROUPS)
ATTN_WIDTH = N_ATTN_HEADS * HEAD_DIM
GROUP_WIDTH = HEADS_PER_GROUP * HEAD_DIM
KEY_STEPS = 128
NEG_INF = -1e30
CONV_CH = 512
CONV_WIDTH = 31
CONV_HALO = 32
REL_BUCKETS = 32
REL_MAX_DISTANCE = 2048
PEER_HEADS = 8
PEER_N_KEYS = 128
PEER_TOPK = 16
EPS = 1e-6

LANES = 128
MXU_COLS = 256
VMEM_LIMIT = 56 * 1024 * 1024

_NT = (((1,), (1,)), ((), ()))


def _params(n_axes):
    return pltpu.CompilerParams(dimension_semantics=("arbitrary",) * n_axes,
                                vmem_limit_bytes=VMEM_LIMIT)


def _full(shape):
    n = len(shape)
    return pl.BlockSpec(shape, lambda *_: (0,) * n)


def _inproj_kernel(x_ref, g1_ref, wglu_ref, bglu_ref, wq_ref, wk_ref, wv_ref, qg_ref, kg_ref,
                   pm_ref, wg_ref, u_ref, q_ref, k_ref, v_ref, gate_ref):
    x = x_ref[...]
    ms = jnp.mean(x * x, axis=-1, keepdims=True)
    xn = (x * lax.rsqrt(ms + EPS) * g1_ref[...]).astype(BF16)

    glu = jnp.dot(xn, wglu_ref[...], preferred_element_type=F32) + bglu_ref[...]
    u_ref[...] = glu[:, :CONV_CH] * jax.nn.sigmoid(glu[:, CONV_CH:])

    def head_rmsnorm(w_ref, g_ref):
        y = jnp.dot(xn, w_ref[...], preferred_element_type=F32)
        msq = jnp.dot((y * y).astype(BF16), pm_ref[...], preferred_element_type=F32)
        return y * lax.rsqrt(msq + EPS) * g_ref[...]

    q_ref[...] = (head_rmsnorm(wq_ref, qg_ref) * (HEAD_DIM ** -0.5)).astype(BF16)
    k_ref[...] = head_rmsnorm(wk_ref, kg_ref).astype(BF16)
    v_ref[...] = jnp.dot(xn, wv_ref[...], preferred_element_type=F32).astype(BF16)
    gate_ref[...] = jax.nn.sigmoid(
        jnp.dot(xn, wg_ref[...], preferred_element_type=F32)).astype(BF16)


def _inproj(x2, g1, wglu, bglu, wq, wk, wv, qg, kg, pm, wg, tm):
    n, d = x2.shape
    row = lambda w: pl.BlockSpec((tm, w), lambda i: (i, 0))
    return pl.pallas_call(
        _inproj_kernel,
        grid=(n // tm,),
        in_specs=[row(d), _full(g1.shape), _full(wglu.shape), _full(bglu.shape), _full(wq.shape),
                  _full(wk.shape), _full(wv.shape), _full(qg.shape), _full(kg.shape),
                  _full(pm.shape), _full(wg.shape)],
        out_specs=[row(CONV_CH), row(ATTN_WIDTH), row(ATTN_WIDTH), row(ATTN_WIDTH), row(2 * d)],
        out_shape=[jax.ShapeDtypeStruct((n, CONV_CH), F32),
                   jax.ShapeDtypeStruct((n, ATTN_WIDTH), BF16),
                   jax.ShapeDtypeStruct((n, ATTN_WIDTH), BF16),
                   jax.ShapeDtypeStruct((n, ATTN_WIDTH), BF16),
                   jax.ShapeDtypeStruct((n, 2 * d), BF16)],
        compiler_params=_params(1),
        name="inproj",
    )(x2, g1, wglu, bglu, wq, wk, wv, qg, kg, pm, wg)


def _attn_kernel(q_ref, kc_ref, kp_ref, vc_ref, vp_ref, bias_ref, o_ref, lse_ref, kbuf, vbuf,
                 *, nsub):
    pair = pl.program_id(2)
    first_blk = (pl.program_id(3) == 0).astype(jnp.int32)
    kbuf[0:KEY_STEPS, :] = kp_ref[...]
    kbuf[KEY_STEPS:, :] = kc_ref[...]
    vbuf[0:KEY_STEPS, :] = vp_ref[...]
    vbuf[KEY_STEPS:, :] = vc_ref[...]
    lane = lax.broadcasted_iota(jnp.int32, (1, LANES), 1)
    lo = lane < HEAD_DIM
    head_mask = (lo.astype(BF16), (~lo).astype(BF16))
    for sub in range(nsub):
        r0 = sub * KEY_STEPS
        qs = q_ref[r0:r0 + KEY_STEPS, :]
        kw = kbuf[r0:r0 + 2 * KEY_STEPS, :]
        vw = vbuf[r0:r0 + 2 * KEY_STEPS, :]
        outs, lses = [], []
        for hh in range(2):
            s = lax.dot_general(qs * head_mask[hh], kw, _NT, preferred_element_type=F32)
            variant = first_blk if sub == 0 else 0
            s = s + bias_ref[variant, pair * 2 + hh]
            m = jnp.max(s, axis=1, keepdims=True)
            p = jnp.exp(s - m)
            l = jnp.sum(p, axis=1, keepdims=True)
            pv = jnp.dot(p.astype(BF16), vw, preferred_element_type=F32)
            outs.append(pv / l)
            lses.append(m + jnp.log(l))
        o_ref[r0:r0 + KEY_STEPS, :] = jnp.where(lo, outs[0], outs[1]).astype(BF16)
        lse_ref[r0:r0 + KEY_STEPS, :] = jnp.where(lo, lses[0], lses[1])


def _attn_group(qn, kn, v, bias, g, dil, batch, seq):
    sub_len = seq // dil
    tq = min(512, sub_len)
    nsub = tq // KEY_STEPS
    pairs_per_row = ATTN_WIDTH // LANES
    view = lambda a: a.reshape(batch, sub_len, dil * ATTN_WIDTH)
    col = lambda r, p: r * pairs_per_row + 2 * g + p
    cur = pl.BlockSpec((None, tq, LANES), lambda b, r, p, i: (b, i, col(r, p)))
    prev = pl.BlockSpec((None, KEY_STEPS, LANES),
                        lambda b, r, p, i: (b, jnp.maximum(i * nsub - 1, 0), col(r, p)))
    out = pl.BlockSpec((None, tq, LANES), lambda b, r, p, i: (b, i, r * 2 + p))
    o, lse = pl.pallas_call(
        functools.partial(_attn_kernel, nsub=nsub),
        grid=(batch, dil, 2, sub_len // tq),
        in_specs=[cur, cur, prev, cur, prev, _full(bias.shape)],
        out_specs=[out, out],
        out_shape=[jax.ShapeDtypeStruct((batch, sub_len, dil * GROUP_WIDTH), BF16),
                   jax.ShapeDtypeStruct((batch, sub_len, dil * GROUP_WIDTH), F32)],
        scratch_shapes=[pltpu.VMEM((tq + KEY_STEPS, LANES), BF16),
                        pltpu.VMEM((tq + KEY_STEPS, LANES), BF16)],
        compiler_params=_params(4),
        name=f"attn_dil{dil}",
    )(view(qn), view(kn), view(kn), view(v), view(v), bias)
    return o.reshape(batch * seq, GROUP_WIDTH), lse.reshape(batch * seq, GROUP_WIDTH)


def _t5_causal_bucket(distance):
    n = distance.astype(jnp.int32)
    max_exact = REL_BUCKETS // 2
    nf = jnp.maximum(n, 1).astype(F32)
    large = max_exact + (jnp.log(nf / max_exact) / math.log(REL_MAX_DISTANCE / max_exact)
                         * (REL_BUCKETS - max_exact)).astype(jnp.int32)
    large = jnp.minimum(large, REL_BUCKETS - 1)
    return jnp.where(n < max_exact, n, large)


def _band_bias(rel_bias, g, dil):
    steps = jnp.arange(KEY_STEPS + 1)
    per_step = rel_bias[_t5_causal_bucket(steps * dil)][:, g * HEADS_PER_GROUP:(g + 1) * HEADS_PER_GROUP].T
    k = KEY_STEPS
    pad = lambda w: jnp.full((HEADS_PER_GROUP, w), NEG_INF, F32)
    f = jnp.concatenate([pad(k - 1), per_step[:, ::-1].astype(F32), pad(k)], axis=1)
    width = f.shape[1]
    normal = jnp.tile(f, (1, k))[:, :k * (width - 1)].reshape(HEADS_PER_GROUP, k, width - 1)
    normal = normal[:, :, k - 1:k - 1 + 2 * k]
    before_start = np.arange(2 * k)[None, None, :] < k
    at_start = jnp.where(before_start, NEG_INF, normal)
    return jnp.stack([normal, at_start])


def _mixer_kernel(u_ref, halo_ref, o0_ref, o1_ref, o2_ref, l0_ref, l1_ref, l2_ref, gate_ref, x_ref,
                  dw_ref, dwb_ref, lng_ref, lnb_ref, wc_ref, bc_ref, wa_ref, wo_ref,
                  h_ref, uext, *, tm, tiles_per_seq):
    d = x_ref.shape[1]
    at_seq_start = pl.program_id(0) % tiles_per_seq == 0
    uext[0:CONV_HALO, :] = jnp.where(at_seq_start, 0.0, halo_ref[...])
    uext[CONV_HALO:, :] = u_ref[...]
    acc = jnp.zeros((tm, CONV_CH), F32)
    for w in range(CONV_WIDTH):
        off = CONV_HALO - (CONV_WIDTH - 1) + w
        acc = acc + uext[off:off + tm, :] * dw_ref[w:w + 1, :]
    c = acc + dwb_ref[...]
    mu = jnp.mean(c, axis=-1, keepdims=True)
    cc = c - mu
    var = jnp.mean(cc * cc, axis=-1, keepdims=True)
    c = cc * lax.rsqrt(var + EPS) * lng_ref[...] + lnb_ref[...]
    c = c * jax.nn.sigmoid(c)
    conv_out = jnp.dot(c.astype(BF16), wc_ref[...], preferred_element_type=F32) + bc_ref[...]

    lses = (l0_ref[...], l1_ref[...], l2_ref[...])
    outs = (o0_ref, o1_ref, o2_ref)
    m = jnp.maximum(jnp.maximum(lses[0], lses[1]), lses[2])
    es = [jnp.exp(l - m) for l in lses]
    inv = 1.0 / (es[0] + es[1] + es[2])
    attn_out = jnp.zeros((tm, d), F32)
    for g in range(3):
        og = (outs[g][...].astype(F32) * (es[g] * inv)).astype(BF16)
        attn_out = attn_out + jnp.dot(og, wa_ref[g * GROUP_WIDTH:(g + 1) * GROUP_WIDTH, :],
                                      preferred_element_type=F32)

    merged = (gate_ref[:, :d].astype(F32) * conv_out + gate_ref[:, d:].astype(F32) * attn_out)
    h_ref[...] = x_ref[...] + jnp.dot(merged.astype(BF16), wo_ref[...], preferred_element_type=F32)


def _mixer(u, outs, lses, gates, x2, dw, dwb, lng, lnb, wc, bc, wa, wo, tm, seq):
    n, d = x2.shape
    row = lambda w: pl.BlockSpec((tm, w), lambda i: (i, 0))
    halo = pl.BlockSpec((CONV_HALO, CONV_CH),
                        lambda i: (jnp.maximum(i * (tm // CONV_HALO) - 1, 0), 0))
    return pl.pallas_call(
        functools.partial(_mixer_kernel, tm=tm, tiles_per_seq=seq // tm),
        grid=(n // tm,),
        in_specs=[row(CONV_CH), halo] + [row(GROUP_WIDTH)] * 6 + [row(2 * d), row(d)]
                 + [_full(a.shape) for a in (dw, dwb, lng, lnb, wc, bc, wa, wo)],
        out_specs=row(d),
        out_shape=jax.ShapeDtypeStruct((n, d), F32),
        scratch_shapes=[pltpu.VMEM((tm + CONV_HALO, CONV_CH), F32)],
        compiler_params=_params(1),
        name="mixer_out",
    )(u, u, *outs, *lses, gates, x2, dw, dwb, lng, lnb, wc, bc, wa, wo)


_CAND_WIDTHS = tuple(PEER_TOPK // (r1 + 1) for r1 in range(PEER_TOPK))
_CAND_BLOCK_ROWS = 8
_UNRANKED = float(PEER_N_KEYS)


def _pack_rows(x):
    return pltpu.bitcast(x, jnp.uint32)


def _unpack_rows(x):
    return pltpu.bitcast(x, BF16)


def _extract_top(s, key_id, on_round):
    for r in range(PEER_TOPK):
        m = jnp.max(s, axis=0, keepdims=True)
        idx = jnp.min(jnp.where(s == m, key_id, 1e9), axis=0, keepdims=True)
        sel = key_id == idx
        s = jnp.where(sel, -jnp.inf, s)
        on_round(r, m, idx, sel)


def _rank_keys(s, key_id, iota16):
    state = [jnp.zeros(iota16.shape, F32), jnp.full(s.shape, _UNRANKED, F32)]

    def on_round(r, m, idx, sel):
        state[0] = jnp.where(iota16 == float(r), m, state[0])
        state[1] = jnp.where(sel, float(r), state[1])

    _extract_top(s, key_id, on_round)
    return state[0], state[1]


_MARK_SCALE = 2.0 ** 100
_CAND_MARK = -_MARK_SCALE


def _top_values_unique(s, iota16):
    ss = jnp.zeros(iota16.shape, F32)
    for r in range(PEER_TOPK):
        m = jnp.max(s, axis=0, keepdims=True)
        s = jnp.where(s == m, -_MARK_SCALE * (1.0 + r / 32.0), s)
        ss = jnp.where(iota16 == float(r), m, ss)
    return s, ss


def _decode_marks(s):
    marked = s < -0.5 * _MARK_SCALE
    rank = jnp.where(marked, (s * (-1.0 / _MARK_SCALE) - 1.0) * 32.0, _UNRANKED)
    return rank, jnp.sum(marked.astype(F32), axis=0, keepdims=True)


def _candidate_sums(ss1, ss2, iota8):
    blocks = [ss1[0:1, :] + ss2]
    for r1 in range(1, _CAND_BLOCK_ROWS):
        blk = ss1[r1:r1 + 1, :] + ss2[0:_CAND_BLOCK_ROWS, :]
        blocks.append(jnp.where(iota8 < float(_CAND_WIDTHS[r1]), blk, -jnp.inf))
    blocks.append(ss1[_CAND_BLOCK_ROWS:, :] + ss2[0:1, :])
    return jnp.concatenate(blocks, axis=0)


def _peer_prep_kernel(h_ref, g2_ref, wqt_ref, keys_ref,
                      hnt_ref, rk2_ref, b_ref, cnt_ref, a_ref,
                      qt_ref, s1_ref, s2_ref, tk_ref, *, tt):
    h = h_ref[...]
    ms = jnp.mean(h * h, axis=-1, keepdims=True)
    hnt = (h * lax.rsqrt(ms + EPS) * g2_ref[...]).T.astype(BF16)
    hnt_ref[...] = _pack_rows(hnt)
    qt_ref[...] = jnp.dot(wqt_ref[...], hnt, preferred_element_type=F32)

    key_id = lax.broadcasted_iota(jnp.int32, (PEER_N_KEYS, LANES), 0).astype(F32)
    iota16 = lax.broadcasted_iota(jnp.int32, (PEER_TOPK, LANES), 0).astype(F32)
    iota8 = lax.broadcasted_iota(jnp.int32, (_CAND_BLOCK_ROWS, LANES), 0).astype(F32)
    cand_id = jnp.concatenate(
        [iota16]
        + [iota8 + float(r1 * PEER_TOPK) for r1 in range(1, _CAND_BLOCK_ROWS)]
        + [(iota8 + float(_CAND_BLOCK_ROWS)) * float(PEER_TOPK)], axis=0)

    def head_body(hd, carry):
        for p, dst_ref in ((0, s1_ref), (1, s2_ref)):
            base = pl.multiple_of((hd * 2 + p) * PEER_N_KEYS, PEER_N_KEYS)
            qhp = qt_ref[pl.ds(base, PEER_N_KEYS), :].astype(BF16)
            dst_ref[...] = jnp.dot(keys_ref[hd, p], qhp, preferred_element_type=F32)

        def lane_group(c, carry2):
            lanes = pl.ds(pl.multiple_of(c * LANES, LANES), LANES)
            s1 = s1_ref[:, lanes]
            s2 = s2_ref[:, lanes]
            k = PEER_TOPK
            row_ss1, row_ss2, row_best, row_cnt = (slice(i * k, (i + 1) * k) for i in range(4))
            row_rk1 = slice(4 * k, 4 * k + PEER_N_KEYS)
            row_rk2 = slice(4 * k + PEER_N_KEYS, 4 * k + 2 * PEER_N_KEYS)

            m1, ss1 = _top_values_unique(s1, iota16)
            m2, ss2 = _top_values_unique(s2, iota16)
            rk1, n1 = _decode_marks(m1)
            rk2, n2 = _decode_marks(m2)
            cand = _candidate_sums(ss1, ss2, iota8)
            best = jnp.zeros((k, LANES), F32)
            for r in range(k):
                m = jnp.max(cand, axis=0, keepdims=True)
                cand = jnp.where(cand == m, _CAND_MARK, cand)
                best = jnp.where(iota16 == float(r), m, best)
            picked = (cand == _CAND_MARK).astype(F32)
            per_row = [jnp.sum(picked[0:k, :], axis=0, keepdims=True)]
            for r1 in range(1, _CAND_BLOCK_ROWS):
                lo = k + (r1 - 1) * _CAND_BLOCK_ROWS
                per_row.append(jnp.sum(picked[lo:lo + _CAND_BLOCK_ROWS, :], axis=0, keepdims=True))
            cnt16 = jnp.concatenate(per_row + [picked[k + 7 * _CAND_BLOCK_ROWS:, :]], axis=0)
            n3 = jnp.sum(cnt16, axis=0, keepdims=True)
            tk_ref[row_ss1, :] = ss1
            tk_ref[row_ss2, :] = ss2
            tk_ref[row_best, :] = best
            tk_ref[row_cnt, :] = cnt16
            tk_ref[row_rk1, :] = rk1
            tk_ref[row_rk2, :] = rk2
            had_tie = jnp.max(jnp.abs(n1 - k) + jnp.abs(n2 - k) + jnp.abs(n3 - k)) > 0.5

            @pl.when(had_tie)
            def _():
                ss1x, rk1x = _rank_keys(s1, key_id, iota16)
                ss2x, rk2x = _rank_keys(s2, key_id, iota16)
                state = [jnp.zeros((k, LANES), F32), jnp.zeros((k, LANES), F32)]

                def on_round(r, m, idx, sel):
                    state[0] = jnp.where(iota16 == float(r), m, state[0])
                    row = jnp.floor(idx * (1.0 / k))
                    state[1] = state[1] + (iota16 == row).astype(F32)

                _extract_top(_candidate_sums(ss1x, ss2x, iota8), cand_id, on_round)
                tk_ref[row_ss1, :] = ss1x
                tk_ref[row_ss2, :] = ss2x
                tk_ref[row_best, :] = state[0]
                tk_ref[row_cnt, :] = state[1]
                tk_ref[row_rk1, :] = rk1x
                tk_ref[row_rk2, :] = rk2x

            ss1, ss2, best, cnt16 = (tk_ref[rows, :] for rows in (row_ss1, row_ss2, row_best, row_cnt))
            rk1 = tk_ref[row_rk1, :]
            rk2 = tk_ref[row_rk2, :]
            z = jnp.sum(jnp.exp(best - best[0:1, :]), axis=0, keepdims=True)

            a_ref[hd, :, lanes] = jnp.exp(s1 - ss1[0:1, :])
            b_ref[hd, :, lanes] = _pack_rows((jnp.exp(s2 - ss2[0:1, :]) / z).astype(BF16))
            rk2_ref[hd, :, lanes] = _pack_rows(rk2.astype(BF16))
            cnt = jnp.zeros((PEER_N_KEYS, LANES), F32)
            for r1 in range(PEER_TOPK):
                cnt = jnp.where(rk1 == float(r1), cnt16[r1:r1 + 1, :], cnt)
            cnt_ref[hd, :, lanes] = cnt
            return carry2

        lax.fori_loop(0, tt // LANES, lane_group, 0)
        return carry

    lax.fori_loop(0, PEER_HEADS, head_body, 0)


def _peer_prep(h2, g2, wqt, keys, tt):
    n, d = h2.shape
    nt = n // tt
    state = lambda rows: pl.BlockSpec((None, PEER_HEADS, rows, tt), lambda i: (i, 0, 0, 0))
    shape = lambda rows, dt: jax.ShapeDtypeStruct((nt, PEER_HEADS, rows, tt), dt)
    half = PEER_N_KEYS // 2
    return pl.pallas_call(
        functools.partial(_peer_prep_kernel, tt=tt),
        grid=(nt,),
        in_specs=[pl.BlockSpec((tt, d), lambda i: (i, 0)), _full(g2.shape), _full(wqt.shape),
                  _full(keys.shape)],
        out_specs=[pl.BlockSpec((d // 2, tt), lambda i: (0, i)), state(half), state(half),
                   state(PEER_N_KEYS), state(PEER_N_KEYS)],
        out_shape=[jax.ShapeDtypeStruct((d // 2, n), jnp.uint32), shape(half, jnp.uint32),
                   shape(half, jnp.uint32), shape(PEER_N_KEYS, F32), shape(PEER_N_KEYS, F32)],
        scratch_shapes=[pltpu.VMEM((wqt.shape[0], tt), F32),
                        pltpu.VMEM((PEER_N_KEYS, tt), F32),
                        pltpu.VMEM((PEER_N_KEYS, tt), F32),
                        pltpu.VMEM((4 * PEER_TOPK + 2 * PEER_N_KEYS, LANES), F32)],
        compiler_params=_params(1),
        name="peer_prep",
    )(h2, g2, wqt, keys)


def _peer_main_kernel(hnt_ref, rk2_ref, b_ref, cnt_ref, a_ref, u_ref, vt_ref, res_ref,
                      y_ref, yt_ref, ht0_ref, ht1_ref, act0_ref, act1_ref, rows_ref,
                      *, keys_per_chunk, tt, n_chunks):
    j = pl.program_id(1)

    @pl.when(j == 0)
    def _():
        yt_ref[...] = jnp.zeros(yt_ref.shape, F32)
        for ref in (ht0_ref, ht1_ref):
            ref[...] = jnp.zeros(ref.shape, F32)
        for ref in (act0_ref, act1_ref):
            ref[...] = jnp.zeros(ref.shape, jnp.uint32)

    half = PEER_N_KEYS // 2
    chunk_prev = jnp.clip(j - 1, 0, n_chunks - 1)

    def gate_block(c, ht_old, act_new):
        lanes = slice(c * LANES, (c + 1) * LANES)
        for k0 in range(0, keys_per_chunk, 2):
            ks = range(k0, min(k0 + 2, keys_per_chunk))
            gates = {k: jnp.zeros((PEER_N_KEYS, LANES), BF16) for k in ks}
            for hd in range(PEER_HEADS):
                rk2 = _unpack_rows(rk2_ref[hd, :, lanes])
                b = _unpack_rows(b_ref[hd, :, lanes])
                for k in ks:
                    cnt_row = rows_ref[k, hd:hd + 1, lanes].astype(BF16)
                    a_row = rows_ref[k, PEER_HEADS + hd:PEER_HEADS + hd + 1, lanes].astype(BF16)
                    gates[k] = gates[k] + jnp.where(rk2 < cnt_row, b * a_row, 0.0)
            for k in ks:
                x = ht_old[k * PEER_N_KEYS:(k + 1) * PEER_N_KEYS, lanes]
                gelu = 0.5 * x * (1.0 + lax.erf(x * math.sqrt(0.5)))
                act_new[k * half:(k + 1) * half, lanes] = _pack_rows(gelu.astype(BF16) * gates[k])

    def step(ht_new, ht_old, act_new, act_old):
        for k in range(keys_per_chunk):
            i1 = chunk_prev * keys_per_chunk + k
            for hd in range(PEER_HEADS):
                rows_ref[k, hd:hd + 1, :] = cnt_ref[hd, pl.ds(i1, 1), :]
                rows_ref[k, PEER_HEADS + hd:PEER_HEADS + hd + 1, :] = a_ref[hd, pl.ds(i1, 1), :]
        lanes_per_piece = MXU_COLS // LANES
        for piece in range(tt // MXU_COLS):
            cols = slice(piece * MXU_COLS, (piece + 1) * MXU_COLS)
            ht_new[:, cols] = jnp.dot(_unpack_rows(u_ref[...]), _unpack_rows(hnt_ref[:, cols]),
                                      preferred_element_type=F32)
            gate_block(piece * lanes_per_piece, ht_old, act_new)
            yt_ref[:, cols] += jnp.dot(_unpack_rows(vt_ref[...]), _unpack_rows(act_old[:, cols]),
                                       preferred_element_type=F32)
            for c in range(piece * lanes_per_piece + 1, (piece + 1) * lanes_per_piece):
                gate_block(c, ht_old, act_new)

    @pl.when(j % 2 == 0)
    def _():
        step(ht0_ref, ht1_ref, act1_ref, act0_ref)

    @pl.when(j % 2 == 1)
    def _():
        step(ht1_ref, ht0_ref, act0_ref, act1_ref)

    @pl.when(j == pl.num_programs(1) - 1)
    def _():
        y_ref[...] = res_ref[...] + yt_ref[...].T


def _peer_main(hnt, rk2, b, cnt, a, u_packed, vt_packed, h2, tt, chunk):
    n, d = h2.shape
    n_chunks = 2 * u_packed.shape[0] // chunk
    kpc = chunk // PEER_N_KEYS
    state = lambda a: pl.BlockSpec((None,) + a.shape[1:], lambda i, j: (i, 0, 0, 0))
    tok = pl.BlockSpec((tt, d), lambda i, j: (i, 0))
    return pl.pallas_call(
        functools.partial(_peer_main_kernel, keys_per_chunk=kpc, tt=tt, n_chunks=n_chunks),
        grid=(n // tt, n_chunks + 2),
        in_specs=[pl.BlockSpec((d // 2, tt), lambda i, j: (0, i)), state(rk2), state(b),
                  state(cnt), state(a),
                  pl.BlockSpec((chunk // 2, d), lambda i, j: (jnp.minimum(j, n_chunks - 1), 0)),
                  pl.BlockSpec((d // 2, chunk),
                               lambda i, j: (0, jnp.clip(j - 2, 0, n_chunks - 1))),
                  tok],
        out_specs=tok,
        out_shape=jax.ShapeDtypeStruct((n, d), F32),
        scratch_shapes=[pltpu.VMEM((d, tt), F32),
                        pltpu.VMEM((chunk, tt), F32),
                        pltpu.VMEM((chunk, tt), F32),
                        pltpu.VMEM((chunk // 2, tt), jnp.uint32),
                        pltpu.VMEM((chunk // 2, tt), jnp.uint32),
                        pltpu.VMEM((kpc, 2 * PEER_HEADS, tt), F32)],
        compiler_params=_params(2),
        name="peer_main",
    )(hnt, rk2, b, cnt, a, u_packed, vt_packed, h2)


def _pack_rows_xla(w):
    bits = lax.bitcast_convert_type(w.astype(BF16), jnp.uint16).astype(jnp.uint32)
    return bits[0::2] | (bits[1::2] << 16)


def _layer(x2, batch, seq, w_in, conv_b_glu, conv_dw_w, conv_dw_b, conv_ln_g, conv_ln_b, conv_w_proj,
           conv_b_proj, q_norm_g, k_norm_g, rel_bias, attn_w_proj, mix_w_out, norm1_g, norm2_g,
           peer_w_q, peer_sub_keys, peer_u, peer_v):
    n, d = x2.shape
    row = lambda v: v.reshape(1, -1).astype(F32)
    c0 = 2 * CONV_CH
    cuts = [c0, c0 + ATTN_WIDTH, c0 + 2 * ATTN_WIDTH, c0 + 3 * ATTN_WIDTH]
    wglu, wq, wk, wv, wg = [w.astype(BF16) for w in jnp.split(w_in, cuts, axis=-1)]
    head_of = np.arange(ATTN_WIDTH) // HEAD_DIM
    pm = jnp.asarray((head_of[:, None] == head_of[None, :]) / HEAD_DIM, BF16)
    tile_heads = lambda g: jnp.tile(g.astype(F32), N_ATTN_HEADS).reshape(1, ATTN_WIDTH)

    u, qn, kn, v, gates = _inproj(x2, row(norm1_g), wglu, row(conv_b_glu), wq, wk, wv,
                                  tile_heads(q_norm_g), tile_heads(k_norm_g), pm, wg, tm=256)

    outs, lses = [], []
    for g, (_, dil) in enumerate(ATTN_GROUPS):
        o, lse = _attn_group(qn, kn, v, _band_bias(rel_bias, g, dil), g, dil, batch, seq)
        outs.append(o)
        lses.append(lse)

    h2 = _mixer(u, outs, lses, gates, x2, conv_dw_w.reshape(CONV_WIDTH, CONV_CH).astype(F32),
                row(conv_dw_b), row(conv_ln_g), row(conv_ln_b), conv_w_proj.astype(BF16),
                row(conv_b_proj), attn_w_proj.astype(BF16), mix_w_out.astype(BF16),
                tm=min(512, seq), seq=seq)

    tt = min(512, n)
    hnt, rk2, b, cnt, a = _peer_prep(h2, row(norm2_g), peer_w_q.T.astype(BF16),
                                     peer_sub_keys.astype(BF16), tt)
    return _peer_main(hnt, rk2, b, cnt, a, _pack_rows_xla(peer_u), _pack_rows_xla(peer_v.T), h2, tt,
                      chunk=512)


def kernel(x, w_in, conv_b_glu, conv_dw_w, conv_dw_b, conv_ln_g, conv_ln_b, conv_w_proj, conv_b_proj, q_norm_g, k_norm_g, rel_bias, attn_w_proj, mix_w_out, norm1_g, norm2_g, peer_w_q, peer_sub_keys, peer_u, peer_v):
    batch, seq, d = x.shape
    x2 = x.reshape(batch * seq, d)
    for l in range(w_in.shape[0]):
        x2 = _layer(x2, batch, seq, w_in[l], conv_b_glu[l], conv_dw_w[l], conv_dw_b[l], conv_ln_g[l],
                    conv_ln_b[l], conv_w_proj[l], conv_b_proj[l], q_norm_g[l], k_norm_g[l], rel_bias,
                    attn_w_proj[l], mix_w_out[l], norm1_g[l], norm2_g[l], peer_w_q[l],
                    peer_sub_keys[l], peer_u[l], peer_v[l])
    return x2.reshape(batch, seq, d)
```

```python
import functools
import math

import numpy as np
import jax
import jax.numpy as jnp
from jax import lax
from jax.experimental import pallas as pl
from jax.experimental.pallas import tpu as pltpu

F32 = jnp.float32
BF16 = jnp.bfloat16

HEAD_DIM = 64
HEADS_PER_GROUP = 4
ATTN_GROUPS = ((128, 1), (512, 4), (2048, 16))
N_ATTN_HEADS = HEADS_PER_GROUP * len(ATTN_GROUPS)
ATTN_WIDTH = N_ATTN_HEADS * HEAD_DIM
GROUP_WIDTH = HEADS_PER_GROUP * HEAD_DIM
KEY_STEPS = 128
NEG_INF = -1e30
CONV_CH = 512
CONV_WIDTH = 31
CONV_HALO = 32
REL_BUCKETS = 32
REL_MAX_DISTANCE = 2048
PEER_HEADS = 8
PEER_N_KEYS = 128
PEER_TOPK = 16
EPS = 1e-6

LANES = 128
MXU_COLS = 256
VMEM_LIMIT = 56 * 1024 * 1024

_NT = (((1,), (1,)), ((), ()))


def _params(n_axes):
    return pltpu.CompilerParams(dimension_semantics=("arbitrary",) * n_axes,
                                vmem_limit_bytes=VMEM_LIMIT)


def _full(shape):
    n = len(shape)
    return pl.BlockSpec(shape, lambda *_: (0,) * n)


def _inproj_kernel(x_ref, g1_ref, wglu_ref, bglu_ref, wq_ref, wk_ref, wv_ref, qg_ref, kg_ref,
                   pm_ref, wg_ref, u_ref, *rest, tm):
    qkv_refs, gate_ref, stage = rest[:9], rest[9], rest[10]
    x = x_ref[...]
    ms = jnp.mean(x * x, axis=-1, keepdims=True)
    xn = (x * lax.rsqrt(ms + EPS) * g1_ref[...]).astype(BF16)

    glu = jnp.dot(xn, wglu_ref[...], preferred_element_type=F32) + bglu_ref[...]
    u_ref[...] = glu[:, :CONV_CH] * jax.nn.sigmoid(glu[:, CONV_CH:])

    def head_rmsnorm(w_ref, g_ref):
        y = jnp.dot(xn, w_ref[...], preferred_element_type=F32)
        msq = jnp.dot((y * y).astype(BF16), pm_ref[...], preferred_element_type=F32)
        return y * lax.rsqrt(msq + EPS) * g_ref[...]

    tensors = (head_rmsnorm(wq_ref, qg_ref) * (HEAD_DIM ** -0.5),
               head_rmsnorm(wk_ref, kg_ref),
               jnp.dot(xn, wv_ref[...], preferred_element_type=F32))
    for t, y in enumerate(tensors):
        for g, (_, dil) in enumerate(ATTN_GROUPS):
            out_ref = qkv_refs[t * len(ATTN_GROUPS) + g]
            yg = y[:, g * GROUP_WIDTH:(g + 1) * GROUP_WIDTH]
            if dil == 1:
                out_ref[0] = yg.astype(BF16)
                continue
            for s in range(GROUP_WIDTH // LANES):
                stage[s] = yg[:, s * LANES:(s + 1) * LANES]
            for r in range(dil):
                for s in range(GROUP_WIDTH // LANES):
                    out_ref[r, :, s * LANES:(s + 1) * LANES] = (
                        stage[s, pl.ds(r, tm // dil, stride=dil), :].astype(BF16))
    gate_ref[...] = jax.nn.sigmoid(
        jnp.dot(xn, wg_ref[...], preferred_element_type=F32)).astype(BF16)


def _inproj(x2, g1, wglu, bglu, wq, wk, wv, qg, kg, pm, wg, tm, batch, seq):
    n, d = x2.shape
    tiles_per_seq = seq // tm
    row = lambda w: pl.BlockSpec((tm, w), lambda i: (i, 0))
    grouped = lambda dil: pl.BlockSpec((None, dil, tm // dil, GROUP_WIDTH),
                                       lambda i: (i // tiles_per_seq, 0, i % tiles_per_seq, 0))
    grouped_shape = lambda dil: jax.ShapeDtypeStruct((batch, dil, seq // dil, GROUP_WIDTH), BF16)
    dils = [dil for _ in range(3) for _, dil in ATTN_GROUPS]
    outs = pl.pallas_call(
        functools.partial(_inproj_kernel, tm=tm),
        grid=(n // tm,),
        in_specs=[row(d), _full(g1.shape), _full(wglu.shape), _full(bglu.shape), _full(wq.shape),
                  _full(wk.shape), _full(wv.shape), _full(qg.shape), _full(kg.shape),
                  _full(pm.shape), _full(wg.shape)],
        out_specs=[row(CONV_CH)] + [grouped(dil) for dil in dils] + [row(2 * d)],
        out_shape=[jax.ShapeDtypeStruct((n, CONV_CH), F32)] + [grouped_shape(dil) for dil in dils]
                  + [jax.ShapeDtypeStruct((n, 2 * d), BF16)],
        scratch_shapes=[pltpu.VMEM((GROUP_WIDTH // LANES, tm, LANES), F32)],
        compiler_params=_params(1),
        name="inproj",
    )(x2, g1, wglu, bglu, wq, wk, wv, qg, kg, pm, wg)
    qkv = [list(outs[1 + 3 * t:4 + 3 * t]) for t in range(3)]
    return outs[0], qkv, outs[10]


def _attn_kernel(q_ref, kc_ref, kp_ref, vc_ref, vp_ref, bias_ref, o_ref, lse_ref, kbuf, vbuf,
                 *, nsub):
    pair = pl.program_id(2)
    first_blk = (pl.program_id(3) == 0).astype(jnp.int32)
    kbuf[0:KEY_STEPS, :] = kp_ref[...]
    kbuf[KEY_STEPS:, :] = kc_ref[...]
    vbuf[0:KEY_STEPS, :] = vp_ref[...]
    vbuf[KEY_STEPS:, :] = vc_ref[...]
    lane = lax.broadcasted_iota(jnp.int32, (1, LANES), 1)
    lo = lane < HEAD_DIM
    head_mask = (lo.astype(BF16), (~lo).astype(BF16))
    for sub in range(nsub):
        r0 = sub * KEY_STEPS
        qs = q_ref[r0:r0 + KEY_STEPS, :]
        kw = kbuf[r0:r0 + 2 * KEY_STEPS, :]
        vw = vbuf[r0:r0 + 2 * KEY_STEPS, :]
        outs, lses = [], []
        for hh in range(2):
            s = lax.dot_general(qs * head_mask[hh], kw, _NT, preferred_element_type=F32)
            variant = first_blk if sub == 0 else 0
            s = s + bias_ref[variant, pair * 2 + hh]
            m = jnp.max(s, axis=1, keepdims=True)
            p = jnp.exp(s - m)
            l = jnp.sum(p, axis=1, keepdims=True)
            pv = jnp.dot(p.astype(BF16), vw, preferred_element_type=F32)
            outs.append(pv / l)
            lses.append(m + jnp.log(l))
        o_ref[r0:r0 + KEY_STEPS, :] = jnp.where(lo, outs[0], outs[1]).astype(BF16)
        lse_ref[r0:r0 + KEY_STEPS, :] = jnp.where(lo, lses[0], lses[1])


def _attn_group(q, k, v, bias, dil):
    batch, _, sub_len, _ = q.shape
    tq = min(512, sub_len)
    nsub = tq // KEY_STEPS
    cur = pl.BlockSpec((None, None, tq, LANES), lambda b, r, p, i: (b, r, i, p))
    prev = pl.BlockSpec((None, None, KEY_STEPS, LANES),
                        lambda b, r, p, i: (b, r, jnp.maximum(i * nsub - 1, 0), p))
    return pl.pallas_call(
        functools.partial(_attn_kernel, nsub=nsub),
        grid=(batch, dil, 2, sub_len // tq),
        in_specs=[cur, cur, prev, cur, prev, _full(bias.shape)],
        out_specs=[cur, cur],
        out_shape=[jax.ShapeDtypeStruct(q.shape, BF16), jax.ShapeDtypeStruct(q.shape, F32)],
        scratch_shapes=[pltpu.VMEM((tq + KEY_STEPS, LANES), BF16),
                        pltpu.VMEM((tq + KEY_STEPS, LANES), BF16)],
        compiler_params=_params(4),
        name=f"attn_dil{dil}",
    )(q, k, k, v, v, bias)


def _t5_causal_bucket(distance):
    n = distance.astype(jnp.int32)
    max_exact = REL_BUCKETS // 2
    nf = jnp.maximum(n, 1).astype(F32)
    large = max_exact + (jnp.log(nf / max_exact) / math.log(REL_MAX_DISTANCE / max_exact)
                         * (REL_BUCKETS - max_exact)).astype(jnp.int32)
    large = jnp.minimum(large, REL_BUCKETS - 1)
    return jnp.where(n < max_exact, n, large)


def _band_bias(rel_bias, g, dil):
    steps = jnp.arange(KEY_STEPS + 1)
    per_step = rel_bias[_t5_causal_bucket(steps * dil)][:, g * HEADS_PER_GROUP:(g + 1) * HEADS_PER_GROUP].T
    k = KEY_STEPS
    pad = lambda w: jnp.full((HEADS_PER_GROUP, w), NEG_INF, F32)
    f = jnp.concatenate([pad(k - 1), per_step[:, ::-1].astype(F32), pad(k)], axis=1)
    width = f.shape[1]
    normal = jnp.tile(f, (1, k))[:, :k * (width - 1)].reshape(HEADS_PER_GROUP, k, width - 1)
    normal = normal[:, :, k - 1:k - 1 + 2 * k]
    before_start = np.arange(2 * k)[None, None, :] < k
    at_start = jnp.where(before_start, NEG_INF, normal)
    return jnp.stack([normal, at_start])


def _mixer_kernel(u_ref, halo_ref, o0_ref, o1_ref, o2_ref, l0_ref, l1_ref, l2_ref, gate_ref, x_ref,
                  dw_ref, dwb_ref, lng_ref, lnb_ref, wc_ref, bc_ref, wa_ref, wo_ref,
                  h_ref, uext, order_ref, *, tm, tiles_per_seq):
    d = x_ref.shape[1]

    def token_order(ref, dil, slot):
        if dil == 1:
            return ref[0].astype(F32)
        for r in range(dil):
            blk = ref[r].astype(F32)
            for s in range(GROUP_WIDTH // LANES):
                order_ref[slot, s, pl.ds(r, tm // dil, stride=dil), :] = blk[:, s * LANES:(s + 1) * LANES]
        return jnp.concatenate([order_ref[slot, s] for s in range(GROUP_WIDTH // LANES)], axis=1)

    at_seq_start = pl.program_id(0) % tiles_per_seq == 0
    uext[0:CONV_HALO, :] = jnp.where(at_seq_start, 0.0, halo_ref[...])
    uext[CONV_HALO:, :] = u_ref[...]
    acc = jnp.zeros((tm, CONV_CH), F32)
    for w in range(CONV_WIDTH):
        off = CONV_HALO - (CONV_WIDTH - 1) + w
        acc = acc + uext[off:off + tm, :] * dw_ref[w:w + 1, :]
    c = acc + dwb_ref[...]
    mu = jnp.mean(c, axis=-1, keepdims=True)
    cc = c - mu
    var = jnp.mean(cc * cc, axis=-1, keepdims=True)
    c = cc * lax.rsqrt(var + EPS) * lng_ref[...] + lnb_ref[...]
    c = c * jax.nn.sigmoid(c)
    conv_out = jnp.dot(c.astype(BF16), wc_ref[...], preferred_element_type=F32) + bc_ref[...]

    dils = [dil for _, dil in ATTN_GROUPS]
    lses = [token_order(ref, dil, 2 * g) for g, (ref, dil) in enumerate(zip((l0_ref, l1_ref, l2_ref), dils))]
    outs = [token_order(ref, dil, 2 * g + 1) for g, (ref, dil) in enumerate(zip((o0_ref, o1_ref, o2_ref), dils))]
    m = jnp.maximum(jnp.maximum(lses[0], lses[1]), lses[2])
    es = [jnp.exp(l - m) for l in lses]
    inv = 1.0 / (es[0] + es[1] + es[2])
    attn_out = jnp.zeros((tm, d), F32)
    for g in range(3):
        og = (outs[g] * (es[g] * inv)).astype(BF16)
        attn_out = attn_out + jnp.dot(og, wa_ref[g * GROUP_WIDTH:(g + 1) * GROUP_WIDTH, :],
                                      preferred_element_type=F32)

    merged = (gate_ref[:, :d].astype(F32) * conv_out + gate_ref[:, d:].astype(F32) * attn_out)
    h_ref[...] = x_ref[...] + jnp.dot(merged.astype(BF16), wo_ref[...], preferred_element_type=F32)


def _mixer(u, outs, lses, gates, x2, dw, dwb, lng, lnb, wc, bc, wa, wo, tm, seq):
    n, d = x2.shape
    row = lambda w: pl.BlockSpec((tm, w), lambda i: (i, 0))
    halo = pl.BlockSpec((CONV_HALO, CONV_CH),
                        lambda i: (jnp.maximum(i * (tm // CONV_HALO) - 1, 0), 0))
    tiles_per_seq = seq // tm
    grouped = lambda dil: pl.BlockSpec((None, dil, tm // dil, GROUP_WIDTH),
                                       lambda i: (i // tiles_per_seq, 0, i % tiles_per_seq, 0))
    group_specs = [grouped(dil) for _, dil in ATTN_GROUPS]
    return pl.pallas_call(
        functools.partial(_mixer_kernel, tm=tm, tiles_per_seq=tiles_per_seq),
        grid=(n // tm,),
        in_specs=[row(CONV_CH), halo] + group_specs * 2 + [row(2 * d), row(d)]
                 + [_full(a.shape) for a in (dw, dwb, lng, lnb, wc, bc, wa, wo)],
        out_specs=row(d),
        out_shape=jax.ShapeDtypeStruct((n, d), F32),
        scratch_shapes=[pltpu.VMEM((tm + CONV_HALO, CONV_CH), F32),
                        pltpu.VMEM((2 * len(ATTN_GROUPS), GROUP_WIDTH // LANES, tm, LANES), F32)],
        compiler_params=_params(1),
        name="mixer_out",
    )(u, u, *outs, *lses, gates, x2, dw, dwb, lng, lnb, wc, bc, wa, wo)


_CAND_WIDTHS = tuple(PEER_TOPK // (r1 + 1) for r1 in range(PEER_TOPK))
_CAND_BLOCK_ROWS = 8
_UNRANKED = float(PEER_N_KEYS)


def _pack_rows(x):
    return pltpu.bitcast(x, jnp.uint32)


def _unpack_rows(x):
    return pltpu.bitcast(x, BF16)


def _extract_top(s, key_id, on_round):
    for r in range(PEER_TOPK):
        m = jnp.max(s, axis=0, keepdims=True)
        idx = jnp.min(jnp.where(s == m, key_id, 1e9), axis=0, keepdims=True)
        sel = key_id == idx
        s = jnp.where(sel, -jnp.inf, s)
        on_round(r, m, idx, sel)


def _rank_keys(s, key_id, iota16):
    state = [jnp.zeros(iota16.shape, F32), jnp.full(s.shape, _UNRANKED, F32)]

    def on_round(r, m, idx, sel):
        state[0] = jnp.where(iota16 == float(r), m, state[0])
        state[1] = jnp.where(sel, float(r), state[1])

    _extract_top(s, key_id, on_round)
    return state[0], state[1]


_MARK_SCALE = 2.0 ** 100
_CAND_MARK = -_MARK_SCALE


def _top_values_unique(s, iota16):
    ss = jnp.zeros(iota16.shape, F32)
    for r in range(PEER_TOPK):
        m = jnp.max(s, axis=0, keepdims=True)
        s = jnp.where(s == m, -_MARK_SCALE * (1.0 + r / 32.0), s)
        ss = jnp.where(iota16 == float(r), m, ss)
    return s, ss


def _decode_marks(s):
    marked = s < -0.5 * _MARK_SCALE
    rank = jnp.where(marked, (s * (-1.0 / _MARK_SCALE) - 1.0) * 32.0, _UNRANKED)
    return rank, jnp.sum(marked.astype(F32), axis=0, keepdims=True)


def _candidate_sums(ss1, ss2, iota8):
    blocks = [ss1[0:1, :] + ss2]
    for r1 in range(1, _CAND_BLOCK_ROWS):
        blk = ss1[r1:r1 + 1, :] + ss2[0:_CAND_BLOCK_ROWS, :]
        blocks.append(jnp.where(iota8 < float(_CAND_WIDTHS[r1]), blk, -jnp.inf))
    blocks.append(ss1[_CAND_BLOCK_ROWS:, :] + ss2[0:1, :])
    return jnp.concatenate(blocks, axis=0)


def _peer_prep_kernel(h_ref, g2_ref, wqt_ref, keys_ref,
                      hnt_ref, rk2_ref, b_ref, cnt_ref, a_ref,
                      qt_ref, s1_ref, s2_ref, tk_ref, *, tt):
    h = h_ref[...]
    ms = jnp.mean(h * h, axis=-1, keepdims=True)
    hnt = (h * lax.rsqrt(ms + EPS) * g2_ref[...]).T.astype(BF16)
    hnt_ref[...] = _pack_rows(hnt)
    qt_ref[...] = jnp.dot(wqt_ref[...], hnt, preferred_element_type=F32)

    key_id = lax.broadcasted_iota(jnp.int32, (PEER_N_KEYS, LANES), 0).astype(F32)
    iota16 = lax.broadcasted_iota(jnp.int32, (PEER_TOPK, LANES), 0).astype(F32)
    iota8 = lax.broadcasted_iota(jnp.int32, (_CAND_BLOCK_ROWS, LANES), 0).astype(F32)
    cand_id = jnp.concatenate(
        [iota16]
        + [iota8 + float(r1 * PEER_TOPK) for r1 in range(1, _CAND_BLOCK_ROWS)]
        + [(iota8 + float(_CAND_BLOCK_ROWS)) * float(PEER_TOPK)], axis=0)

    def head_body(hd, carry):
        for p, dst_ref in ((0, s1_ref), (1, s2_ref)):
            base = pl.multiple_of((hd * 2 + p) * PEER_N_KEYS, PEER_N_KEYS)
            qhp = qt_ref[pl.ds(base, PEER_N_KEYS), :].astype(BF16)
            dst_ref[...] = jnp.dot(keys_ref[hd, p], qhp, preferred_element_type=F32)

        def lane_group(c, carry2):
            lanes = pl.ds(pl.multiple_of(c * LANES, LANES), LANES)
            s1 = s1_ref[:, lanes]
            s2 = s2_ref[:, lanes]
            k = PEER_TOPK
            row_ss1, row_ss2, row_best, row_cnt = (slice(i * k, (i + 1) * k) for i in range(4))
            row_rk1 = slice(4 * k, 4 * k + PEER_N_KEYS)
            row_rk2 = slice(4 * k + PEER_N_KEYS, 4 * k + 2 * PEER_N_KEYS)

            m1, ss1 = _top_values_unique(s1, iota16)
            m2, ss2 = _top_values_unique(s2, iota16)
            rk1, n1 = _decode_marks(m1)
            rk2, n2 = _decode_marks(m2)
            cand = _candidate_sums(ss1, ss2, iota8)
            best = jnp.zeros((k, LANES), F32)
            for r in range(k):
                m = jnp.max(cand, axis=0, keepdims=True)
                cand = jnp.where(cand == m, _CAND_MARK, cand)
                best = jnp.where(iota16 == float(r), m, best)
            picked = (cand == _CAND_MARK).astype(F32)
            per_row = [jnp.sum(picked[0:k, :], axis=0, keepdims=True)]
            for r1 in range(1, _CAND_BLOCK_ROWS):
                lo = k + (r1 - 1) * _CAND_BLOCK_ROWS
                per_row.append(jnp.sum(picked[lo:lo + _CAND_BLOCK_ROWS, :], axis=0, keepdims=True))
            cnt16 = jnp.concatenate(per_row + [picked[k + 7 * _CAND_BLOCK_ROWS:, :]], axis=0)
            n3 = jnp.sum(cnt16, axis=0, keepdims=True)
            tk_ref[row_ss1, :] = ss1
            tk_ref[row_ss2, :] = ss2
            tk_ref[row_best, :] = best
            tk_ref[row_cnt, :] = cnt16
            tk_ref[row_rk1, :] = rk1
            tk_ref[row_rk2, :] = rk2
            had_tie = jnp.max(jnp.abs(n1 - k) + jnp.abs(n2 - k) + jnp.abs(n3 - k)) > 0.5

            @pl.when(had_tie)
            def _():
                ss1x, rk1x = _rank_keys(s1, key_id, iota16)
                ss2x, rk2x = _rank_keys(s2, key_id, iota16)
                state = [jnp.zeros((k, LANES), F32), jnp.zeros((k, LANES), F32)]

                def on_round(r, m, idx, sel):
                    state[0] = jnp.where(iota16 == float(r), m, state[0])
                    row = jnp.floor(idx * (1.0 / k))
                    state[1] = state[1] + (iota16 == row).astype(F32)

                _extract_top(_candidate_sums(ss1x, ss2x, iota8), cand_id, on_round)
                tk_ref[row_ss1, :] = ss1x
                tk_ref[row_ss2, :] = ss2x
                tk_ref[row_best, :] = state[0]
                tk_ref[row_cnt, :] = state[1]
                tk_ref[row_rk1, :] = rk1x
                tk_ref[row_rk2, :] = rk2x

            ss1, ss2, best, cnt16 = (tk_ref[rows, :] for rows in (row_ss1, row_ss2, row_best, row_cnt))
            rk1 = tk_ref[row_rk1, :]
            rk2 = tk_ref[row_rk2, :]
            z = jnp.sum(jnp.exp(best - best[0:1, :]), axis=0, keepdims=True)

            a_ref[hd, :, lanes] = jnp.exp(s1 - ss1[0:1, :])
            b_ref[hd, :, lanes] = _pack_rows((jnp.exp(s2 - ss2[0:1, :]) / z).astype(BF16))
            rk2_ref[hd, :, lanes] = _pack_rows(rk2.astype(BF16))
            cnt = jnp.zeros((PEER_N_KEYS, LANES), F32)
            for r1 in range(PEER_TOPK):
                cnt = jnp.where(rk1 == float(r1), cnt16[r1:r1 + 1, :], cnt)
            cnt_ref[hd, :, lanes] = cnt
            return carry2

        lax.fori_loop(0, tt // LANES, lane_group, 0)
        return carry

    lax.fori_loop(0, PEER_HEADS, head_body, 0)


def _peer_prep(h2, g2, wqt, keys, tt):
    n, d = h2.shape
    nt = n // tt
    state = lambda rows: pl.BlockSpec((None, PEER_HEADS, rows, tt), lambda i: (i, 0, 0, 0))
    shape = lambda rows, dt: jax.ShapeDtypeStruct((nt, PEER_HEADS, rows, tt), dt)
    half = PEER_N_KEYS // 2
    return pl.pallas_call(
        functools.partial(_peer_prep_kernel, tt=tt),
        grid=(nt,),
        in_specs=[pl.BlockSpec((tt, d), lambda i: (i, 0)), _full(g2.shape), _full(wqt.shape),
                  _full(keys.shape)],
        out_specs=[pl.BlockSpec((d // 2, tt), lambda i: (0, i)), state(half), state(half),
                   state(PEER_N_KEYS), state(PEER_N_KEYS)],
        out_shape=[jax.ShapeDtypeStruct((d // 2, n), jnp.uint32), shape(half, jnp.uint32),
                   shape(half, jnp.uint32), shape(PEER_N_KEYS, F32), shape(PEER_N_KEYS, F32)],
        scratch_shapes=[pltpu.VMEM((wqt.shape[0], tt), F32),
                        pltpu.VMEM((PEER_N_KEYS, tt), F32),
                        pltpu.VMEM((PEER_N_KEYS, tt), F32),
                        pltpu.VMEM((4 * PEER_TOPK + 2 * PEER_N_KEYS, LANES), F32)],
        compiler_params=_params(1),
        name="peer_prep",
    )(h2, g2, wqt, keys)


def _peer_main_kernel(hnt_ref, rk2_ref, b_ref, cnt_ref, a_ref, u_ref, vt_ref, res_ref,
                      y_ref, yt_ref, ht0_ref, ht1_ref, act0_ref, act1_ref, rows_ref,
                      *, keys_per_chunk, tt, n_chunks, n_items):
    g = pl.program_id(0)
    item3 = g - 2
    first_of_tile = (item3 >= 0) & (item3 % n_chunks == 0)
    last_of_tile = (item3 >= 0) & (item3 % n_chunks == n_chunks - 1)

    @pl.when(g == 0)
    def _():
        for ref in (ht0_ref, ht1_ref):
            ref[...] = jnp.zeros(ref.shape, F32)
        for ref in (act0_ref, act1_ref):
            ref[...] = jnp.zeros(ref.shape, jnp.uint32)

    @pl.when((g == 0) | first_of_tile)
    def _():
        yt_ref[...] = jnp.zeros(yt_ref.shape, F32)

    half = PEER_N_KEYS // 2
    chunk2 = jnp.clip(g - 1, 0, n_items - 1) % n_chunks

    def gate_block(c, k0, ht_old, act_new):
        lanes = slice(c * LANES, (c + 1) * LANES)
        ks = range(k0, k0 + 2)
        gates = {k: jnp.zeros((PEER_N_KEYS, LANES), BF16) for k in ks}
        for hd in range(PEER_HEADS):
            rk2 = _unpack_rows(rk2_ref[hd, :, lanes])
            b = _unpack_rows(b_ref[hd, :, lanes])
            for k in ks:
                cnt_row = rows_ref[k, hd:hd + 1, lanes].astype(BF16)
                a_row = rows_ref[k, PEER_HEADS + hd:PEER_HEADS + hd + 1, lanes].astype(BF16)
                gates[k] = gates[k] + jnp.where(rk2 < cnt_row, b * a_row, 0.0)
        for k in ks:
            x = ht_old[k * PEER_N_KEYS:(k + 1) * PEER_N_KEYS, lanes]
            gelu = 0.5 * x * (1.0 + lax.erf(x * math.sqrt(0.5)))
            act_new[k * half:(k + 1) * half, lanes] = _pack_rows(gelu.astype(BF16) * gates[k])

    def step(ht_new, ht_old, act_new, act_old):
        for k in range(keys_per_chunk):
            i1 = chunk2 * keys_per_chunk + k
            for hd in range(PEER_HEADS):
                rows_ref[k, hd:hd + 1, :] = cnt_ref[hd, pl.ds(i1, 1), :]
                rows_ref[k, PEER_HEADS + hd:PEER_HEADS + hd + 1, :] = a_ref[hd, pl.ds(i1, 1), :]

        def u_piece(piece):
            cols = slice(piece * MXU_COLS, (piece + 1) * MXU_COLS)
            ht_new[:, cols] = jnp.dot(_unpack_rows(u_ref[...]), _unpack_rows(hnt_ref[:, cols]),
                                      preferred_element_type=F32)

        def v_piece(piece):
            cols = slice(piece * MXU_COLS, (piece + 1) * MXU_COLS)
            yt_ref[:, cols] += jnp.dot(_unpack_rows(vt_ref[...]), _unpack_rows(act_old[:, cols]),
                                       preferred_element_type=F32)

        mxu_work = [functools.partial(f, p) for p in range(tt // MXU_COLS) for f in (u_piece, v_piece)]
        vpu_work = [functools.partial(gate_block, c, k0, ht_old, act_new)
                    for c in range(tt // LANES) for k0 in range(0, keys_per_chunk, 2)]
        per_mxu = len(vpu_work) // len(mxu_work)
        for i, vpu in enumerate(vpu_work):
            if i % per_mxu == 0 and i // per_mxu < len(mxu_work):
                mxu_work[i // per_mxu]()
            vpu()

    @pl.when(g % 2 == 0)
    def _():
        step(ht0_ref, ht1_ref, act1_ref, act0_ref)

    @pl.when(g % 2 == 1)
    def _():
        step(ht1_ref, ht0_ref, act0_ref, act1_ref)

    @pl.when(last_of_tile)
    def _():
        y_ref[...] = res_ref[...] + yt_ref[...].T


def _peer_main(hnt, rk2, b, cnt, a, u_packed, vt_packed, h2, tt, chunk):
    n, d = h2.shape
    n_chunks = 2 * u_packed.shape[0] // chunk
    n_items = (n // tt) * n_chunks
    kpc = chunk // PEER_N_KEYS
    assert kpc % 2 == 0 and (tt // LANES) * (kpc // 2) % (2 * (tt // MXU_COLS)) == 0
    item = lambda g, lag: jnp.clip(g - lag, 0, n_items - 1)
    tile = lambda g, lag: item(g, lag) // n_chunks
    chunk_of = lambda g, lag: item(g, lag) % n_chunks
    state = lambda arr: pl.BlockSpec((None,) + arr.shape[1:], lambda g: (tile(g, 1), 0, 0, 0))
    return pl.pallas_call(
        functools.partial(_peer_main_kernel, keys_per_chunk=kpc, tt=tt, n_chunks=n_chunks,
                          n_items=n_items),
        grid=(n_items + 2,),
        in_specs=[pl.BlockSpec((d // 2, tt), lambda g: (0, tile(g, 0))), state(rk2), state(b),
                  state(cnt), state(a),
                  pl.BlockSpec((chunk // 2, d), lambda g: (chunk_of(g, 0), 0)),
                  pl.BlockSpec((d // 2, chunk), lambda g: (0, chunk_of(g, 2))),
                  pl.BlockSpec((tt, d), lambda g: (tile(g, 2), 0))],
        out_specs=pl.BlockSpec((tt, d), lambda g: (tile(g, 2), 0)),
        out_shape=jax.ShapeDtypeStruct((n, d), F32),
        scratch_shapes=[pltpu.VMEM((d, tt), F32),
                        pltpu.VMEM((chunk, tt), F32),
                        pltpu.VMEM((chunk, tt), F32),
                        pltpu.VMEM((chunk // 2, tt), jnp.uint32),
                        pltpu.VMEM((chunk // 2, tt), jnp.uint32),
                        pltpu.VMEM((kpc, 2 * PEER_HEADS, tt), F32)],
        compiler_params=_params(1),
        name="peer_main",
    )(hnt, rk2, b, cnt, a, u_packed, vt_packed, h2)


def _pack_table_kernel(w_ref, o_ref, *, transpose):
    w = w_ref[...]
    o_ref[...] = _pack_rows((w.T if transpose else w).astype(BF16))


def _pack_table(w, transpose, rows=512):
    r, c = w.shape
    if transpose:
        out_shape, out_spec = (c // 2, r), pl.BlockSpec((c // 2, rows), lambda i: (0, i))
    else:
        out_shape, out_spec = (r // 2, c), pl.BlockSpec((rows // 2, c), lambda i: (i, 0))
    return pl.pallas_call(
        functools.partial(_pack_table_kernel, transpose=transpose),
        grid=(r // rows,),
        in_specs=[pl.BlockSpec((rows, c), lambda i: (i, 0))],
        out_specs=out_spec,
        out_shape=jax.ShapeDtypeStruct(out_shape, jnp.uint32),
        compiler_params=_params(1),
        name="pack_vt" if transpose else "pack_u",
    )(w)


def _layer(x2, batch, seq, w_in, conv_b_glu, conv_dw_w, conv_dw_b, conv_ln_g, conv_ln_b, conv_w_proj,
           conv_b_proj, q_norm_g, k_norm_g, rel_bias, attn_w_proj, mix_w_out, norm1_g, norm2_g,
           peer_w_q, peer_sub_keys, peer_u, peer_v):
    n, d = x2.shape
    row = lambda v: v.reshape(1, -1).astype(F32)
    c0 = 2 * CONV_CH
    cuts = [c0, c0 + ATTN_WIDTH, c0 + 2 * ATTN_WIDTH, c0 + 3 * ATTN_WIDTH]
    wglu, wq, wk, wv, wg = [w.astype(BF16) for w in jnp.split(w_in, cuts, axis=-1)]
    head_of = np.arange(ATTN_WIDTH) // HEAD_DIM
    pm = jnp.asarray((head_of[:, None] == head_of[None, :]) / HEAD_DIM, BF16)
    tile_heads = lambda g: jnp.tile(g.astype(F32), N_ATTN_HEADS).reshape(1, ATTN_WIDTH)

    u, (qs, ks, vs), gates = _inproj(x2, row(norm1_g), wglu, row(conv_b_glu), wq, wk, wv,
                                     tile_heads(q_norm_g), tile_heads(k_norm_g), pm, wg,
                                     tm=256, batch=batch, seq=seq)

    outs, lses = [], []
    for g, (_, dil) in enumerate(ATTN_GROUPS):
        o, lse = _attn_group(qs[g], ks[g], vs[g], _band_bias(rel_bias, g, dil), dil)
        outs.append(o)
        lses.append(lse)

    h2 = _mixer(u, outs, lses, gates, x2, conv_dw_w.reshape(CONV_WIDTH, CONV_CH).astype(F32),
                row(conv_dw_b), row(conv_ln_g), row(conv_ln_b), conv_w_proj.astype(BF16),
                row(conv_b_proj), attn_w_proj.astype(BF16), mix_w_out.astype(BF16),
                tm=min(512, seq), seq=seq)

    tt = min(512, n)
    hnt, rk2, b, cnt, a = _peer_prep(h2, row(norm2_g), peer_w_q.T.astype(BF16),
                                     peer_sub_keys.astype(BF16), tt)
    return _peer_main(hnt, rk2, b, cnt, a, _pack_table(peer_u, transpose=False),
                      _pack_table(peer_v, transpose=True), h2, tt, chunk=1024)


def kernel(x, w_in, conv_b_glu, conv_dw_w, conv_dw_b, conv_ln_g, conv_ln_b, conv_w_proj, conv_b_proj, q_norm_g, k_norm_g, rel_bias, attn_w_proj, mix_w_out, norm1_g, norm2_g, peer_w_q, peer_sub_keys, peer_u, peer_v):
    batch, seq, d = x.shape
    x2 = x.reshape(batch * seq, d)
    for l in range(w_in.shape[0]):
        x2 = _layer(x2, batch, seq, w_in[l], conv_b_glu[l], conv_dw_w[l], conv_dw_b[l], conv_ln_g[l],
                    conv_ln_b[l], conv_w_proj[l], conv_b_proj[l], q_norm_g[l], k_norm_g[l], rel_bias,
                    attn_w_proj[l], mix_w_out[l], norm1_g[l], norm2_g[l], peer_w_q[l],
                    peer_sub_keys[l], peer_u[l], peer_v[l])
    return x2.reshape(batch, seq, d)
```

```python
import functools
import math

import numpy as np
import jax
import jax.numpy as jnp
from jax import lax
from jax.experimental import pallas as pl
from jax.experimental.pallas import tpu as pltpu

F32 = jnp.float32
BF16 = jnp.bfloat16

HEAD_DIM = 64
HEADS_PER_GROUP = 4
ATTN_GROUPS = ((128, 1), (512, 4), (2048, 16))
N_ATTN_HEADS = HEADS_PER_GROUP * len(ATTN_GROUPS)
ATTN_WIDTH = N_ATTN_HEADS * HEAD_DIM
GROUP_WIDTH = HEADS_PER_GROUP * HEAD_DIM
KEY_STEPS = 128
NEG_INF = -1e30
CONV_CH = 512
CONV_WIDTH = 31
CONV_HALO = 32
REL_BUCKETS = 32
REL_MAX_DISTANCE = 2048
PEER_HEADS = 8
PEER_N_KEYS = 128
PEER_TOPK = 16
EPS = 1e-6

LANES = 128
MXU_COLS = 256
ROW_SPLIT = 4
VMEM_LIMIT = 56 * 1024 * 1024

_NT = (((1,), (1,)), ((), ()))


def _params(n_axes):
    return pltpu.CompilerParams(dimension_semantics=("arbitrary",) * n_axes,
                                vmem_limit_bytes=VMEM_LIMIT)


def _full(shape):
    n = len(shape)
    return pl.BlockSpec(shape, lambda *_: (0,) * n)


def _inproj_kernel(x_ref, g1_ref, wglu_ref, bglu_ref, wq_ref, wk_ref, wv_ref, qg_ref, kg_ref,
                   pm_ref, wg_ref, u_ref, *rest, tm):
    qkv_refs, gate_ref, stage = rest[:9], rest[9], rest[10]
    x = x_ref[...]
    ms = jnp.mean(x * x, axis=-1, keepdims=True)
    xn = (x * lax.rsqrt(ms + EPS) * g1_ref[...]).astype(BF16)

    glu = jnp.dot(xn, wglu_ref[...], preferred_element_type=F32) + bglu_ref[...]
    u_ref[...] = glu[:, :CONV_CH] * jax.nn.sigmoid(glu[:, CONV_CH:])

    def head_rmsnorm(w_ref, g_ref):
        y = jnp.dot(xn, w_ref[...], preferred_element_type=F32)
        msq = jnp.dot((y * y).astype(BF16), pm_ref[...], preferred_element_type=F32)
        return y * lax.rsqrt(msq + EPS) * g_ref[...]

    tensors = (head_rmsnorm(wq_ref, qg_ref) * (HEAD_DIM ** -0.5),
               head_rmsnorm(wk_ref, kg_ref),
               jnp.dot(xn, wv_ref[...], preferred_element_type=F32))
    for t, y in enumerate(tensors):
        for g, (_, dil) in enumerate(ATTN_GROUPS):
            out_ref = qkv_refs[t * len(ATTN_GROUPS) + g]
            yg = y[:, g * GROUP_WIDTH:(g + 1) * GROUP_WIDTH]
            if dil == 1:
                out_ref[0] = yg.astype(BF16)
                continue
            for s in range(GROUP_WIDTH // LANES):
                stage[s] = yg[:, s * LANES:(s + 1) * LANES]
            for r in range(dil):
                for s in range(GROUP_WIDTH // LANES):
                    out_ref[r, :, s * LANES:(s + 1) * LANES] = (
                        stage[s, pl.ds(r, tm // dil, stride=dil), :].astype(BF16))
    gate_ref[...] = jax.nn.sigmoid(
        jnp.dot(xn, wg_ref[...], preferred_element_type=F32)).astype(BF16)


def _inproj(x2, g1, wglu, bglu, wq, wk, wv, qg, kg, pm, wg, tm, batch, seq):
    n, d = x2.shape
    tiles_per_seq = seq // tm
    row = lambda w: pl.BlockSpec((tm, w), lambda i: (i, 0))
    grouped = lambda dil: pl.BlockSpec((None, dil, tm // dil, GROUP_WIDTH),
                                       lambda i: (i // tiles_per_seq, 0, i % tiles_per_seq, 0))
    grouped_shape = lambda dil: jax.ShapeDtypeStruct((batch, dil, seq // dil, GROUP_WIDTH), BF16)
    dils = [dil for _ in range(3) for _, dil in ATTN_GROUPS]
    outs = pl.pallas_call(
        functools.partial(_inproj_kernel, tm=tm),
        grid=(n // tm,),
        in_specs=[row(d), _full(g1.shape), _full(wglu.shape), _full(bglu.shape), _full(wq.shape),
                  _full(wk.shape), _full(wv.shape), _full(qg.shape), _full(kg.shape),
                  _full(pm.shape), _full(wg.shape)],
        out_specs=[row(CONV_CH)] + [grouped(dil) for dil in dils] + [row(2 * d)],
        out_shape=[jax.ShapeDtypeStruct((n, CONV_CH), F32)] + [grouped_shape(dil) for dil in dils]
                  + [jax.ShapeDtypeStruct((n, 2 * d), BF16)],
        scratch_shapes=[pltpu.VMEM((GROUP_WIDTH // LANES, tm, LANES), F32)],
        compiler_params=_params(1),
        name="inproj",
    )(x2, g1, wglu, bglu, wq, wk, wv, qg, kg, pm, wg)
    qkv = [list(outs[1 + 3 * t:4 + 3 * t]) for t in range(3)]
    return outs[0], qkv, outs[10]


def _attn_kernel(q_ref, kc_ref, kp_ref, vc_ref, vp_ref, bias_ref, o_ref, lse_ref, kbuf, vbuf,
                 *, nsub):
    pair = pl.program_id(2)
    first_blk = (pl.program_id(3) == 0).astype(jnp.int32)
    kbuf[0:KEY_STEPS, :] = kp_ref[...]
    kbuf[KEY_STEPS:, :] = kc_ref[...]
    vbuf[0:KEY_STEPS, :] = vp_ref[...]
    vbuf[KEY_STEPS:, :] = vc_ref[...]
    lane = lax.broadcasted_iota(jnp.int32, (1, LANES), 1)
    lo = lane < HEAD_DIM
    head_mask = (lo.astype(BF16), (~lo).astype(BF16))
    for sub in range(nsub):
        r0 = sub * KEY_STEPS
        qs = q_ref[r0:r0 + KEY_STEPS, :]
        kw = kbuf[r0:r0 + 2 * KEY_STEPS, :]
        vw = vbuf[r0:r0 + 2 * KEY_STEPS, :]
        outs, lses = [], []
        for hh in range(2):
            s = lax.dot_general(qs * head_mask[hh], kw, _NT, preferred_element_type=F32)
            variant = first_blk if sub == 0 else 0
            s = s + bias_ref[variant, pair * 2 + hh]
            m = jnp.max(s, axis=1, keepdims=True)
            p = jnp.exp(s - m)
            l = jnp.sum(p, axis=1, keepdims=True)
            pv = jnp.dot(p.astype(BF16), vw, preferred_element_type=F32)
            outs.append(pv / l)
            lses.append(m + jnp.log(l))
        o_ref[r0:r0 + KEY_STEPS, :] = jnp.where(lo, outs[0], outs[1]).astype(BF16)
        lse_ref[r0:r0 + KEY_STEPS, :] = jnp.where(lo, lses[0], lses[1])


def _attn_group(q, k, v, bias, dil):
    batch, _, sub_len, _ = q.shape
    tq = min(512, sub_len)
    nsub = tq // KEY_STEPS
    cur = pl.BlockSpec((None, None, tq, LANES), lambda b, r, p, i: (b, r, i, p))
    prev = pl.BlockSpec((None, None, KEY_STEPS, LANES),
                        lambda b, r, p, i: (b, r, jnp.maximum(i * nsub - 1, 0), p))
    return pl.pallas_call(
        functools.partial(_attn_kernel, nsub=nsub),
        grid=(batch, dil, 2, sub_len // tq),
        in_specs=[cur, cur, prev, cur, prev, _full(bias.shape)],
        out_specs=[cur, cur],
        out_shape=[jax.ShapeDtypeStruct(q.shape, BF16), jax.ShapeDtypeStruct(q.shape, F32)],
        scratch_shapes=[pltpu.VMEM((tq + KEY_STEPS, LANES), BF16),
                        pltpu.VMEM((tq + KEY_STEPS, LANES), BF16)],
        compiler_params=_params(4),
        name=f"attn_dil{dil}",
    )(q, k, k, v, v, bias)


def _t5_causal_bucket(distance):
    n = distance.astype(jnp.int32)
    max_exact = REL_BUCKETS // 2
    nf = jnp.maximum(n, 1).astype(F32)
    large = max_exact + (jnp.log(nf / max_exact) / math.log(REL_MAX_DISTANCE / max_exact)
                         * (REL_BUCKETS - max_exact)).astype(jnp.int32)
    large = jnp.minimum(large, REL_BUCKETS - 1)
    return jnp.where(n < max_exact, n, large)


def _band_bias(rel_bias, g, dil):
    steps = jnp.arange(KEY_STEPS + 1)
    per_step = rel_bias[_t5_causal_bucket(steps * dil)][:, g * HEADS_PER_GROUP:(g + 1) * HEADS_PER_GROUP].T
    k = KEY_STEPS
    pad = lambda w: jnp.full((HEADS_PER_GROUP, w), NEG_INF, F32)
    f = jnp.concatenate([pad(k - 1), per_step[:, ::-1].astype(F32), pad(k)], axis=1)
    width = f.shape[1]
    normal = jnp.tile(f, (1, k))[:, :k * (width - 1)].reshape(HEADS_PER_GROUP, k, width - 1)
    normal = normal[:, :, k - 1:k - 1 + 2 * k]
    before_start = np.arange(2 * k)[None, None, :] < k
    at_start = jnp.where(before_start, NEG_INF, normal)
    return jnp.stack([normal, at_start])


def _mixer_kernel(u_ref, halo_ref, o0_ref, o1_ref, o2_ref, l0_ref, l1_ref, l2_ref, gate_ref, x_ref,
                  dw_ref, dwb_ref, lng_ref, lnb_ref, wc_ref, bc_ref, wa_ref, wo_ref,
                  h_ref, uext, order_ref, *, tm, tiles_per_seq):
    d = x_ref.shape[1]

    def token_order(ref, dil, slot):
        if dil == 1:
            return ref[0].astype(F32)
        for r in range(dil):
            blk = ref[r].astype(F32)
            for s in range(GROUP_WIDTH // LANES):
                order_ref[slot, s, pl.ds(r, tm // dil, stride=dil), :] = blk[:, s * LANES:(s + 1) * LANES]
        return jnp.concatenate([order_ref[slot, s] for s in range(GROUP_WIDTH // LANES)], axis=1)

    at_seq_start = pl.program_id(0) % tiles_per_seq == 0
    uext[0:CONV_HALO, :] = jnp.where(at_seq_start, 0.0, halo_ref[...])
    uext[CONV_HALO:, :] = u_ref[...]
    acc = jnp.zeros((tm, CONV_CH), F32)
    for w in range(CONV_WIDTH):
        off = CONV_HALO - (CONV_WIDTH - 1) + w
        acc = acc + uext[off:off + tm, :] * dw_ref[w:w + 1, :]
    c = acc + dwb_ref[...]
    mu = jnp.mean(c, axis=-1, keepdims=True)
    cc = c - mu
    var = jnp.mean(cc * cc, axis=-1, keepdims=True)
    c = cc * lax.rsqrt(var + EPS) * lng_ref[...] + lnb_ref[...]
    c = c * jax.nn.sigmoid(c)
    conv_out = jnp.dot(c.astype(BF16), wc_ref[...], preferred_element_type=F32) + bc_ref[...]

    dils = [dil for _, dil in ATTN_GROUPS]
    lses = [token_order(ref, dil, 2 * g) for g, (ref, dil) in enumerate(zip((l0_ref, l1_ref, l2_ref), dils))]
    outs = [token_order(ref, dil, 2 * g + 1) for g, (ref, dil) in enumerate(zip((o0_ref, o1_ref, o2_ref), dils))]
    m = jnp.maximum(jnp.maximum(lses[0], lses[1]), lses[2])
    es = [jnp.exp(l - m) for l in lses]
    inv = 1.0 / (es[0] + es[1] + es[2])
    attn_out = jnp.zeros((tm, d), F32)
    for g in range(3):
        og = (outs[g] * (es[g] * inv)).astype(BF16)
        attn_out = attn_out + jnp.dot(og, wa_ref[g * GROUP_WIDTH:(g + 1) * GROUP_WIDTH, :],
                                      preferred_element_type=F32)

    merged = (gate_ref[:, :d].astype(F32) * conv_out + gate_ref[:, d:].astype(F32) * attn_out)
    h_ref[...] = x_ref[...] + jnp.dot(merged.astype(BF16), wo_ref[...], preferred_element_type=F32)


def _mixer(u, outs, lses, gates, x2, dw, dwb, lng, lnb, wc, bc, wa, wo, tm, seq):
    n, d = x2.shape
    row = lambda w: pl.BlockSpec((tm, w), lambda i: (i, 0))
    halo = pl.BlockSpec((CONV_HALO, CONV_CH),
                        lambda i: (jnp.maximum(i * (tm // CONV_HALO) - 1, 0), 0))
    tiles_per_seq = seq // tm
    grouped = lambda dil: pl.BlockSpec((None, dil, tm // dil, GROUP_WIDTH),
                                       lambda i: (i // tiles_per_seq, 0, i % tiles_per_seq, 0))
    group_specs = [grouped(dil) for _, dil in ATTN_GROUPS]
    return pl.pallas_call(
        functools.partial(_mixer_kernel, tm=tm, tiles_per_seq=tiles_per_seq),
        grid=(n // tm,),
        in_specs=[row(CONV_CH), halo] + group_specs * 2 + [row(2 * d), row(d)]
                 + [_full(a.shape) for a in (dw, dwb, lng, lnb, wc, bc, wa, wo)],
        out_specs=row(d),
        out_shape=jax.ShapeDtypeStruct((n, d), F32),
        scratch_shapes=[pltpu.VMEM((tm + CONV_HALO, CONV_CH), F32),
                        pltpu.VMEM((2 * len(ATTN_GROUPS), GROUP_WIDTH // LANES, tm, LANES), F32)],
        compiler_params=_params(1),
        name="mixer_out",
    )(u, u, *outs, *lses, gates, x2, dw, dwb, lng, lnb, wc, bc, wa, wo)


_CAND_WIDTHS = tuple(PEER_TOPK // (r1 + 1) for r1 in range(PEER_TOPK))
_CAND_BLOCK_ROWS = 8
_UNRANKED = float(PEER_N_KEYS)


def _pack_rows(x):
    return pltpu.bitcast(x, jnp.uint32)


def _unpack_rows(x):
    return pltpu.bitcast(x, BF16)


def _col_reduce(x, op, reduce_fn):
    parts = [x[i:i + 8] for i in range(0, x.shape[0], 8)]
    while len(parts) > 1:
        nxt = [op(parts[i], parts[i + 1]) for i in range(0, len(parts) - 1, 2)]
        parts = nxt + ([parts[-1]] if len(parts) % 2 else [])
    return reduce_fn(parts[0], axis=0, keepdims=True)


def _col_max(x):
    return _col_reduce(x, jnp.maximum, jnp.max)


def _col_min(x):
    return _col_reduce(x, jnp.minimum, jnp.min)


def _col_sum(x):
    return _col_reduce(x, jnp.add, jnp.sum)


def _extract_top(s, key_id, on_round):
    for r in range(PEER_TOPK):
        m = _col_max(s)
        idx = _col_min(jnp.where(s == m, key_id, 1e9))
        sel = key_id == idx
        s = jnp.where(sel, -jnp.inf, s)
        on_round(r, m, idx, sel)


def _rank_keys(s, key_id, iota16):
    state = [jnp.zeros(iota16.shape, F32), jnp.full(s.shape, _UNRANKED, F32)]

    def on_round(r, m, idx, sel):
        state[0] = jnp.where(iota16 == float(r), m, state[0])
        state[1] = jnp.where(sel, float(r), state[1])

    _extract_top(s, key_id, on_round)
    return state[0], state[1]


_MARK_SCALE = 2.0 ** 100
_CAND_MARK = -_MARK_SCALE


def _top_values_unique(scores, iota16):
    scores = list(scores)
    tops = [jnp.zeros(iota16.shape, F32) for _ in scores]
    for r in range(PEER_TOPK):
        for i, s in enumerate(scores):
            m = _col_max(s)
            scores[i] = jnp.where(s == m, -_MARK_SCALE * (1.0 + r / 32.0), s)
            tops[i] = jnp.where(iota16 == float(r), m, tops[i])
    return scores, tops


def _decode_marks(s):
    marked = s < -0.5 * _MARK_SCALE
    rank = jnp.where(marked, (s * (-1.0 / _MARK_SCALE) - 1.0) * 32.0, _UNRANKED)
    return rank, _col_sum(marked.astype(F32))


def _candidate_sums(ss1, ss2, iota8):
    blocks = [ss1[0:1, :] + ss2]
    for r1 in range(1, _CAND_BLOCK_ROWS):
        blk = ss1[r1:r1 + 1, :] + ss2[0:_CAND_BLOCK_ROWS, :]
        blocks.append(jnp.where(iota8 < float(_CAND_WIDTHS[r1]), blk, -jnp.inf))
    blocks.append(ss1[_CAND_BLOCK_ROWS:, :] + ss2[0:1, :])
    return jnp.concatenate(blocks, axis=0)


def _peer_prep_kernel(h_ref, g2_ref, wqt_ref, keys_ref,
                      hnt_ref, rk2_ref, b_ref, cnt_ref, a_ref,
                      qt_ref, s1_ref, s2_ref, tk_ref, *, tt):
    h = h_ref[...]
    ms = jnp.mean(h * h, axis=-1, keepdims=True)
    hnt = (h * lax.rsqrt(ms + EPS) * g2_ref[...]).T.astype(BF16)
    hnt_ref[...] = _pack_rows(hnt)
    qt_ref[...] = jnp.dot(wqt_ref[...], hnt, preferred_element_type=F32)

    key_id = lax.broadcasted_iota(jnp.int32, (PEER_N_KEYS, LANES), 0).astype(F32)
    iota16 = lax.broadcasted_iota(jnp.int32, (PEER_TOPK, LANES), 0).astype(F32)
    iota8 = lax.broadcasted_iota(jnp.int32, (_CAND_BLOCK_ROWS, LANES), 0).astype(F32)
    cand_id = jnp.concatenate(
        [iota16]
        + [iota8 + float(r1 * PEER_TOPK) for r1 in range(1, _CAND_BLOCK_ROWS)]
        + [(iota8 + float(_CAND_BLOCK_ROWS)) * float(PEER_TOPK)], axis=0)

    def head_body(hd, carry):
        for p, dst_ref in ((0, s1_ref), (1, s2_ref)):
            base = pl.multiple_of((hd * 2 + p) * PEER_N_KEYS, PEER_N_KEYS)
            qhp = qt_ref[pl.ds(base, PEER_N_KEYS), :].astype(BF16)
            dst_ref[...] = jnp.dot(keys_ref[hd, p], qhp, preferred_element_type=F32)

        def lane_group(c, carry2):
            lanes = pl.ds(pl.multiple_of(c * LANES, LANES), LANES)
            s1 = s1_ref[:, lanes]
            s2 = s2_ref[:, lanes]
            k = PEER_TOPK
            row_ss1, row_ss2, row_best, row_cnt = (slice(i * k, (i + 1) * k) for i in range(4))
            row_rk1 = slice(4 * k, 4 * k + PEER_N_KEYS)
            row_rk2 = slice(4 * k + PEER_N_KEYS, 4 * k + 2 * PEER_N_KEYS)

            (m1, m2), (ss1, ss2) = _top_values_unique((s1, s2), iota16)
            rk1, n1 = _decode_marks(m1)
            rk2, n2 = _decode_marks(m2)
            cand = _candidate_sums(ss1, ss2, iota8)
            best = jnp.zeros((k, LANES), F32)
            for r in range(k):
                m = _col_max(cand)
                cand = jnp.where(cand == m, _CAND_MARK, cand)
                best = jnp.where(iota16 == float(r), m, best)
            picked = (cand == _CAND_MARK).astype(F32)
            per_row = [jnp.sum(picked[0:k, :], axis=0, keepdims=True)]
            for r1 in range(1, _CAND_BLOCK_ROWS):
                lo = k + (r1 - 1) * _CAND_BLOCK_ROWS
                per_row.append(jnp.sum(picked[lo:lo + _CAND_BLOCK_ROWS, :], axis=0, keepdims=True))
            cnt16 = jnp.concatenate(per_row + [picked[k + 7 * _CAND_BLOCK_ROWS:, :]], axis=0)
            n3 = jnp.sum(cnt16, axis=0, keepdims=True)
            tk_ref[row_ss1, :] = ss1
            tk_ref[row_ss2, :] = ss2
            tk_ref[row_best, :] = best
            tk_ref[row_cnt, :] = cnt16
            tk_ref[row_rk1, :] = rk1
            tk_ref[row_rk2, :] = rk2
            had_tie = jnp.max(jnp.abs(n1 - k) + jnp.abs(n2 - k) + jnp.abs(n3 - k)) > 0.5

            @pl.when(had_tie)
            def _():
                ss1x, rk1x = _rank_keys(s1, key_id, iota16)
                ss2x, rk2x = _rank_keys(s2, key_id, iota16)
                state = [jnp.zeros((k, LANES), F32), jnp.zeros((k, LANES), F32)]

                def on_round(r, m, idx, sel):
                    state[0] = jnp.where(iota16 == float(r), m, state[0])
                    row = jnp.floor(idx * (1.0 / k))
                    state[1] = state[1] + (iota16 == row).astype(F32)

                _extract_top(_candidate_sums(ss1x, ss2x, iota8), cand_id, on_round)
                tk_ref[row_ss1, :] = ss1x
                tk_ref[row_ss2, :] = ss2x
                tk_ref[row_best, :] = state[0]
                tk_ref[row_cnt, :] = state[1]
                tk_ref[row_rk1, :] = rk1x
                tk_ref[row_rk2, :] = rk2x

            ss1, ss2, best, cnt16 = (tk_ref[rows, :] for rows in (row_ss1, row_ss2, row_best, row_cnt))
            rk1 = tk_ref[row_rk1, :]
            rk2 = tk_ref[row_rk2, :]
            z = jnp.sum(jnp.exp(best - best[0:1, :]), axis=0, keepdims=True)

            a_ref[hd, :, lanes] = jnp.exp(s1 - ss1[0:1, :])
            b_ref[hd, :, lanes] = _pack_rows((jnp.exp(s2 - ss2[0:1, :]) / z).astype(BF16))
            rk2_ref[hd, :, lanes] = _pack_rows(rk2.astype(BF16))
            cnt = jnp.zeros((PEER_N_KEYS, LANES), F32)
            for r1 in range(PEER_TOPK):
                cnt = jnp.where(rk1 == float(r1), cnt16[r1:r1 + 1, :], cnt)
            cnt_ref[hd, :, lanes] = cnt
            return carry2

        lax.fori_loop(0, tt // LANES, lane_group, 0)
        return carry

    lax.fori_loop(0, PEER_HEADS, head_body, 0)


def _peer_prep(h2, g2, wqt, keys, tt):
    n, d = h2.shape
    nt = n // tt
    state = lambda rows: pl.BlockSpec((None, PEER_HEADS, rows, tt), lambda i: (i, 0, 0, 0))
    shape = lambda rows, dt: jax.ShapeDtypeStruct((nt, PEER_HEADS, rows, tt), dt)
    half = PEER_N_KEYS // 2
    return pl.pallas_call(
        functools.partial(_peer_prep_kernel, tt=tt),
        grid=(nt,),
        in_specs=[pl.BlockSpec((tt, d), lambda i: (i, 0)), _full(g2.shape), _full(wqt.shape),
                  _full(keys.shape)],
        out_specs=[pl.BlockSpec((d // 2, tt), lambda i: (0, i)), state(half), state(half),
                   state(PEER_N_KEYS), state(PEER_N_KEYS)],
        out_shape=[jax.ShapeDtypeStruct((d // 2, n), jnp.uint32), shape(half, jnp.uint32),
                   shape(half, jnp.uint32), shape(PEER_N_KEYS, F32), shape(PEER_N_KEYS, F32)],
        scratch_shapes=[pltpu.VMEM((wqt.shape[0], tt), F32),
                        pltpu.VMEM((PEER_N_KEYS, tt), F32),
                        pltpu.VMEM((PEER_N_KEYS, tt), F32),
                        pltpu.VMEM((4 * PEER_TOPK + 2 * PEER_N_KEYS, LANES), F32)],
        compiler_params=_params(1),
        name="peer_prep",
    )(h2, g2, wqt, keys)


def _peer_main_kernel(hnt_ref, rk2_ref, b_ref, cnt_ref, a_ref, u_ref, vt_ref, res_ref,
                      y_ref, yt_ref, ht0_ref, ht1_ref, act0_ref, act1_ref, rows_ref,
                      *, keys_per_chunk, tt, n_chunks, n_items):
    g = pl.program_id(0)
    item3 = g - 2
    first_of_tile = (item3 >= 0) & (item3 % n_chunks == 0)
    last_of_tile = (item3 >= 0) & (item3 % n_chunks == n_chunks - 1)

    @pl.when(g == 0)
    def _():
        for ref in (ht0_ref, ht1_ref):
            ref[...] = jnp.zeros(ref.shape, F32)
        for ref in (act0_ref, act1_ref):
            ref[...] = jnp.zeros(ref.shape, jnp.uint32)

    @pl.when((g == 0) | first_of_tile)
    def _():
        yt_ref[...] = jnp.zeros(yt_ref.shape, F32)

    half = PEER_N_KEYS // 2
    chunk2 = jnp.clip(g - 1, 0, n_items - 1) % n_chunks

    def gate_block(c, k0, ht_old, act_new):
        lanes = slice(c * LANES, (c + 1) * LANES)
        ks = range(k0, k0 + 2)
        gates = {k: jnp.zeros((PEER_N_KEYS, LANES), BF16) for k in ks}
        for hd in range(PEER_HEADS):
            rk2 = _unpack_rows(rk2_ref[hd, :, lanes])
            b = _unpack_rows(b_ref[hd, :, lanes])
            for k in ks:
                cnt_row = rows_ref[k, hd:hd + 1, lanes].astype(BF16)
                a_row = rows_ref[k, PEER_HEADS + hd:PEER_HEADS + hd + 1, lanes].astype(BF16)
                gates[k] = gates[k] + jnp.where(rk2 < cnt_row, b * a_row, 0.0)
        for k in ks:
            x = ht_old[k * PEER_N_KEYS:(k + 1) * PEER_N_KEYS, lanes]
            gelu = 0.5 * x * (1.0 + lax.erf(x * math.sqrt(0.5)))
            act_new[k * half:(k + 1) * half, lanes] = _pack_rows(gelu.astype(BF16) * gates[k])

    def step(ht_new, ht_old, act_new, act_old):
        for k in range(keys_per_chunk):
            i1 = chunk2 * keys_per_chunk + k
            for hd in range(PEER_HEADS):
                rows_ref[k, hd:hd + 1, :] = cnt_ref[hd, pl.ds(i1, 1), :]
                rows_ref[k, PEER_HEADS + hd:PEER_HEADS + hd + 1, :] = a_ref[hd, pl.ds(i1, 1), :]

        def u_piece(piece, q):
            cols = slice(piece * MXU_COLS, (piece + 1) * MXU_COLS)
            rp = u_ref.shape[0] // ROW_SPLIT
            ht_new[2 * q * rp:2 * (q + 1) * rp, cols] = jnp.dot(
                _unpack_rows(u_ref[q * rp:(q + 1) * rp, :]), _unpack_rows(hnt_ref[:, cols]),
                preferred_element_type=F32)

        def v_piece(piece, q):
            cols = slice(piece * MXU_COLS, (piece + 1) * MXU_COLS)
            rp = vt_ref.shape[0] // ROW_SPLIT
            yt_ref[2 * q * rp:2 * (q + 1) * rp, cols] += jnp.dot(
                _unpack_rows(vt_ref[q * rp:(q + 1) * rp, :]), _unpack_rows(act_old[:, cols]),
                preferred_element_type=F32)

        mxu_work = [functools.partial(f, p, q) for p in range(tt // MXU_COLS)
                    for f in (u_piece, v_piece) for q in range(ROW_SPLIT)]
        vpu_work = [functools.partial(gate_block, c, k0, ht_old, act_new)
                    for c in range(tt // LANES) for k0 in range(0, keys_per_chunk, 2)]
        done_m = done_v = 0
        while done_m < len(mxu_work) or done_v < len(vpu_work):
            if done_m * len(vpu_work) <= done_v * len(mxu_work) and done_m < len(mxu_work):
                mxu_work[done_m]()
                done_m += 1
            else:
                vpu_work[done_v]()
                done_v += 1

    @pl.when(g % 2 == 0)
    def _():
        step(ht0_ref, ht1_ref, act1_ref, act0_ref)

    @pl.when(g % 2 == 1)
    def _():
        step(ht1_ref, ht0_ref, act0_ref, act1_ref)

    @pl.when(last_of_tile)
    def _():
        y_ref[...] = res_ref[...] + yt_ref[...].T


def _peer_main(hnt, rk2, b, cnt, a, u_packed, vt_packed, h2, tt, chunk):
    n, d = h2.shape
    n_chunks = 2 * u_packed.shape[0] // chunk
    n_items = (n // tt) * n_chunks
    kpc = chunk // PEER_N_KEYS
    assert kpc % 2 == 0 and chunk % (2 * ROW_SPLIT) == 0 and d % (2 * ROW_SPLIT) == 0
    item = lambda g, lag: jnp.clip(g - lag, 0, n_items - 1)
    tile = lambda g, lag: item(g, lag) // n_chunks
    chunk_of = lambda g, lag: item(g, lag) % n_chunks
    state = lambda arr: pl.BlockSpec((None,) + arr.shape[1:], lambda g: (tile(g, 1), 0, 0, 0))
    return pl.pallas_call(
        functools.partial(_peer_main_kernel, keys_per_chunk=kpc, tt=tt, n_chunks=n_chunks,
                          n_items=n_items),
        grid=(n_items + 2,),
        in_specs=[pl.BlockSpec((d // 2, tt), lambda g: (0, tile(g, 0))), state(rk2), state(b),
                  state(cnt), state(a),
                  pl.BlockSpec((chunk // 2, d), lambda g: (chunk_of(g, 0), 0)),
                  pl.BlockSpec((d // 2, chunk), lambda g: (0, chunk_of(g, 2))),
                  pl.BlockSpec((tt, d), lambda g: (tile(g, 2), 0))],
        out_specs=pl.BlockSpec((tt, d), lambda g: (tile(g, 2), 0)),
        out_shape=jax.ShapeDtypeStruct((n, d), F32),
        scratch_shapes=[pltpu.VMEM((d, tt), F32),
                        pltpu.VMEM((chunk, tt), F32),
                        pltpu.VMEM((chunk, tt), F32),
                        pltpu.VMEM((chunk // 2, tt), jnp.uint32),
                        pltpu.VMEM((chunk // 2, tt), jnp.uint32),
                        pltpu.VMEM((kpc, 2 * PEER_HEADS, tt), F32)],
        compiler_params=_params(1),
        name="peer_main",
    )(hnt, rk2, b, cnt, a, u_packed, vt_packed, h2)


def _pack_table_kernel(w_ref, o_ref, *, transpose):
    w = w_ref[...]
    o_ref[...] = _pack_rows((w.T if transpose else w).astype(BF16))


def _pack_table(w, transpose, rows=512):
    r, c = w.shape
    if transpose:
        out_shape, out_spec = (c // 2, r), pl.BlockSpec((c // 2, rows), lambda i: (0, i))
    else:
        out_shape, out_spec = (r // 2, c), pl.BlockSpec((rows // 2, c), lambda i: (i, 0))
    return pl.pallas_call(
        functools.partial(_pack_table_kernel, transpose=transpose),
        grid=(r // rows,),
        in_specs=[pl.BlockSpec((rows, c), lambda i: (i, 0))],
        out_specs=out_spec,
        out_shape=jax.ShapeDtypeStruct(out_shape, jnp.uint32),
        compiler_params=_params(1),
        name="pack_vt" if transpose else "pack_u",
    )(w)


def _layer(x2, batch, seq, w_in, conv_b_glu, conv_dw_w, conv_dw_b, conv_ln_g, conv_ln_b, conv_w_proj,
           conv_b_proj, q_norm_g, k_norm_g, rel_bias, attn_w_proj, mix_w_out, norm1_g, norm2_g,
           peer_w_q, peer_sub_keys, peer_u, peer_v):
    n, d = x2.shape
    row = lambda v: v.reshape(1, -1).astype(F32)
    c0 = 2 * CONV_CH
    cuts = [c0, c0 + ATTN_WIDTH, c0 + 2 * ATTN_WIDTH, c0 + 3 * ATTN_WIDTH]
    wglu, wq, wk, wv, wg = [w.astype(BF16) for w in jnp.split(w_in, cuts, axis=-1)]
    head_of = np.arange(ATTN_WIDTH) // HEAD_DIM
    pm = jnp.asarray((head_of[:, None] == head_of[None, :]) / HEAD_DIM, BF16)
    tile_heads = lambda g: jnp.tile(g.astype(F32), N_ATTN_HEADS).reshape(1, ATTN_WIDTH)

    u, (qs, ks, vs), gates = _inproj(x2, row(norm1_g), wglu, row(conv_b_glu), wq, wk, wv,
                                     tile_heads(q_norm_g), tile_heads(k_norm_g), pm, wg,
                                     tm=256, batch=batch, seq=seq)

    outs, lses = [], []
    for g, (_, dil) in enumerate(ATTN_GROUPS):
        o, lse = _attn_group(qs[g], ks[g], vs[g], _band_bias(rel_bias, g, dil), dil)
        outs.append(o)
        lses.append(lse)

    h2 = _mixer(u, outs, lses, gates, x2, conv_dw_w.reshape(CONV_WIDTH, CONV_CH).astype(F32),
                row(conv_dw_b), row(conv_ln_g), row(conv_ln_b), conv_w_proj.astype(BF16),
                row(conv_b_proj), attn_w_proj.astype(BF16), mix_w_out.astype(BF16),
                tm=min(512, seq), seq=seq)

    tt = min(512, n)
    hnt, rk2, b, cnt, a = _peer_prep(h2, row(norm2_g), peer_w_q.T.astype(BF16),
                                     peer_sub_keys.astype(BF16), tt)
    return _peer_main(hnt, rk2, b, cnt, a, _pack_table(peer_u, transpose=False),
                      _pack_table(peer_v, transpose=True), h2, tt, chunk=1024)


def kernel(x, w_in, conv_b_glu, conv_dw_w, conv_dw_b, conv_ln_g, conv_ln_b, conv_w_proj, conv_b_proj, q_norm_g, k_norm_g, rel_bias, attn_w_proj, mix_w_out, norm1_g, norm2_g, peer_w_q, peer_sub_keys, peer_u, peer_v):
    batch, seq, d = x.shape
    x2 = x.reshape(batch * seq, d)
    for l in range(w_in.shape[0]):
        x2 = _layer(x2, batch, seq, w_in[l], conv_b_glu[l], conv_dw_w[l], conv_dw_b[l], conv_ln_g[l],
                    conv_ln_b[l], conv_w_proj[l], conv_b_proj[l], q_norm_g[l], k_norm_g[l], rel_bias,
                    attn_w_proj[l], mix_w_out[l], norm1_g[l], norm2_g[l], peer_w_q[l],
                    peer_sub_keys[l], peer_u[l], peer_v[l])
    return x2.reshape(batch, seq, d)
```

```python
import functools
import math

import numpy as np
import jax
import jax.numpy as jnp
from jax import lax
from jax.experimental import pallas as pl
from jax.experimental.pallas import tpu as pltpu

F32 = jnp.float32
BF16 = jnp.bfloat16

HEAD_DIM = 64
HEADS_PER_GROUP = 4
ATTN_GROUPS = ((128, 1), (512, 4), (2048, 16))
N_ATTN_HEADS = HEADS_PER_GROUP * len(ATTN_GROUPS)
ATTN_WIDTH = N_ATTN_HEADS * HEAD_DIM
GROUP_WIDTH = HEADS_PER_GROUP * HEAD_DIM
KEY_STEPS = 128
NEG_INF = -1e30
CONV_CH = 512
CONV_WIDTH = 31
CONV_HALO = 32
REL_BUCKETS = 32
REL_MAX_DISTANCE = 2048
PEER_HEADS = 8
PEER_N_KEYS = 128
PEER_TOPK = 16
EPS = 1e-6

LANES = 128
SUBLANES = 8
CONV_ROW_CHUNK = 32
MXU_COLS = 256
ROW_SPLIT = 4
KEYS_PER_PASS = 4
VMEM_LIMIT = 56 * 1024 * 1024

_NT = (((1,), (1,)), ((), ()))


def _params(n_axes):
    return pltpu.CompilerParams(dimension_semantics=("arbitrary",) * n_axes,
                                vmem_limit_bytes=VMEM_LIMIT)


def _full(shape):
    n = len(shape)
    return pl.BlockSpec(shape, lambda *_: (0,) * n)


def _inproj_kernel(x_ref, g1_ref, wglu_ref, bglu_ref, wq_ref, wk_ref, wv_ref, qg_ref, kg_ref,
                   pm_ref, wg_ref, u_ref, *rest, tm):
    qkv_refs, gate_ref, stage = rest[:9], rest[9], rest[10]
    x = x_ref[...]
    ms = jnp.mean(x * x, axis=-1, keepdims=True)
    xn = (x * lax.rsqrt(ms + EPS) * g1_ref[...]).astype(BF16)

    glu = jnp.dot(xn, wglu_ref[...], preferred_element_type=F32) + bglu_ref[...]
    u_ref[...] = glu[:, :CONV_CH] * jax.nn.sigmoid(glu[:, CONV_CH:])

    def head_rmsnorm(w_ref, g_ref):
        y = jnp.dot(xn, w_ref[...], preferred_element_type=F32)
        msq = jnp.dot((y * y).astype(BF16), pm_ref[...], preferred_element_type=F32)
        return y * lax.rsqrt(msq + EPS) * g_ref[...]

    tensors = (head_rmsnorm(wq_ref, qg_ref) * (HEAD_DIM ** -0.5),
               head_rmsnorm(wk_ref, kg_ref),
               jnp.dot(xn, wv_ref[...], preferred_element_type=F32))
    for t, y in enumerate(tensors):
        for g, (_, dil) in enumerate(ATTN_GROUPS):
            out_ref = qkv_refs[t * len(ATTN_GROUPS) + g]
            yg = y[:, g * GROUP_WIDTH:(g + 1) * GROUP_WIDTH]
            if dil == 1:
                out_ref[0] = yg.astype(BF16)
                continue
            for s in range(GROUP_WIDTH // LANES):
                stage[s] = yg[:, s * LANES:(s + 1) * LANES]
            for r in range(dil):
                for s in range(GROUP_WIDTH // LANES):
                    out_ref[r, :, s * LANES:(s + 1) * LANES] = (
                        stage[s, pl.ds(r, tm // dil, stride=dil), :].astype(BF16))
    gate_ref[...] = jax.nn.sigmoid(
        jnp.dot(xn, wg_ref[...], preferred_element_type=F32)).astype(BF16)


def _inproj(x2, g1, wglu, bglu, wq, wk, wv, qg, kg, pm, wg, tm, batch, seq):
    n, d = x2.shape
    tiles_per_seq = seq // tm
    row = lambda w: pl.BlockSpec((tm, w), lambda i: (i, 0))
    grouped = lambda dil: pl.BlockSpec((None, dil, tm // dil, GROUP_WIDTH),
                                       lambda i: (i // tiles_per_seq, 0, i % tiles_per_seq, 0))
    grouped_shape = lambda dil: jax.ShapeDtypeStruct((batch, dil, seq // dil, GROUP_WIDTH), BF16)
    dils = [dil for _ in range(3) for _, dil in ATTN_GROUPS]
    outs = pl.pallas_call(
        functools.partial(_inproj_kernel, tm=tm),
        grid=(n // tm,),
        in_specs=[row(d), _full(g1.shape), _full(wglu.shape), _full(bglu.shape), _full(wq.shape),
                  _full(wk.shape), _full(wv.shape), _full(qg.shape), _full(kg.shape),
                  _full(pm.shape), _full(wg.shape)],
        out_specs=[row(CONV_CH)] + [grouped(dil) for dil in dils] + [row(2 * d)],
        out_shape=[jax.ShapeDtypeStruct((n, CONV_CH), F32)] + [grouped_shape(dil) for dil in dils]
                  + [jax.ShapeDtypeStruct((n, 2 * d), BF16)],
        scratch_shapes=[pltpu.VMEM((GROUP_WIDTH // LANES, tm, LANES), F32)],
        compiler_params=_params(1),
        name="inproj",
    )(x2, g1, wglu, bglu, wq, wk, wv, qg, kg, pm, wg)
    qkv = [list(outs[1 + 3 * t:4 + 3 * t]) for t in range(3)]
    return outs[0], qkv, outs[10]


def _attn_kernel(q_ref, kc_ref, kp_ref, vc_ref, vp_ref, bias_ref, o_ref, lse_ref, kbuf, vbuf,
                 *, nsub):
    pair = pl.program_id(2)
    first_blk = (pl.program_id(3) == 0).astype(jnp.int32)
    kbuf[0:KEY_STEPS, :] = kp_ref[...]
    kbuf[KEY_STEPS:, :] = kc_ref[...]
    vbuf[0:KEY_STEPS, :] = vp_ref[...]
    vbuf[KEY_STEPS:, :] = vc_ref[...]
    lane = lax.broadcasted_iota(jnp.int32, (1, LANES), 1)
    lo = lane < HEAD_DIM
    head_mask = (lo.astype(BF16), (~lo).astype(BF16))
    for sub in range(nsub):
        r0 = sub * KEY_STEPS
        qs = q_ref[r0:r0 + KEY_STEPS, :]
        kw = kbuf[r0:r0 + 2 * KEY_STEPS, :]
        vw = vbuf[r0:r0 + 2 * KEY_STEPS, :]
        outs, lses = [], []
        for hh in range(2):
            s = lax.dot_general(qs * head_mask[hh], kw, _NT, preferred_element_type=F32)
            variant = first_blk if sub == 0 else 0
            s = s + bias_ref[variant, pair * 2 + hh]
            m = jnp.max(s, axis=1, keepdims=True)
            p = jnp.exp(s - m)
            l = jnp.sum(p, axis=1, keepdims=True)
            pv = jnp.dot(p.astype(BF16), vw, preferred_element_type=F32)
            outs.append(pv / l)
            lses.append(m + jnp.log(l))
        o_ref[r0:r0 + KEY_STEPS, :] = jnp.where(lo, outs[0], outs[1]).astype(BF16)
        lse_ref[r0:r0 + KEY_STEPS, :] = jnp.where(lo, lses[0], lses[1])


def _attn_group(q, k, v, bias, dil):
    batch, _, sub_len, _ = q.shape
    tq = min(512, sub_len)
    nsub = tq // KEY_STEPS
    cur = pl.BlockSpec((None, None, tq, LANES), lambda b, r, p, i: (b, r, i, p))
    prev = pl.BlockSpec((None, None, KEY_STEPS, LANES),
                        lambda b, r, p, i: (b, r, jnp.maximum(i * nsub - 1, 0), p))
    return pl.pallas_call(
        functools.partial(_attn_kernel, nsub=nsub),
        grid=(batch, dil, 2, sub_len // tq),
        in_specs=[cur, cur, prev, cur, prev, _full(bias.shape)],
        out_specs=[cur, cur],
        out_shape=[jax.ShapeDtypeStruct(q.shape, BF16), jax.ShapeDtypeStruct(q.shape, F32)],
        scratch_shapes=[pltpu.VMEM((tq + KEY_STEPS, LANES), BF16),
                        pltpu.VMEM((tq + KEY_STEPS, LANES), BF16)],
        compiler_params=_params(4),
        name=f"attn_dil{dil}",
    )(q, k, k, v, v, bias)


def _t5_causal_bucket(distance):
    n = distance.astype(jnp.int32)
    max_exact = REL_BUCKETS // 2
    nf = jnp.maximum(n, 1).astype(F32)
    large = max_exact + (jnp.log(nf / max_exact) / math.log(REL_MAX_DISTANCE / max_exact)
                         * (REL_BUCKETS - max_exact)).astype(jnp.int32)
    large = jnp.minimum(large, REL_BUCKETS - 1)
    return jnp.where(n < max_exact, n, large)


def _band_bias(rel_bias, g, dil):
    steps = jnp.arange(KEY_STEPS + 1)
    per_step = rel_bias[_t5_causal_bucket(steps * dil)][:, g * HEADS_PER_GROUP:(g + 1) * HEADS_PER_GROUP].T
    k = KEY_STEPS
    pad = lambda w: jnp.full((HEADS_PER_GROUP, w), NEG_INF, F32)
    f = jnp.concatenate([pad(k - 1), per_step[:, ::-1].astype(F32), pad(k)], axis=1)
    width = f.shape[1]
    normal = jnp.tile(f, (1, k))[:, :k * (width - 1)].reshape(HEADS_PER_GROUP, k, width - 1)
    normal = normal[:, :, k - 1:k - 1 + 2 * k]
    before_start = np.arange(2 * k)[None, None, :] < k
    at_start = jnp.where(before_start, NEG_INF, normal)
    return jnp.stack([normal, at_start])


def _mixer_kernel(u_ref, halo_ref, o0_ref, o1_ref, o2_ref, l0_ref, l1_ref, l2_ref, gate_ref, x_ref,
                  dw_ref, dwb_ref, lng_ref, lnb_ref, wc_ref, bc_ref, wa_ref, wo_ref,
                  h_ref, uext, order_ref, shift_ref, conv_ref, *, tm, tiles_per_seq):
    d = x_ref.shape[1]

    def token_order(ref, dil, slot):
        if dil == 1:
            return ref[0].astype(F32)
        for r in range(dil):
            blk = ref[r].astype(F32)
            for s in range(GROUP_WIDTH // LANES):
                order_ref[slot, s, pl.ds(r, tm // dil, stride=dil), :] = blk[:, s * LANES:(s + 1) * LANES]
        return jnp.concatenate([order_ref[slot, s] for s in range(GROUP_WIDTH // LANES)], axis=1)

    at_seq_start = pl.program_id(0) % tiles_per_seq == 0
    uext[0:CONV_HALO, :] = jnp.where(at_seq_start, 0.0, halo_ref[...])
    uext[CONV_HALO:, :] = u_ref[...]
    shift_rows = shift_ref.shape[1]
    for b in range(1, SUBLANES):
        shift_ref[b - 1] = uext[b:b + shift_rows, :]

    def conv_rows(rc, carry):
        base = pl.multiple_of(rc * CONV_ROW_CHUNK, CONV_ROW_CHUNK)
        acc = jnp.broadcast_to(dwb_ref[...], (CONV_ROW_CHUNK, CONV_CH))
        for w in range(CONV_WIDTH):
            a, b = divmod(CONV_HALO - (CONV_WIDTH - 1) + w, SUBLANES)
            rows = pl.ds(base + a * SUBLANES, CONV_ROW_CHUNK)
            tap = uext[rows, :] if b == 0 else shift_ref[b - 1, rows, :]
            acc = acc + tap * dw_ref[w:w + 1, :]
        conv_ref[pl.ds(base, CONV_ROW_CHUNK), :] = acc
        return carry

    lax.fori_loop(0, tm // CONV_ROW_CHUNK, conv_rows, 0)
    c = conv_ref[...]
    mu = jnp.mean(c, axis=-1, keepdims=True)
    cc = c - mu
    var = jnp.mean(cc * cc, axis=-1, keepdims=True)
    c = cc * lax.rsqrt(var + EPS) * lng_ref[...] + lnb_ref[...]
    c = c * jax.nn.sigmoid(c)
    conv_out = jnp.dot(c.astype(BF16), wc_ref[...], preferred_element_type=F32) + bc_ref[...]

    dils = [dil for _, dil in ATTN_GROUPS]
    lses = [token_order(ref, dil, 2 * g) for g, (ref, dil) in enumerate(zip((l0_ref, l1_ref, l2_ref), dils))]
    outs = [token_order(ref, dil, 2 * g + 1) for g, (ref, dil) in enumerate(zip((o0_ref, o1_ref, o2_ref), dils))]
    m = jnp.maximum(jnp.maximum(lses[0], lses[1]), lses[2])
    es = [jnp.exp(l - m) for l in lses]
    inv = 1.0 / (es[0] + es[1] + es[2])
    attn_out = jnp.zeros((tm, d), F32)
    for g in range(3):
        og = (outs[g] * (es[g] * inv)).astype(BF16)
        attn_out = attn_out + jnp.dot(og, wa_ref[g * GROUP_WIDTH:(g + 1) * GROUP_WIDTH, :],
                                      preferred_element_type=F32)

    merged = (gate_ref[:, :d].astype(F32) * conv_out + gate_ref[:, d:].astype(F32) * attn_out)
    h_ref[...] = x_ref[...] + jnp.dot(merged.astype(BF16), wo_ref[...], preferred_element_type=F32)


def _mixer(u, outs, lses, gates, x2, dw, dwb, lng, lnb, wc, bc, wa, wo, tm, seq):
    n, d = x2.shape
    row = lambda w: pl.BlockSpec((tm, w), lambda i: (i, 0))
    halo = pl.BlockSpec((CONV_HALO, CONV_CH),
                        lambda i: (jnp.maximum(i * (tm // CONV_HALO) - 1, 0), 0))
    tiles_per_seq = seq // tm
    grouped = lambda dil: pl.BlockSpec((None, dil, tm // dil, GROUP_WIDTH),
                                       lambda i: (i // tiles_per_seq, 0, i % tiles_per_seq, 0))
    group_specs = [grouped(dil) for _, dil in ATTN_GROUPS]
    return pl.pallas_call(
        functools.partial(_mixer_kernel, tm=tm, tiles_per_seq=tiles_per_seq),
        grid=(n // tm,),
        in_specs=[row(CONV_CH), halo] + group_specs * 2 + [row(2 * d), row(d)]
                 + [_full(a.shape) for a in (dw, dwb, lng, lnb, wc, bc, wa, wo)],
        out_specs=row(d),
        out_shape=jax.ShapeDtypeStruct((n, d), F32),
        scratch_shapes=[pltpu.VMEM((tm + CONV_HALO, CONV_CH), F32),
                        pltpu.VMEM((2 * len(ATTN_GROUPS), GROUP_WIDTH // LANES, tm, LANES), F32),
                        pltpu.VMEM((SUBLANES - 1, tm + CONV_HALO - SUBLANES, CONV_CH), F32),
                        pltpu.VMEM((tm, CONV_CH), F32)],
        compiler_params=_params(1),
        name="mixer_out",
    )(u, u, *outs, *lses, gates, x2, dw, dwb, lng, lnb, wc, bc, wa, wo)


_CAND_WIDTHS = tuple(PEER_TOPK // (r1 + 1) for r1 in range(PEER_TOPK))
_CAND_BLOCK_ROWS = 8
_UNRANKED = float(PEER_N_KEYS)


def _pack_rows(x):
    return pltpu.bitcast(x, jnp.uint32)


def _unpack_rows(x):
    return pltpu.bitcast(x, BF16)


def _col_reduce(x, op, reduce_fn):
    parts = [x[i:i + 8] for i in range(0, x.shape[0], 8)]
    while len(parts) > 1:
        nxt = [op(parts[i], parts[i + 1]) for i in range(0, len(parts) - 1, 2)]
        parts = nxt + ([parts[-1]] if len(parts) % 2 else [])
    return reduce_fn(parts[0], axis=0, keepdims=True)


def _col_max(x):
    return _col_reduce(x, jnp.maximum, jnp.max)


def _col_min(x):
    return _col_reduce(x, jnp.minimum, jnp.min)


def _col_sum(x):
    return _col_reduce(x, jnp.add, jnp.sum)


def _extract_top(s, key_id, on_round):
    for r in range(PEER_TOPK):
        m = _col_max(s)
        idx = _col_min(jnp.where(s == m, key_id, 1e9))
        sel = key_id == idx
        s = jnp.where(sel, -jnp.inf, s)
        on_round(r, m, idx, sel)


def _rank_keys(s, key_id, iota16):
    state = [jnp.zeros(iota16.shape, F32), jnp.full(s.shape, _UNRANKED, F32)]

    def on_round(r, m, idx, sel):
        state[0] = jnp.where(iota16 == float(r), m, state[0])
        state[1] = jnp.where(sel, float(r), state[1])

    _extract_top(s, key_id, on_round)
    return state[0], state[1]


_MARK_SCALE = 2.0 ** 100
_CAND_MARK = -_MARK_SCALE


def _top_values_unique(scores, iota16):
    scores = list(scores)
    tops = [jnp.zeros(iota16.shape, F32) for _ in scores]
    for r in range(PEER_TOPK):
        for i, s in enumerate(scores):
            m = _col_max(s)
            scores[i] = jnp.where(s == m, -_MARK_SCALE * (1.0 + r / 32.0), s)
            tops[i] = jnp.where(iota16 == float(r), m, tops[i])
    return scores, tops


def _decode_marks(s):
    marked = s < -0.5 * _MARK_SCALE
    rank = jnp.where(marked, (s * (-1.0 / _MARK_SCALE) - 1.0) * 32.0, _UNRANKED)
    return rank, _col_sum(marked.astype(F32))


def _candidate_sums(ss1, ss2, iota8):
    blocks = [ss1[0:1, :] + ss2]
    for r1 in range(1, _CAND_BLOCK_ROWS):
        blk = ss1[r1:r1 + 1, :] + ss2[0:_CAND_BLOCK_ROWS, :]
        blocks.append(jnp.where(iota8 < float(_CAND_WIDTHS[r1]), blk, -jnp.inf))
    blocks.append(ss1[_CAND_BLOCK_ROWS:, :] + ss2[0:1, :])
    return jnp.concatenate(blocks, axis=0)


def _peer_prep_kernel(h_ref, g2_ref, wqt_ref, keys_ref,
                      hnt_ref, rk2_ref, b_ref, cnt_ref, a_ref,
                      qt_ref, s1_ref, s2_ref, tk_ref, *, tt):
    h = h_ref[...]
    ms = jnp.mean(h * h, axis=-1, keepdims=True)
    hnt = (h * lax.rsqrt(ms + EPS) * g2_ref[...]).T.astype(BF16)
    hnt_ref[...] = _pack_rows(hnt)
    qt_ref[...] = jnp.dot(wqt_ref[...], hnt, preferred_element_type=F32)

    key_id = lax.broadcasted_iota(jnp.int32, (PEER_N_KEYS, LANES), 0).astype(F32)
    iota16 = lax.broadcasted_iota(jnp.int32, (PEER_TOPK, LANES), 0).astype(F32)
    iota8 = lax.broadcasted_iota(jnp.int32, (_CAND_BLOCK_ROWS, LANES), 0).astype(F32)
    cand_id = jnp.concatenate(
        [iota16]
        + [iota8 + float(r1 * PEER_TOPK) for r1 in range(1, _CAND_BLOCK_ROWS)]
        + [(iota8 + float(_CAND_BLOCK_ROWS)) * float(PEER_TOPK)], axis=0)

    def head_body(hd, carry):
        for p, dst_ref in ((0, s1_ref), (1, s2_ref)):
            base = pl.multiple_of((hd * 2 + p) * PEER_N_KEYS, PEER_N_KEYS)
            qhp = qt_ref[pl.ds(base, PEER_N_KEYS), :].astype(BF16)
            dst_ref[...] = jnp.dot(keys_ref[hd, p], qhp, preferred_element_type=F32)

        def lane_group(c, carry2):
            lanes = pl.ds(pl.multiple_of(c * LANES, LANES), LANES)
            s1 = s1_ref[:, lanes]
            s2 = s2_ref[:, lanes]
            k = PEER_TOPK
            row_ss1, row_ss2, row_best, row_cnt = (slice(i * k, (i + 1) * k) for i in range(4))
            row_rk1 = slice(4 * k, 4 * k + PEER_N_KEYS)
            row_rk2 = slice(4 * k + PEER_N_KEYS, 4 * k + 2 * PEER_N_KEYS)

            (m1, m2), (ss1, ss2) = _top_values_unique((s1, s2), iota16)
            rk1, n1 = _decode_marks(m1)
            rk2, n2 = _decode_marks(m2)
            cand = _candidate_sums(ss1, ss2, iota8)
            best = jnp.zeros((k, LANES), F32)
            for r in range(k):
                m = _col_max(cand)
                cand = jnp.where(cand == m, _CAND_MARK, cand)
                best = jnp.where(iota16 == float(r), m, best)
            picked = (cand == _CAND_MARK).astype(F32)
            per_row = [jnp.sum(picked[0:k, :], axis=0, keepdims=True)]
            for r1 in range(1, _CAND_BLOCK_ROWS):
                lo = k + (r1 - 1) * _CAND_BLOCK_ROWS
                per_row.append(jnp.sum(picked[lo:lo + _CAND_BLOCK_ROWS, :], axis=0, keepdims=True))
            cnt16 = jnp.concatenate(per_row + [picked[k + 7 * _CAND_BLOCK_ROWS:, :]], axis=0)
            n3 = jnp.sum(cnt16, axis=0, keepdims=True)
            tk_ref[row_ss1, :] = ss1
            tk_ref[row_ss2, :] = ss2
            tk_ref[row_best, :] = best
            tk_ref[row_cnt, :] = cnt16
            tk_ref[row_rk1, :] = rk1
            tk_ref[row_rk2, :] = rk2
            had_tie = jnp.max(jnp.abs(n1 - k) + jnp.abs(n2 - k) + jnp.abs(n3 - k)) > 0.5

            @pl.when(had_tie)
            def _():
                ss1x, rk1x = _rank_keys(s1, key_id, iota16)
                ss2x, rk2x = _rank_keys(s2, key_id, iota16)
                state = [jnp.zeros((k, LANES), F32), jnp.zeros((k, LANES), F32)]

                def on_round(r, m, idx, sel):
                    state[0] = jnp.where(iota16 == float(r), m, state[0])
                    row = jnp.floor(idx * (1.0 / k))
                    state[1] = state[1] + (iota16 == row).astype(F32)

                _extract_top(_candidate_sums(ss1x, ss2x, iota8), cand_id, on_round)
                tk_ref[row_ss1, :] = ss1x
                tk_ref[row_ss2, :] = ss2x
                tk_ref[row_best, :] = state[0]
                tk_ref[row_cnt, :] = state[1]
                tk_ref[row_rk1, :] = rk1x
                tk_ref[row_rk2, :] = rk2x

            ss1, ss2, best, cnt16 = (tk_ref[rows, :] for rows in (row_ss1, row_ss2, row_best, row_cnt))
            rk1 = tk_ref[row_rk1, :]
            rk2 = tk_ref[row_rk2, :]
            z = jnp.sum(jnp.exp(best - best[0:1, :]), axis=0, keepdims=True)

            a_ref[hd, :, lanes] = jnp.exp(s1 - ss1[0:1, :])
            b_ref[hd, :, lanes] = _pack_rows((jnp.exp(s2 - ss2[0:1, :]) / z).astype(BF16))
            rk2_ref[hd, :, lanes] = _pack_rows(rk2.astype(BF16))
            cnt = jnp.zeros((PEER_N_KEYS, LANES), F32)
            for r1 in range(PEER_TOPK):
                cnt = jnp.where(rk1 == float(r1), cnt16[r1:r1 + 1, :], cnt)
            cnt_ref[hd, :, lanes] = cnt
            return carry2

        lax.fori_loop(0, tt // LANES, lane_group, 0)
        return carry

    lax.fori_loop(0, PEER_HEADS, head_body, 0)


def _peer_prep(h2, g2, wqt, keys, tt):
    n, d = h2.shape
    nt = n // tt
    state = lambda rows: pl.BlockSpec((None, PEER_HEADS, rows, tt), lambda i: (i, 0, 0, 0))
    shape = lambda rows, dt: jax.ShapeDtypeStruct((nt, PEER_HEADS, rows, tt), dt)
    half = PEER_N_KEYS // 2
    return pl.pallas_call(
        functools.partial(_peer_prep_kernel, tt=tt),
        grid=(nt,),
        in_specs=[pl.BlockSpec((tt, d), lambda i: (i, 0)), _full(g2.shape), _full(wqt.shape),
                  _full(keys.shape)],
        out_specs=[pl.BlockSpec((d // 2, tt), lambda i: (0, i)), state(half), state(half),
                   state(PEER_N_KEYS), state(PEER_N_KEYS)],
        out_shape=[jax.ShapeDtypeStruct((d // 2, n), jnp.uint32), shape(half, jnp.uint32),
                   shape(half, jnp.uint32), shape(PEER_N_KEYS, F32), shape(PEER_N_KEYS, F32)],
        scratch_shapes=[pltpu.VMEM((wqt.shape[0], tt), F32),
                        pltpu.VMEM((PEER_N_KEYS, tt), F32),
                        pltpu.VMEM((PEER_N_KEYS, tt), F32),
                        pltpu.VMEM((4 * PEER_TOPK + 2 * PEER_N_KEYS, LANES), F32)],
        compiler_params=_params(1),
        name="peer_prep",
    )(h2, g2, wqt, keys)


def _peer_main_kernel(hnt_ref, rk2_ref, b_ref, cnt_ref, a_ref, u_ref, vt_ref, res_ref,
                      y_ref, yt_ref, ht0_ref, ht1_ref, act0_ref, act1_ref, rows_ref,
                      *, keys_per_chunk, tt, n_chunks, n_items):
    g = pl.program_id(0)
    item3 = g - 2
    first_of_tile = (item3 >= 0) & (item3 % n_chunks == 0)
    last_of_tile = (item3 >= 0) & (item3 % n_chunks == n_chunks - 1)

    @pl.when(g == 0)
    def _():
        for ref in (ht0_ref, ht1_ref):
            ref[...] = jnp.zeros(ref.shape, F32)
        for ref in (act0_ref, act1_ref):
            ref[...] = jnp.zeros(ref.shape, jnp.uint32)

    @pl.when((g == 0) | first_of_tile)
    def _():
        yt_ref[...] = jnp.zeros(yt_ref.shape, F32)

    half = PEER_N_KEYS // 2
    chunk2 = jnp.clip(g - 1, 0, n_items - 1) % n_chunks

    def gate_block(c, k0, ht_old, act_new):
        lanes = slice(c * LANES, (c + 1) * LANES)
        ks = range(k0, k0 + KEYS_PER_PASS)
        gates = {k: jnp.zeros((PEER_N_KEYS, LANES), BF16) for k in ks}
        for hd in range(PEER_HEADS):
            rk2 = _unpack_rows(rk2_ref[hd, :, lanes])
            b = _unpack_rows(b_ref[hd, :, lanes])
            for k in ks:
                cnt_row = rows_ref[k, hd:hd + 1, lanes].astype(BF16)
                a_row = rows_ref[k, PEER_HEADS + hd:PEER_HEADS + hd + 1, lanes].astype(BF16)
                gates[k] = gates[k] + jnp.where(rk2 < cnt_row, b * a_row, 0.0)
        for k in ks:
            x = ht_old[k * PEER_N_KEYS:(k + 1) * PEER_N_KEYS, lanes]
            gelu = 0.5 * x * (1.0 + lax.erf(x * math.sqrt(0.5)))
            act_new[k * half:(k + 1) * half, lanes] = _pack_rows(gelu.astype(BF16) * gates[k])

    def step(ht_new, ht_old, act_new, act_old):
        for k in range(keys_per_chunk):
            i1 = chunk2 * keys_per_chunk + k
            for hd in range(PEER_HEADS):
                rows_ref[k, hd:hd + 1, :] = cnt_ref[hd, pl.ds(i1, 1), :]
                rows_ref[k, PEER_HEADS + hd:PEER_HEADS + hd + 1, :] = a_ref[hd, pl.ds(i1, 1), :]

        def u_piece(piece, q):
            cols = slice(piece * MXU_COLS, (piece + 1) * MXU_COLS)
            rp = u_ref.shape[0] // ROW_SPLIT
            ht_new[2 * q * rp:2 * (q + 1) * rp, cols] = jnp.dot(
                _unpack_rows(u_ref[q * rp:(q + 1) * rp, :]), _unpack_rows(hnt_ref[:, cols]),
                preferred_element_type=F32)

        def v_piece(piece, q):
            cols = slice(piece * MXU_COLS, (piece + 1) * MXU_COLS)
            rp = vt_ref.shape[0] // ROW_SPLIT
            yt_ref[2 * q * rp:2 * (q + 1) * rp, cols] += jnp.dot(
                _unpack_rows(vt_ref[q * rp:(q + 1) * rp, :]), _unpack_rows(act_old[:, cols]),
                preferred_element_type=F32)

        mxu_work = [functools.partial(f, p, q) for p in range(tt // MXU_COLS)
                    for f in (u_piece, v_piece) for q in range(ROW_SPLIT)]
        vpu_work = [functools.partial(gate_block, c, k0, ht_old, act_new)
                    for c in range(tt // LANES) for k0 in range(0, keys_per_chunk, KEYS_PER_PASS)]
        done_m = done_v = 0
        while done_m < len(mxu_work) or done_v < len(vpu_work):
            behind = done_m * len(vpu_work) < done_v * len(mxu_work)
            if (behind and done_m < len(mxu_work)) or done_v == len(vpu_work):
                mxu_work[done_m]()
                done_m += 1
            else:
                vpu_work[done_v]()
                done_v += 1

    @pl.when(g % 2 == 0)
    def _():
        step(ht0_ref, ht1_ref, act1_ref, act0_ref)

    @pl.when(g % 2 == 1)
    def _():
        step(ht1_ref, ht0_ref, act0_ref, act1_ref)

    @pl.when(last_of_tile)
    def _():
        y_ref[...] = res_ref[...] + yt_ref[...].T


def _peer_main(hnt, rk2, b, cnt, a, u_packed, vt_packed, h2, tt, chunk):
    n, d = h2.shape
    n_chunks = 2 * u_packed.shape[0] // chunk
    n_items = (n // tt) * n_chunks
    kpc = chunk // PEER_N_KEYS
    assert kpc % KEYS_PER_PASS == 0 and chunk % (2 * ROW_SPLIT) == 0 and d % (2 * ROW_SPLIT) == 0
    item = lambda g, lag: jnp.clip(g - lag, 0, n_items - 1)
    tile = lambda g, lag: item(g, lag) // n_chunks
    chunk_of = lambda g, lag: item(g, lag) % n_chunks
    state = lambda arr: pl.BlockSpec((None,) + arr.shape[1:], lambda g: (tile(g, 1), 0, 0, 0))
    return pl.pallas_call(
        functools.partial(_peer_main_kernel, keys_per_chunk=kpc, tt=tt, n_chunks=n_chunks,
                          n_items=n_items),
        grid=(n_items + 2,),
        in_specs=[pl.BlockSpec((d // 2, tt), lambda g: (0, tile(g, 0))), state(rk2), state(b),
                  state(cnt), state(a),
                  pl.BlockSpec((chunk // 2, d), lambda g: (chunk_of(g, 0), 0)),
                  pl.BlockSpec((d // 2, chunk), lambda g: (0, chunk_of(g, 2))),
                  pl.BlockSpec((tt, d), lambda g: (tile(g, 2), 0))],
        out_specs=pl.BlockSpec((tt, d), lambda g: (tile(g, 2), 0)),
        out_shape=jax.ShapeDtypeStruct((n, d), F32),
        scratch_shapes=[pltpu.VMEM((d, tt), F32),
                        pltpu.VMEM((chunk, tt), F32),
                        pltpu.VMEM((chunk, tt), F32),
                        pltpu.VMEM((chunk // 2, tt), jnp.uint32),
                        pltpu.VMEM((chunk // 2, tt), jnp.uint32),
                        pltpu.VMEM((kpc, 2 * PEER_HEADS, tt), F32)],
        compiler_params=_params(1),
        name="peer_main",
    )(hnt, rk2, b, cnt, a, u_packed, vt_packed, h2)


def _pack_table_kernel(w_ref, o_ref, *, transpose):
    w = w_ref[...]
    o_ref[...] = _pack_rows((w.T if transpose else w).astype(BF16))


def _pack_table(w, transpose, rows=512):
    r, c = w.shape
    if transpose:
        out_shape, out_spec = (c // 2, r), pl.BlockSpec((c // 2, rows), lambda i: (0, i))
    else:
        out_shape, out_spec = (r // 2, c), pl.BlockSpec((rows // 2, c), lambda i: (i, 0))
    return pl.pallas_call(
        functools.partial(_pack_table_kernel, transpose=transpose),
        grid=(r // rows,),
        in_specs=[pl.BlockSpec((rows, c), lambda i: (i, 0))],
        out_specs=out_spec,
        out_shape=jax.ShapeDtypeStruct(out_shape, jnp.uint32),
        compiler_params=_params(1),
        name="pack_vt" if transpose else "pack_u",
    )(w)


def _layer(x2, batch, seq, w_in, conv_b_glu, conv_dw_w, conv_dw_b, conv_ln_g, conv_ln_b, conv_w_proj,
           conv_b_proj, q_norm_g, k_norm_g, rel_bias, attn_w_proj, mix_w_out, norm1_g, norm2_g,
           peer_w_q, peer_sub_keys, peer_u, peer_v):
    n, d = x2.shape
    row = lambda v: v.reshape(1, -1).astype(F32)
    c0 = 2 * CONV_CH
    cuts = [c0, c0 + ATTN_WIDTH, c0 + 2 * ATTN_WIDTH, c0 + 3 * ATTN_WIDTH]
    wglu, wq, wk, wv, wg = [w.astype(BF16) for w in jnp.split(w_in, cuts, axis=-1)]
    head_of = np.arange(ATTN_WIDTH) // HEAD_DIM
    pm = jnp.asarray((head_of[:, None] == head_of[None, :]) / HEAD_DIM, BF16)
    tile_heads = lambda g: jnp.tile(g.astype(F32), N_ATTN_HEADS).reshape(1, ATTN_WIDTH)

    u, (qs, ks, vs), gates = _inproj(x2, row(norm1_g), wglu, row(conv_b_glu), wq, wk, wv,
                                     tile_heads(q_norm_g), tile_heads(k_norm_g), pm, wg,
                                     tm=256, batch=batch, seq=seq)

    outs, lses = [], []
    for g, (_, dil) in enumerate(ATTN_GROUPS):
        o, lse = _attn_group(qs[g], ks[g], vs[g], _band_bias(rel_bias, g, dil), dil)
        outs.append(o)
        lses.append(lse)

    h2 = _mixer(u, outs, lses, gates, x2, conv_dw_w.reshape(CONV_WIDTH, CONV_CH).astype(F32),
                row(conv_dw_b), row(conv_ln_g), row(conv_ln_b), conv_w_proj.astype(BF16),
                row(conv_b_proj), attn_w_proj.astype(BF16), mix_w_out.astype(BF16),
                tm=min(512, seq), seq=seq)

    tt = min(512, n)
    hnt, rk2, b, cnt, a = _peer_prep(h2, row(norm2_g), peer_w_q.T.astype(BF16),
                                     peer_sub_keys.astype(BF16), tt)
    return _peer_main(hnt, rk2, b, cnt, a, _pack_table(peer_u, transpose=False),
                      _pack_table(peer_v, transpose=True), h2, tt, chunk=1024)


def kernel(x, w_in, conv_b_glu, conv_dw_w, conv_dw_b, conv_ln_g, conv_ln_b, conv_w_proj, conv_b_proj, q_norm_g, k_norm_g, rel_bias, attn_w_proj, mix_w_out, norm1_g, norm2_g, peer_w_q, peer_sub_keys, peer_u, peer_v):
    batch, seq, d = x.shape
    x2 = x.reshape(batch * seq, d)
    for l in range(w_in.shape[0]):
        x2 = _layer(x2, batch, seq, w_in[l], conv_b_glu[l], conv_dw_w[l], conv_dw_b[l], conv_ln_g[l],
                    conv_ln_b[l], conv_w_proj[l], conv_b_proj[l], q_norm_g[l], k_norm_g[l], rel_bias,
                    attn_w_proj[l], mix_w_out[l], norm1_g[l], norm2_g[l], peer_w_q[l],
                    peer_sub_keys[l], peer_u[l], peer_v[l])
    return x2.reshape(batch, seq, d)
```

```python
import functools
import math

import numpy as np
import jax
import jax.numpy as jnp
from jax import lax
from jax.experimental import pallas as pl
from jax.experimental.pallas import tpu as pltpu

F32 = jnp.float32
BF16 = jnp.bfloat16

HEAD_DIM = 64
HEADS_PER_GROUP = 4
ATTN_GROUPS = ((128, 1), (512, 4), (2048, 16))
N_ATTN_HEADS = HEADS_PER_GROUP * len(ATTN_GROUPS)
ATTN_WIDTH = N_ATTN_HEADS * HEAD_DIM
GROUP_WIDTH = HEADS_PER_GROUP * HEAD_DIM
KEY_STEPS = 128
NEG_INF = -1e30
CONV_CH = 512
CONV_WIDTH = 31
CONV_HALO = 32
REL_BUCKETS = 32
REL_MAX_DISTANCE = 2048
PEER_HEADS = 8
PEER_N_KEYS = 128
PEER_TOPK = 16
EPS = 1e-6

LANES = 128
SUBLANES = 8
CONV_ROW_CHUNK = 32
MXU_COLS = 256
ROW_SPLIT = 4
KEYS_PER_PASS = 4
VMEM_LIMIT = 56 * 1024 * 1024

_NT = (((1,), (1,)), ((), ()))


def _params(n_axes):
    return pltpu.CompilerParams(dimension_semantics=("arbitrary",) * n_axes,
                                vmem_limit_bytes=VMEM_LIMIT)


def _full(shape):
    n = len(shape)
    return pl.BlockSpec(shape, lambda *_: (0,) * n)


def _inproj_kernel(x_ref, g1_ref, wglu_ref, bglu_ref, wq_ref, wk_ref, wv_ref, qg_ref, kg_ref,
                   pm_ref, wg_ref, u_ref, *rest, tm):
    qkv_refs, gate_ref, stage = rest[:9], rest[9], rest[10]
    x = x_ref[...]
    ms = jnp.mean(x * x, axis=-1, keepdims=True)
    xn = (x * lax.rsqrt(ms + EPS) * g1_ref[...]).astype(BF16)

    glu = jnp.dot(xn, wglu_ref[...], preferred_element_type=F32) + bglu_ref[...]
    u_ref[...] = glu[:, :CONV_CH] * jax.nn.sigmoid(glu[:, CONV_CH:])

    def head_rmsnorm(w_ref, g_ref):
        y = jnp.dot(xn, w_ref[...], preferred_element_type=F32)
        msq = jnp.dot((y * y).astype(BF16), pm_ref[...], preferred_element_type=F32)
        return y * lax.rsqrt(msq + EPS) * g_ref[...]

    tensors = (head_rmsnorm(wq_ref, qg_ref) * (HEAD_DIM ** -0.5),
               head_rmsnorm(wk_ref, kg_ref),
               jnp.dot(xn, wv_ref[...], preferred_element_type=F32))
    for t, y in enumerate(tensors):
        for g, (_, dil) in enumerate(ATTN_GROUPS):
            out_ref = qkv_refs[t * len(ATTN_GROUPS) + g]
            yg = y[:, g * GROUP_WIDTH:(g + 1) * GROUP_WIDTH]
            if dil == 1:
                out_ref[0] = yg.astype(BF16)
                continue
            for s in range(GROUP_WIDTH // LANES):
                stage[s] = yg[:, s * LANES:(s + 1) * LANES]
            for r in range(dil):
                for s in range(GROUP_WIDTH // LANES):
                    out_ref[r, :, s * LANES:(s + 1) * LANES] = (
                        stage[s, pl.ds(r, tm // dil, stride=dil), :].astype(BF16))
    gate_ref[...] = jax.nn.sigmoid(
        jnp.dot(xn, wg_ref[...], preferred_element_type=F32)).astype(BF16)


def _inproj(x2, g1, wglu, bglu, wq, wk, wv, qg, kg, pm, wg, tm, batch, seq):
    n, d = x2.shape
    tiles_per_seq = seq // tm
    row = lambda w: pl.BlockSpec((tm, w), lambda i: (i, 0))
    grouped = lambda dil: pl.BlockSpec((None, dil, tm // dil, GROUP_WIDTH),
                                       lambda i: (i // tiles_per_seq, 0, i % tiles_per_seq, 0))
    grouped_shape = lambda dil: jax.ShapeDtypeStruct((batch, dil, seq // dil, GROUP_WIDTH), BF16)
    dils = [dil for _ in range(3) for _, dil in ATTN_GROUPS]
    outs = pl.pallas_call(
        functools.partial(_inproj_kernel, tm=tm),
        grid=(n // tm,),
        in_specs=[row(d), _full(g1.shape), _full(wglu.shape), _full(bglu.shape), _full(wq.shape),
                  _full(wk.shape), _full(wv.shape), _full(qg.shape), _full(kg.shape),
                  _full(pm.shape), _full(wg.shape)],
        out_specs=[row(CONV_CH)] + [grouped(dil) for dil in dils] + [row(2 * d)],
        out_shape=[jax.ShapeDtypeStruct((n, CONV_CH), F32)] + [grouped_shape(dil) for dil in dils]
                  + [jax.ShapeDtypeStruct((n, 2 * d), BF16)],
        scratch_shapes=[pltpu.VMEM((GROUP_WIDTH // LANES, tm, LANES), F32)],
        compiler_params=_params(1),
        name="inproj",
    )(x2, g1, wglu, bglu, wq, wk, wv, qg, kg, pm, wg)
    qkv = [list(outs[1 + 3 * t:4 + 3 * t]) for t in range(3)]
    return outs[0], qkv, outs[10]


def _attn_kernel(q_ref, kc_ref, kp_ref, vc_ref, vp_ref, bias_ref, o_ref, lse_ref, kbuf, vbuf,
                 *, nsub):
    pair = pl.program_id(2)
    first_blk = (pl.program_id(3) == 0).astype(jnp.int32)
    kbuf[0:KEY_STEPS, :] = kp_ref[...]
    kbuf[KEY_STEPS:, :] = kc_ref[...]
    vbuf[0:KEY_STEPS, :] = vp_ref[...]
    vbuf[KEY_STEPS:, :] = vc_ref[...]
    lane = lax.broadcasted_iota(jnp.int32, (1, LANES), 1)
    lo = lane < HEAD_DIM
    head_mask = (lo.astype(BF16), (~lo).astype(BF16))
    for sub in range(nsub):
        r0 = sub * KEY_STEPS
        qs = q_ref[r0:r0 + KEY_STEPS, :]
        kw = kbuf[r0:r0 + 2 * KEY_STEPS, :]
        vw = vbuf[r0:r0 + 2 * KEY_STEPS, :]
        outs, lses = [], []
        for hh in range(2):
            s = lax.dot_general(qs * head_mask[hh], kw, _NT, preferred_element_type=F32)
            variant = first_blk if sub == 0 else 0
            s = s + bias_ref[variant, pair * 2 + hh]
            m = jnp.max(s, axis=1, keepdims=True)
            p = jnp.exp(s - m)
            l = jnp.sum(p, axis=1, keepdims=True)
            pv = jnp.dot(p.astype(BF16), vw, preferred_element_type=F32)
            outs.append(pv / l)
            lses.append(m + jnp.log(l))
        o_ref[r0:r0 + KEY_STEPS, :] = jnp.where(lo, outs[0], outs[1]).astype(BF16)
        lse_ref[r0:r0 + KEY_STEPS, :] = jnp.where(lo, lses[0], lses[1])


def _attn_group(q, k, v, bias, dil):
    batch, _, sub_len, _ = q.shape
    tq = min(512, sub_len)
    nsub = tq // KEY_STEPS
    cur = pl.BlockSpec((None, None, tq, LANES), lambda b, r, p, i: (b, r, i, p))
    prev = pl.BlockSpec((None, None, KEY_STEPS, LANES),
                        lambda b, r, p, i: (b, r, jnp.maximum(i * nsub - 1, 0), p))
    return pl.pallas_call(
        functools.partial(_attn_kernel, nsub=nsub),
        grid=(batch, dil, 2, sub_len // tq),
        in_specs=[cur, cur, prev, cur, prev, _full(bias.shape)],
        out_specs=[cur, cur],
        out_shape=[jax.ShapeDtypeStruct(q.shape, BF16), jax.ShapeDtypeStruct(q.shape, F32)],
        scratch_shapes=[pltpu.VMEM((tq + KEY_STEPS, LANES), BF16),
                        pltpu.VMEM((tq + KEY_STEPS, LANES), BF16)],
        compiler_params=_params(4),
        name=f"attn_dil{dil}",
    )(q, k, k, v, v, bias)


def _t5_causal_bucket(distance):
    n = distance.astype(jnp.int32)
    max_exact = REL_BUCKETS // 2
    nf = jnp.maximum(n, 1).astype(F32)
    large = max_exact + (jnp.log(nf / max_exact) / math.log(REL_MAX_DISTANCE / max_exact)
                         * (REL_BUCKETS - max_exact)).astype(jnp.int32)
    large = jnp.minimum(large, REL_BUCKETS - 1)
    return jnp.where(n < max_exact, n, large)


def _band_bias(rel_bias, g, dil):
    steps = jnp.arange(KEY_STEPS + 1)
    per_step = rel_bias[_t5_causal_bucket(steps * dil)][:, g * HEADS_PER_GROUP:(g + 1) * HEADS_PER_GROUP].T
    k = KEY_STEPS
    pad = lambda w: jnp.full((HEADS_PER_GROUP, w), NEG_INF, F32)
    f = jnp.concatenate([pad(k - 1), per_step[:, ::-1].astype(F32), pad(k)], axis=1)
    width = f.shape[1]
    normal = jnp.tile(f, (1, k))[:, :k * (width - 1)].reshape(HEADS_PER_GROUP, k, width - 1)
    normal = normal[:, :, k - 1:k - 1 + 2 * k]
    before_start = np.arange(2 * k)[None, None, :] < k
    at_start = jnp.where(before_start, NEG_INF, normal)
    return jnp.stack([normal, at_start])


def _mixer_kernel(u_ref, halo_ref, o0_ref, o1_ref, o2_ref, l0_ref, l1_ref, l2_ref, gate_ref, x_ref,
                  dw_ref, dwb_ref, lng_ref, lnb_ref, wc_ref, bc_ref, wa_ref, wo_ref,
                  h_ref, uext, order_ref, shift_ref, conv_ref, *, tm, tiles_per_seq):
    d = x_ref.shape[1]

    def token_order(ref, dil, slot):
        if dil == 1:
            return ref[0].astype(F32)
        for r in range(dil):
            blk = ref[r].astype(F32)
            for s in range(GROUP_WIDTH // LANES):
                order_ref[slot, s, pl.ds(r, tm // dil, stride=dil), :] = blk[:, s * LANES:(s + 1) * LANES]
        return jnp.concatenate([order_ref[slot, s] for s in range(GROUP_WIDTH // LANES)], axis=1)

    at_seq_start = pl.program_id(0) % tiles_per_seq == 0
    uext[0:CONV_HALO, :] = jnp.where(at_seq_start, 0.0, halo_ref[...])
    uext[CONV_HALO:, :] = u_ref[...]
    shift_rows = shift_ref.shape[1]
    for b in range(1, SUBLANES):
        shift_ref[b - 1] = uext[b:b + shift_rows, :]

    def conv_rows(rc, carry):
        base = pl.multiple_of(rc * CONV_ROW_CHUNK, CONV_ROW_CHUNK)
        acc = jnp.broadcast_to(dwb_ref[...], (CONV_ROW_CHUNK, CONV_CH))
        for w in range(CONV_WIDTH):
            a, b = divmod(CONV_HALO - (CONV_WIDTH - 1) + w, SUBLANES)
            rows = pl.ds(base + a * SUBLANES, CONV_ROW_CHUNK)
            tap = uext[rows, :] if b == 0 else shift_ref[b - 1, rows, :]
            acc = acc + tap * dw_ref[w:w + 1, :]
        conv_ref[pl.ds(base, CONV_ROW_CHUNK), :] = acc
        return carry

    lax.fori_loop(0, tm // CONV_ROW_CHUNK, conv_rows, 0)
    c = conv_ref[...]
    mu = jnp.mean(c, axis=-1, keepdims=True)
    cc = c - mu
    var = jnp.mean(cc * cc, axis=-1, keepdims=True)
    c = cc * lax.rsqrt(var + EPS) * lng_ref[...] + lnb_ref[...]
    c = c * jax.nn.sigmoid(c)
    conv_out = jnp.dot(c.astype(BF16), wc_ref[...], preferred_element_type=F32) + bc_ref[...]

    dils = [dil for _, dil in ATTN_GROUPS]
    lses = [token_order(ref, dil, 2 * g) for g, (ref, dil) in enumerate(zip((l0_ref, l1_ref, l2_ref), dils))]
    outs = [token_order(ref, dil, 2 * g + 1) for g, (ref, dil) in enumerate(zip((o0_ref, o1_ref, o2_ref), dils))]
    m = jnp.maximum(jnp.maximum(lses[0], lses[1]), lses[2])
    es = [jnp.exp(l - m) for l in lses]
    inv = 1.0 / (es[0] + es[1] + es[2])
    attn_out = jnp.zeros((tm, d), F32)
    for g in range(3):
        og = (outs[g] * (es[g] * inv)).astype(BF16)
        attn_out = attn_out + jnp.dot(og, wa_ref[g * GROUP_WIDTH:(g + 1) * GROUP_WIDTH, :],
                                      preferred_element_type=F32)

    merged = (gate_ref[:, :d].astype(F32) * conv_out + gate_ref[:, d:].astype(F32) * attn_out)
    h_ref[...] = x_ref[...] + jnp.dot(merged.astype(BF16), wo_ref[...], preferred_element_type=F32)


def _mixer(u, outs, lses, gates, x2, dw, dwb, lng, lnb, wc, bc, wa, wo, tm, seq):
    n, d = x2.shape
    row = lambda w: pl.BlockSpec((tm, w), lambda i: (i, 0))
    halo = pl.BlockSpec((CONV_HALO, CONV_CH),
                        lambda i: (jnp.maximum(i * (tm // CONV_HALO) - 1, 0), 0))
    tiles_per_seq = seq // tm
    grouped = lambda dil: pl.BlockSpec((None, dil, tm // dil, GROUP_WIDTH),
                                       lambda i: (i // tiles_per_seq, 0, i % tiles_per_seq, 0))
    group_specs = [grouped(dil) for _, dil in ATTN_GROUPS]
    return pl.pallas_call(
        functools.partial(_mixer_kernel, tm=tm, tiles_per_seq=tiles_per_seq),
        grid=(n // tm,),
        in_specs=[row(CONV_CH), halo] + group_specs * 2 + [row(2 * d), row(d)]
                 + [_full(a.shape) for a in (dw, dwb, lng, lnb, wc, bc, wa, wo)],
        out_specs=row(d),
        out_shape=jax.ShapeDtypeStruct((n, d), F32),
        scratch_shapes=[pltpu.VMEM((tm + CONV_HALO, CONV_CH), F32),
                        pltpu.VMEM((2 * len(ATTN_GROUPS), GROUP_WIDTH // LANES, tm, LANES), F32),
                        pltpu.VMEM((SUBLANES - 1, tm + CONV_HALO - SUBLANES, CONV_CH), F32),
                        pltpu.VMEM((tm, CONV_CH), F32)],
        compiler_params=_params(1),
        name="mixer_out",
    )(u, u, *outs, *lses, gates, x2, dw, dwb, lng, lnb, wc, bc, wa, wo)


_CAND_WIDTHS = tuple(PEER_TOPK // (r1 + 1) for r1 in range(PEER_TOPK))
_CAND_BLOCK_ROWS = 8
_UNRANKED = float(PEER_N_KEYS)


def _pack_rows(x):
    return pltpu.bitcast(x, jnp.uint32)


def _unpack_rows(x):
    return pltpu.bitcast(x, BF16)


def _col_reduce(x, op, reduce_fn):
    parts = [x[i:i + 8] for i in range(0, x.shape[0], 8)]
    while len(parts) > 1:
        nxt = [op(parts[i], parts[i + 1]) for i in range(0, len(parts) - 1, 2)]
        parts = nxt + ([parts[-1]] if len(parts) % 2 else [])
    return reduce_fn(parts[0], axis=0, keepdims=True)


def _col_max(x):
    return _col_reduce(x, jnp.maximum, jnp.max)


def _col_min(x):
    return _col_reduce(x, jnp.minimum, jnp.min)


def _col_sum(x):
    return _col_reduce(x, jnp.add, jnp.sum)


def _extract_top(s, key_id, on_round):
    for r in range(PEER_TOPK):
        m = _col_max(s)
        idx = _col_min(jnp.where(s == m, key_id, 1e9))
        sel = key_id == idx
        s = jnp.where(sel, -jnp.inf, s)
        on_round(r, m, idx, sel)


def _rank_keys(s, key_id, iota16):
    state = [jnp.zeros(iota16.shape, F32), jnp.full(s.shape, _UNRANKED, F32)]

    def on_round(r, m, idx, sel):
        state[0] = jnp.where(iota16 == float(r), m, state[0])
        state[1] = jnp.where(sel, float(r), state[1])

    _extract_top(s, key_id, on_round)
    return state[0], state[1]


_MARK_SCALE = 2.0 ** 100
_CAND_MARK = -_MARK_SCALE


def _top_values_unique(scores, iota16):
    scores = list(scores)
    tops = [jnp.zeros(iota16.shape, F32) for _ in scores]
    for r in range(PEER_TOPK):
        for i, s in enumerate(scores):
            m = _col_max(s)
            scores[i] = jnp.where(s == m, -_MARK_SCALE * (1.0 + r / 32.0), s)
            tops[i] = jnp.where(iota16 == float(r), m, tops[i])
    return scores, tops


def _decode_marks(s):
    marked = s < -0.5 * _MARK_SCALE
    rank = jnp.where(marked, (s * (-1.0 / _MARK_SCALE) - 1.0) * 32.0, _UNRANKED)
    return rank, _col_sum(marked.astype(F32))


def _candidate_sums(ss1, ss2, iota8):
    blocks = [ss1[0:1, :] + ss2]
    for r1 in range(1, _CAND_BLOCK_ROWS):
        blk = ss1[r1:r1 + 1, :] + ss2[0:_CAND_BLOCK_ROWS, :]
        blocks.append(jnp.where(iota8 < float(_CAND_WIDTHS[r1]), blk, -jnp.inf))
    blocks.append(ss1[_CAND_BLOCK_ROWS:, :] + ss2[0:1, :])
    return jnp.concatenate(blocks, axis=0)


def _selection_iotas():
    key_id = lax.broadcasted_iota(jnp.int32, (PEER_N_KEYS, LANES), 0).astype(F32)
    iota16 = lax.broadcasted_iota(jnp.int32, (PEER_TOPK, LANES), 0).astype(F32)
    iota8 = lax.broadcasted_iota(jnp.int32, (_CAND_BLOCK_ROWS, LANES), 0).astype(F32)
    cand_id = jnp.concatenate(
        [iota16]
        + [iota8 + float(r1 * PEER_TOPK) for r1 in range(1, _CAND_BLOCK_ROWS)]
        + [(iota8 + float(_CAND_BLOCK_ROWS)) * float(PEER_TOPK)], axis=0)
    return key_id, iota16, iota8, cand_id


def _select_unique(s1, s2):
    _, iota16, iota8, _ = _selection_iotas()
    k = PEER_TOPK
    (m1, m2), (ss1, ss2) = _top_values_unique((s1, s2), iota16)
    rk1, n1 = _decode_marks(m1)
    rk2, n2 = _decode_marks(m2)
    cand = _candidate_sums(ss1, ss2, iota8)
    best = jnp.zeros((k, LANES), F32)
    for r in range(k):
        m = _col_max(cand)
        cand = jnp.where(cand == m, _CAND_MARK, cand)
        best = jnp.where(iota16 == float(r), m, best)
    picked = (cand == _CAND_MARK).astype(F32)
    per_row = [jnp.sum(picked[0:k, :], axis=0, keepdims=True)]
    for r1 in range(1, _CAND_BLOCK_ROWS):
        lo = k + (r1 - 1) * _CAND_BLOCK_ROWS
        per_row.append(jnp.sum(picked[lo:lo + _CAND_BLOCK_ROWS, :], axis=0, keepdims=True))
    cnt16 = jnp.concatenate(per_row + [picked[k + 7 * _CAND_BLOCK_ROWS:, :]], axis=0)
    n3 = jnp.sum(cnt16, axis=0, keepdims=True)
    tie = jnp.abs(n1 - k) + jnp.abs(n2 - k) + jnp.abs(n3 - k)
    return ss1, ss2, best, cnt16, rk1, rk2, tie


def _select_exact(s1, s2):
    key_id, iota16, iota8, cand_id = _selection_iotas()
    k = PEER_TOPK
    ss1, rk1 = _rank_keys(s1, key_id, iota16)
    ss2, rk2 = _rank_keys(s2, key_id, iota16)
    state = [jnp.zeros((k, LANES), F32), jnp.zeros((k, LANES), F32)]

    def on_round(r, m, idx, sel):
        state[0] = jnp.where(iota16 == float(r), m, state[0])
        row = jnp.floor(idx * (1.0 / k))
        state[1] = state[1] + (iota16 == row).astype(F32)

    _extract_top(_candidate_sums(ss1, ss2, iota8), cand_id, on_round)
    return ss1, ss2, state[0], state[1], rk1, rk2


def _store_gate_state(sel, s1, s2, slot, hd, lanes, rk2_s, b_s, cnt_s, a_s):
    ss1, ss2, best, cnt16, rk1, rk2 = sel
    z = jnp.sum(jnp.exp(best - best[0:1, :]), axis=0, keepdims=True)
    a_s[slot, hd, :, lanes] = jnp.exp(s1 - ss1[0:1, :])
    b_s[slot, hd, :, lanes] = _pack_rows((jnp.exp(s2 - ss2[0:1, :]) / z).astype(BF16))
    rk2_s[slot, hd, :, lanes] = _pack_rows(rk2.astype(BF16))
    cnt = jnp.zeros((PEER_N_KEYS, LANES), F32)
    for r1 in range(PEER_TOPK):
        cnt = jnp.where(rk1 == float(r1), cnt16[r1:r1 + 1, :], cnt)
    cnt_s[slot, hd, :, lanes] = cnt


def _peer_scores_kernel(h_ref, g2_ref, wqt_ref, keys_ref, hnt_ref, sc_ref, qt_ref):
    h = h_ref[...]
    ms = jnp.mean(h * h, axis=-1, keepdims=True)
    hnt = (h * lax.rsqrt(ms + EPS) * g2_ref[...]).T.astype(BF16)
    hnt_ref[...] = _pack_rows(hnt)
    qt_ref[...] = jnp.dot(wqt_ref[...], hnt, preferred_element_type=F32)
    for hp in range(2 * PEER_HEADS):
        qhp = qt_ref[hp * PEER_N_KEYS:(hp + 1) * PEER_N_KEYS, :].astype(BF16)
        sc_ref[hp] = jnp.dot(keys_ref[hp // 2, hp % 2], qhp, preferred_element_type=F32)


def _peer_scores(h2, g2, wqt, keys, tt):
    n, d = h2.shape
    nt = n // tt
    return pl.pallas_call(
        _peer_scores_kernel,
        grid=(nt,),
        in_specs=[pl.BlockSpec((tt, d), lambda i: (i, 0)), _full(g2.shape), _full(wqt.shape),
                  _full(keys.shape)],
        out_specs=[pl.BlockSpec((d // 2, tt), lambda i: (0, i)),
                   pl.BlockSpec((None, 2 * PEER_HEADS, PEER_N_KEYS, tt), lambda i: (i, 0, 0, 0))],
        out_shape=[jax.ShapeDtypeStruct((d // 2, n), jnp.uint32),
                   jax.ShapeDtypeStruct((nt, 2 * PEER_HEADS, PEER_N_KEYS, tt), F32)],
        scratch_shapes=[pltpu.VMEM((wqt.shape[0], tt), F32)],
        compiler_params=_params(1),
        name="peer_scores",
    )(h2, g2, wqt, keys)


def _peer_main_kernel(hnt_ref, sc_ref, u_ref, vt_ref, res_ref, y_ref,
                      yt_ref, ht0_ref, ht1_ref, act0_ref, act1_ref, rows_ref,
                      rk2_s, b_s, cnt_s, a_s, tie_ref,
                      *, keys_per_chunk, tt, n_chunks, n_items, n_tiles):
    g = pl.program_id(0)
    m = g - n_chunks
    item3 = m - 2
    first_of_tile = (item3 >= 0) & (item3 % n_chunks == 0)
    last_of_tile = (item3 >= 0) & (item3 % n_chunks == n_chunks - 1)
    lane_groups = tt // LANES
    tasks_per_step = PEER_HEADS * lane_groups // n_chunks

    @pl.when(g == 0)
    def _():
        for ref in (ht0_ref, ht1_ref):
            ref[...] = jnp.zeros(ref.shape, F32)
        for ref in (act0_ref, act1_ref):
            ref[...] = jnp.zeros(ref.shape, jnp.uint32)
        tie_ref[...] = jnp.zeros(tie_ref.shape, F32)
        for ref in (rk2_s, b_s, cnt_s, a_s):
            ref[...] = jnp.zeros(ref.shape, ref.dtype)

    @pl.when((m == 0) | first_of_tile)
    def _():
        yt_ref[...] = jnp.zeros(yt_ref.shape, F32)

    sel_step = jnp.maximum(g - 1, 0)
    sel_slot = (sel_step // n_chunks) % 2
    first_task = (sel_step % n_chunks) * tasks_per_step

    def select_task(t):
        task = first_task + t
        hd = task // lane_groups
        lanes = pl.ds(pl.multiple_of((task % lane_groups) * LANES, LANES), LANES)
        s1 = sc_ref[2 * hd, :, lanes]
        s2 = sc_ref[2 * hd + 1, :, lanes]
        *sel, tie = _select_unique(s1, s2)
        _store_gate_state(sel, s1, s2, sel_slot, hd, lanes, rk2_s, b_s, cnt_s, a_s)
        tie_ref[...] = jnp.maximum(tie_ref[...], tie)

    half = PEER_N_KEYS // 2
    item2 = jnp.clip(m - 1, 0, n_items - 1)
    chunk2 = item2 % n_chunks
    slot2 = (item2 // n_chunks) % 2

    def gate_block(c, k0, ht_old, act_new):
        lanes = slice(c * LANES, (c + 1) * LANES)
        ks = range(k0, k0 + KEYS_PER_PASS)
        gates = {k: jnp.zeros((PEER_N_KEYS, LANES), BF16) for k in ks}
        for hd in range(PEER_HEADS):
            rk2 = _unpack_rows(rk2_s[slot2, hd, :, lanes])
            b = _unpack_rows(b_s[slot2, hd, :, lanes])
            for k in ks:
                cnt_row = rows_ref[k, hd:hd + 1, lanes].astype(BF16)
                a_row = rows_ref[k, PEER_HEADS + hd:PEER_HEADS + hd + 1, lanes].astype(BF16)
                gates[k] = gates[k] + jnp.where(rk2 < cnt_row, b * a_row, 0.0)
        for k in ks:
            x = ht_old[k * PEER_N_KEYS:(k + 1) * PEER_N_KEYS, lanes]
            gelu = 0.5 * x * (1.0 + lax.erf(x * math.sqrt(0.5)))
            act_new[k * half:(k + 1) * half, lanes] = _pack_rows(gelu.astype(BF16) * gates[k])

    def step(ht_new, ht_old, act_new, act_old):
        for k in range(keys_per_chunk):
            i1 = chunk2 * keys_per_chunk + k
            for hd in range(PEER_HEADS):
                rows_ref[k, hd:hd + 1, :] = cnt_s[slot2, hd, pl.ds(i1, 1), :]
                rows_ref[k, PEER_HEADS + hd:PEER_HEADS + hd + 1, :] = a_s[slot2, hd, pl.ds(i1, 1), :]

        def u_piece(piece, q):
            cols = slice(piece * MXU_COLS, (piece + 1) * MXU_COLS)
            rp = u_ref.shape[0] // ROW_SPLIT
            ht_new[2 * q * rp:2 * (q + 1) * rp, cols] = jnp.dot(
                _unpack_rows(u_ref[q * rp:(q + 1) * rp, :]), _unpack_rows(hnt_ref[:, cols]),
                preferred_element_type=F32)

        def v_piece(piece, q):
            cols = slice(piece * MXU_COLS, (piece + 1) * MXU_COLS)
            rp = vt_ref.shape[0] // ROW_SPLIT
            yt_ref[2 * q * rp:2 * (q + 1) * rp, cols] += jnp.dot(
                _unpack_rows(vt_ref[q * rp:(q + 1) * rp, :]), _unpack_rows(act_old[:, cols]),
                preferred_element_type=F32)

        mxu_work = [functools.partial(f, p, q) for p in range(tt // MXU_COLS)
                    for f in (u_piece, v_piece) for q in range(ROW_SPLIT)]
        gates = [functools.partial(gate_block, c, k0, ht_old, act_new)
                 for c in range(tt // LANES) for k0 in range(0, keys_per_chunk, KEYS_PER_PASS)]
        selects = [functools.partial(select_task, t) for t in range(tasks_per_step)]
        every = len(gates) // len(selects)
        vpu_work = []
        for i, gate in enumerate(gates):
            vpu_work.append(gate)
            if i % every == every - 1:
                vpu_work.append(selects[i // every])
        done_m = done_v = 0
        while done_m < len(mxu_work) or done_v < len(vpu_work):
            behind = done_m * len(vpu_work) < done_v * len(mxu_work)
            if (behind and done_m < len(mxu_work)) or done_v == len(vpu_work):
                mxu_work[done_m]()
                done_m += 1
            else:
                vpu_work[done_v]()
                done_v += 1

    @pl.when((m >= 0) & (m % 2 == 0))
    def _():
        step(ht0_ref, ht1_ref, act1_ref, act0_ref)

    @pl.when((m >= 0) & (m % 2 == 1))
    def _():
        step(ht1_ref, ht0_ref, act0_ref, act1_ref)

    @pl.when(m < 0)
    def _():
        for t in range(tasks_per_step):
            select_task(t)

    @pl.when((g >= 1) & (sel_step % n_chunks == n_chunks - 1))
    def _():
        @pl.when(jnp.max(tie_ref[...]) > 0.5)
        def _():
            def redo(task, carry):
                hd = task // lane_groups
                lanes = pl.ds(pl.multiple_of((task % lane_groups) * LANES, LANES), LANES)
                s1 = sc_ref[2 * hd, :, lanes]
                s2 = sc_ref[2 * hd + 1, :, lanes]
                _store_gate_state(_select_exact(s1, s2), s1, s2, sel_slot, hd, lanes,
                                  rk2_s, b_s, cnt_s, a_s)
                return carry

            lax.fori_loop(0, PEER_HEADS * lane_groups, redo, 0)

        tie_ref[...] = jnp.zeros(tie_ref.shape, F32)

    @pl.when(last_of_tile)
    def _():
        y_ref[...] = res_ref[...] + yt_ref[...].T


def _peer_main(hnt, scores, u_packed, vt_packed, h2, tt, chunk):
    n, d = h2.shape
    n_tiles = n // tt
    n_chunks = 2 * u_packed.shape[0] // chunk
    n_items = n_tiles * n_chunks
    kpc = chunk // PEER_N_KEYS
    assert kpc % KEYS_PER_PASS == 0 and chunk % (2 * ROW_SPLIT) == 0 and d % (2 * ROW_SPLIT) == 0
    assert (PEER_HEADS * (tt // LANES)) % n_chunks == 0
    item = lambda g, lag: jnp.clip(g - n_chunks - lag, 0, n_items - 1)
    tile = lambda g, lag: item(g, lag) // n_chunks
    chunk_of = lambda g, lag: item(g, lag) % n_chunks
    sel_tile = lambda g: jnp.minimum(jnp.maximum(g - 1, 0) // n_chunks, n_tiles - 1)
    half = PEER_N_KEYS // 2
    state = lambda rows, dt: pltpu.VMEM((2, PEER_HEADS, rows, tt), dt)
    return pl.pallas_call(
        functools.partial(_peer_main_kernel, keys_per_chunk=kpc, tt=tt, n_chunks=n_chunks,
                          n_items=n_items, n_tiles=n_tiles),
        grid=(n_chunks + n_items + 2,),
        in_specs=[pl.BlockSpec((d // 2, tt), lambda g: (0, tile(g, 0))),
                  pl.BlockSpec((None,) + scores.shape[1:], lambda g: (sel_tile(g), 0, 0, 0)),
                  pl.BlockSpec((chunk // 2, d), lambda g: (chunk_of(g, 0), 0)),
                  pl.BlockSpec((d // 2, chunk), lambda g: (0, chunk_of(g, 2))),
                  pl.BlockSpec((tt, d), lambda g: (tile(g, 2), 0))],
        out_specs=pl.BlockSpec((tt, d), lambda g: (tile(g, 2), 0)),
        out_shape=jax.ShapeDtypeStruct((n, d), F32),
        scratch_shapes=[pltpu.VMEM((d, tt), F32),
                        pltpu.VMEM((chunk, tt), F32),
                        pltpu.VMEM((chunk, tt), F32),
                        pltpu.VMEM((chunk // 2, tt), jnp.uint32),
                        pltpu.VMEM((chunk // 2, tt), jnp.uint32),
                        pltpu.VMEM((kpc, 2 * PEER_HEADS, tt), F32),
                        state(half, jnp.uint32),
                        state(half, jnp.uint32),
                        state(PEER_N_KEYS, F32),
                        state(PEER_N_KEYS, F32),
                        pltpu.VMEM((1, LANES), F32)],
        compiler_params=_params(1),
        name="peer_main",
    )(hnt, scores, u_packed, vt_packed, h2)


def _pack_table_kernel(w_ref, o_ref, *, transpose):
    w = w_ref[...]
    o_ref[...] = _pack_rows((w.T if transpose else w).astype(BF16))


def _pack_table(w, transpose, rows=512):
    r, c = w.shape
    if transpose:
        out_shape, out_spec = (c // 2, r), pl.BlockSpec((c // 2, rows), lambda i: (0, i))
    else:
        out_shape, out_spec = (r // 2, c), pl.BlockSpec((rows // 2, c), lambda i: (i, 0))
    return pl.pallas_call(
        functools.partial(_pack_table_kernel, transpose=transpose),
        grid=(r // rows,),
        in_specs=[pl.BlockSpec((rows, c), lambda i: (i, 0))],
        out_specs=out_spec,
        out_shape=jax.ShapeDtypeStruct(out_shape, jnp.uint32),
        compiler_params=_params(1),
        name="pack_vt" if transpose else "pack_u",
    )(w)


def _layer(x2, batch, seq, w_in, conv_b_glu, conv_dw_w, conv_dw_b, conv_ln_g, conv_ln_b, conv_w_proj,
           conv_b_proj, q_norm_g, k_norm_g, rel_bias, attn_w_proj, mix_w_out, norm1_g, norm2_g,
           peer_w_q, peer_sub_keys, peer_u, peer_v):
    n, d = x2.shape
    row = lambda v: v.reshape(1, -1).astype(F32)
    c0 = 2 * CONV_CH
    cuts = [c0, c0 + ATTN_WIDTH, c0 + 2 * ATTN_WIDTH, c0 + 3 * ATTN_WIDTH]
    wglu, wq, wk, wv, wg = [w.astype(BF16) for w in jnp.split(w_in, cuts, axis=-1)]
    head_of = np.arange(ATTN_WIDTH) // HEAD_DIM
    pm = jnp.asarray((head_of[:, None] == head_of[None, :]) / HEAD_DIM, BF16)
    tile_heads = lambda g: jnp.tile(g.astype(F32), N_ATTN_HEADS).reshape(1, ATTN_WIDTH)

    u, (qs, ks, vs), gates = _inproj(x2, row(norm1_g), wglu, row(conv_b_glu), wq, wk, wv,
                                     tile_heads(q_norm_g), tile_heads(k_norm_g), pm, wg,
                                     tm=256, batch=batch, seq=seq)

    outs, lses = [], []
    for g, (_, dil) in enumerate(ATTN_GROUPS):
        o, lse = _attn_group(qs[g], ks[g], vs[g], _band_bias(rel_bias, g, dil), dil)
        outs.append(o)
        lses.append(lse)

    h2 = _mixer(u, outs, lses, gates, x2, conv_dw_w.reshape(CONV_WIDTH, CONV_CH).astype(F32),
                row(conv_dw_b), row(conv_ln_g), row(conv_ln_b), conv_w_proj.astype(BF16),
                row(conv_b_proj), attn_w_proj.astype(BF16), mix_w_out.astype(BF16),
                tm=min(512, seq), seq=seq)

    tt = min(512, n)
    hnt, scores = _peer_scores(h2, row(norm2_g), peer_w_q.T.astype(BF16),
                               peer_sub_keys.astype(BF16), tt)
    return _peer_main(hnt, scores, _pack_table(peer_u, transpose=False),
                      _pack_table(peer_v, transpose=True), h2, tt, chunk=1024)


def kernel(x, w_in, conv_b_glu, conv_dw_w, conv_dw_b, conv_ln_g, conv_ln_b, conv_w_proj, conv_b_proj, q_norm_g, k_norm_g, rel_bias, attn_w_proj, mix_w_out, norm1_g, norm2_g, peer_w_q, peer_sub_keys, peer_u, peer_v):
    batch, seq, d = x.shape
    x2 = x.reshape(batch * seq, d)
    for l in range(w_in.shape[0]):
        x2 = _layer(x2, batch, seq, w_in[l], conv_b_glu[l], conv_dw_w[l], conv_dw_b[l], conv_ln_g[l],
                    conv_ln_b[l], conv_w_proj[l], conv_b_proj[l], q_norm_g[l], k_norm_g[l], rel_bias,
                    attn_w_proj[l], mix_w_out[l], norm1_g[l], norm2_g[l], peer_w_q[l],
                    peer_sub_keys[l], peer_u[l], peer_v[l])
    return x2.reshape(batch, seq, d)
```

```python
import functools
import math

import numpy as np
import jax
import jax.numpy as jnp
from jax import lax
from jax.experimental import pallas as pl
from jax.experimental.pallas import tpu as pltpu

F32 = jnp.float32
BF16 = jnp.bfloat16

HEAD_DIM = 64
HEADS_PER_GROUP = 4
ATTN_GROUPS = ((128, 1), (512, 4), (2048, 16))
N_ATTN_HEADS = HEADS_PER_GROUP * len(ATTN_GROUPS)
ATTN_WIDTH = N_ATTN_HEADS * HEAD_DIM
GROUP_WIDTH = HEADS_PER_GROUP * HEAD_DIM
KEY_STEPS = 128
NEG_INF = -1e30
CONV_CH = 512
CONV_WIDTH = 31
CONV_HALO = 32
REL_BUCKETS = 32
REL_MAX_DISTANCE = 2048
PEER_HEADS = 8
PEER_N_KEYS = 128
PEER_TOPK = 16
EPS = 1e-6

LANES = 128
SUBLANES = 8
CONV_ROW_CHUNK = 32
MXU_COLS = 256
ROW_SPLIT = 2
KEYS_PER_PASS = 4
VMEM_LIMIT = 56 * 1024 * 1024

_NT = (((1,), (1,)), ((), ()))


def _params(n_axes):
    return pltpu.CompilerParams(dimension_semantics=("arbitrary",) * n_axes,
                                vmem_limit_bytes=VMEM_LIMIT)


def _full(shape):
    n = len(shape)
    return pl.BlockSpec(shape, lambda *_: (0,) * n)


def _inproj_kernel(x_ref, g1_ref, wglu_ref, bglu_ref, wq_ref, wk_ref, wv_ref, qg_ref, kg_ref,
                   pm_ref, wg_ref, u_ref, *rest, tm):
    qkv_refs, gate_ref, stage = rest[:9], rest[9], rest[10]
    x = x_ref[...]
    ms = jnp.mean(x * x, axis=-1, keepdims=True)
    xn = (x * lax.rsqrt(ms + EPS) * g1_ref[...]).astype(BF16)

    glu = jnp.dot(xn, wglu_ref[...], preferred_element_type=F32) + bglu_ref[...]
    u_ref[...] = glu[:, :CONV_CH] * jax.nn.sigmoid(glu[:, CONV_CH:])

    def head_rmsnorm(w_ref, g_ref):
        y = jnp.dot(xn, w_ref[...], preferred_element_type=F32)
        msq = jnp.dot((y * y).astype(BF16), pm_ref[...], preferred_element_type=F32)
        return y * lax.rsqrt(msq + EPS) * g_ref[...]

    tensors = (head_rmsnorm(wq_ref, qg_ref) * (HEAD_DIM ** -0.5),
               head_rmsnorm(wk_ref, kg_ref),
               jnp.dot(xn, wv_ref[...], preferred_element_type=F32))
    for t, y in enumerate(tensors):
        for g, (_, dil) in enumerate(ATTN_GROUPS):
            out_ref = qkv_refs[t * len(ATTN_GROUPS) + g]
            yg = y[:, g * GROUP_WIDTH:(g + 1) * GROUP_WIDTH]
            if dil == 1:
                out_ref[0] = yg.astype(BF16)
                continue
            for s in range(GROUP_WIDTH // LANES):
                stage[s] = yg[:, s * LANES:(s + 1) * LANES]
            for r in range(dil):
                for s in range(GROUP_WIDTH // LANES):
                    out_ref[r, :, s * LANES:(s + 1) * LANES] = (
                        stage[s, pl.ds(r, tm // dil, stride=dil), :].astype(BF16))
    gate_ref[...] = jax.nn.sigmoid(
        jnp.dot(xn, wg_ref[...], preferred_element_type=F32)).astype(BF16)


def _inproj(x2, g1, wglu, bglu, wq, wk, wv, qg, kg, pm, wg, tm, batch, seq):
    n, d = x2.shape
    tiles_per_seq = seq // tm
    row = lambda w: pl.BlockSpec((tm, w), lambda i: (i, 0))
    grouped = lambda dil: pl.BlockSpec((None, dil, tm // dil, GROUP_WIDTH),
                                       lambda i: (i // tiles_per_seq, 0, i % tiles_per_seq, 0))
    grouped_shape = lambda dil: jax.ShapeDtypeStruct((batch, dil, seq // dil, GROUP_WIDTH), BF16)
    dils = [dil for _ in range(3) for _, dil in ATTN_GROUPS]
    outs = pl.pallas_call(
        functools.partial(_inproj_kernel, tm=tm),
        grid=(n // tm,),
        in_specs=[row(d), _full(g1.shape), _full(wglu.shape), _full(bglu.shape), _full(wq.shape),
                  _full(wk.shape), _full(wv.shape), _full(qg.shape), _full(kg.shape),
                  _full(pm.shape), _full(wg.shape)],
        out_specs=[row(CONV_CH)] + [grouped(dil) for dil in dils] + [row(2 * d)],
        out_shape=[jax.ShapeDtypeStruct((n, CONV_CH), F32)] + [grouped_shape(dil) for dil in dils]
                  + [jax.ShapeDtypeStruct((n, 2 * d), BF16)],
        scratch_shapes=[pltpu.VMEM((GROUP_WIDTH // LANES, tm, LANES), F32)],
        compiler_params=_params(1),
        name="inproj",
    )(x2, g1, wglu, bglu, wq, wk, wv, qg, kg, pm, wg)
    qkv = [list(outs[1 + 3 * t:4 + 3 * t]) for t in range(3)]
    return outs[0], qkv, outs[10]


def _attn_kernel(q_ref, kc_ref, kp_ref, vc_ref, vp_ref, bias_ref, o_ref, lse_ref, kbuf, vbuf,
                 *, nsub):
    pair = pl.program_id(2)
    first_blk = (pl.program_id(3) == 0).astype(jnp.int32)
    kbuf[0:KEY_STEPS, :] = kp_ref[...]
    kbuf[KEY_STEPS:, :] = kc_ref[...]
    vbuf[0:KEY_STEPS, :] = vp_ref[...]
    vbuf[KEY_STEPS:, :] = vc_ref[...]
    lane = lax.broadcasted_iota(jnp.int32, (1, LANES), 1)
    lo = lane < HEAD_DIM
    head_mask = (lo.astype(BF16), (~lo).astype(BF16))
    for sub in range(nsub):
        r0 = sub * KEY_STEPS
        qs = q_ref[r0:r0 + KEY_STEPS, :]
        kw = kbuf[r0:r0 + 2 * KEY_STEPS, :]
        vw = vbuf[r0:r0 + 2 * KEY_STEPS, :]
        outs, lses = [], []
        for hh in range(2):
            s = lax.dot_general(qs * head_mask[hh], kw, _NT, preferred_element_type=F32)
            variant = first_blk if sub == 0 else 0
            s = s + bias_ref[variant, pair * 2 + hh]
            m = jnp.max(s, axis=1, keepdims=True)
            p = jnp.exp(s - m)
            l = jnp.sum(p, axis=1, keepdims=True)
            pv = jnp.dot(p.astype(BF16), vw, preferred_element_type=F32)
            outs.append(pv / l)
            lses.append(m + jnp.log(l))
        o_ref[r0:r0 + KEY_STEPS, :] = jnp.where(lo, outs[0], outs[1]).astype(BF16)
        lse_ref[r0:r0 + KEY_STEPS, :] = jnp.where(lo, lses[0], lses[1])


def _attn_group(q, k, v, bias, dil):
    batch, _, sub_len, _ = q.shape
    tq = min(512, sub_len)
    nsub = tq // KEY_STEPS
    cur = pl.BlockSpec((None, None, tq, LANES), lambda b, r, p, i: (b, r, i, p))
    prev = pl.BlockSpec((None, None, KEY_STEPS, LANES),
                        lambda b, r, p, i: (b, r, jnp.maximum(i * nsub - 1, 0), p))
    return pl.pallas_call(
        functools.partial(_attn_kernel, nsub=nsub),
        grid=(batch, dil, 2, sub_len // tq),
        in_specs=[cur, cur, prev, cur, prev, _full(bias.shape)],
        out_specs=[cur, cur],
        out_shape=[jax.ShapeDtypeStruct(q.shape, BF16), jax.ShapeDtypeStruct(q.shape, F32)],
        scratch_shapes=[pltpu.VMEM((tq + KEY_STEPS, LANES), BF16),
                        pltpu.VMEM((tq + KEY_STEPS, LANES), BF16)],
        compiler_params=_params(4),
        name=f"attn_dil{dil}",
    )(q, k, k, v, v, bias)


def _t5_causal_bucket(distance):
    n = distance.astype(jnp.int32)
    max_exact = REL_BUCKETS // 2
    nf = jnp.maximum(n, 1).astype(F32)
    large = max_exact + (jnp.log(nf / max_exact) / math.log(REL_MAX_DISTANCE / max_exact)
                         * (REL_BUCKETS - max_exact)).astype(jnp.int32)
    large = jnp.minimum(large, REL_BUCKETS - 1)
    return jnp.where(n < max_exact, n, large)


def _band_bias(rel_bias, g, dil):
    steps = jnp.arange(KEY_STEPS + 1)
    per_step = rel_bias[_t5_causal_bucket(steps * dil)][:, g * HEADS_PER_GROUP:(g + 1) * HEADS_PER_GROUP].T
    k = KEY_STEPS
    pad = lambda w: jnp.full((HEADS_PER_GROUP, w), NEG_INF, F32)
    f = jnp.concatenate([pad(k - 1), per_step[:, ::-1].astype(F32), pad(k)], axis=1)
    width = f.shape[1]
    normal = jnp.tile(f, (1, k))[:, :k * (width - 1)].reshape(HEADS_PER_GROUP, k, width - 1)
    normal = normal[:, :, k - 1:k - 1 + 2 * k]
    before_start = np.arange(2 * k)[None, None, :] < k
    at_start = jnp.where(before_start, NEG_INF, normal)
    return jnp.stack([normal, at_start])


def _mixer_kernel(u_ref, halo_ref, o0_ref, o1_ref, o2_ref, l0_ref, l1_ref, l2_ref, gate_ref, x_ref,
                  dw_ref, dwb_ref, lng_ref, lnb_ref, wc_ref, bc_ref, wa_ref, wo_ref,
                  h_ref, uext, order_ref, shift_ref, conv_ref, *, tm, tiles_per_seq):
    d = x_ref.shape[1]

    def token_order(ref, dil, slot):
        if dil == 1:
            return ref[0].astype(F32)
        for r in range(dil):
            blk = ref[r].astype(F32)
            for s in range(GROUP_WIDTH // LANES):
                order_ref[slot, s, pl.ds(r, tm // dil, stride=dil), :] = blk[:, s * LANES:(s + 1) * LANES]
        return jnp.concatenate([order_ref[slot, s] for s in range(GROUP_WIDTH // LANES)], axis=1)

    at_seq_start = pl.program_id(0) % tiles_per_seq == 0
    uext[0:CONV_HALO, :] = jnp.where(at_seq_start, 0.0, halo_ref[...])
    uext[CONV_HALO:, :] = u_ref[...]
    shift_rows = shift_ref.shape[1]
    for b in range(1, SUBLANES):
        shift_ref[b - 1] = uext[b:b + shift_rows, :]

    def conv_rows(rc, carry):
        base = pl.multiple_of(rc * CONV_ROW_CHUNK, CONV_ROW_CHUNK)
        acc = jnp.broadcast_to(dwb_ref[...], (CONV_ROW_CHUNK, CONV_CH))
        for w in range(CONV_WIDTH):
            a, b = divmod(CONV_HALO - (CONV_WIDTH - 1) + w, SUBLANES)
            rows = pl.ds(base + a * SUBLANES, CONV_ROW_CHUNK)
            tap = uext[rows, :] if b == 0 else shift_ref[b - 1, rows, :]
            acc = acc + tap * dw_ref[w:w + 1, :]
        conv_ref[pl.ds(base, CONV_ROW_CHUNK), :] = acc
        return carry

    lax.fori_loop(0, tm // CONV_ROW_CHUNK, conv_rows, 0)
    c = conv_ref[...]
    mu = jnp.mean(c, axis=-1, keepdims=True)
    cc = c - mu
    var = jnp.mean(cc * cc, axis=-1, keepdims=True)
    c = cc * lax.rsqrt(var + EPS) * lng_ref[...] + lnb_ref[...]
    c = c * jax.nn.sigmoid(c)
    conv_out = jnp.dot(c.astype(BF16), wc_ref[...], preferred_element_type=F32) + bc_ref[...]

    dils = [dil for _, dil in ATTN_GROUPS]
    lses = [token_order(ref, dil, 2 * g) for g, (ref, dil) in enumerate(zip((l0_ref, l1_ref, l2_ref), dils))]
    outs = [token_order(ref, dil, 2 * g + 1) for g, (ref, dil) in enumerate(zip((o0_ref, o1_ref, o2_ref), dils))]
    m = jnp.maximum(jnp.maximum(lses[0], lses[1]), lses[2])
    es = [jnp.exp(l - m) for l in lses]
    inv = 1.0 / (es[0] + es[1] + es[2])
    attn_out = jnp.zeros((tm, d), F32)
    for g in range(3):
        og = (outs[g] * (es[g] * inv)).astype(BF16)
        attn_out = attn_out + jnp.dot(og, wa_ref[g * GROUP_WIDTH:(g + 1) * GROUP_WIDTH, :],
                                      preferred_element_type=F32)

    merged = (gate_ref[:, :d].astype(F32) * conv_out + gate_ref[:, d:].astype(F32) * attn_out)
    h_ref[...] = x_ref[...] + jnp.dot(merged.astype(BF16), wo_ref[...], preferred_element_type=F32)


def _mixer(u, outs, lses, gates, x2, dw, dwb, lng, lnb, wc, bc, wa, wo, tm, seq):
    n, d = x2.shape
    row = lambda w: pl.BlockSpec((tm, w), lambda i: (i, 0))
    halo = pl.BlockSpec((CONV_HALO, CONV_CH),
                        lambda i: (jnp.maximum(i * (tm // CONV_HALO) - 1, 0), 0))
    tiles_per_seq = seq // tm
    grouped = lambda dil: pl.BlockSpec((None, dil, tm // dil, GROUP_WIDTH),
                                       lambda i: (i // tiles_per_seq, 0, i % tiles_per_seq, 0))
    group_specs = [grouped(dil) for _, dil in ATTN_GROUPS]
    return pl.pallas_call(
        functools.partial(_mixer_kernel, tm=tm, tiles_per_seq=tiles_per_seq),
        grid=(n // tm,),
        in_specs=[row(CONV_CH), halo] + group_specs * 2 + [row(2 * d), row(d)]
                 + [_full(a.shape) for a in (dw, dwb, lng, lnb, wc, bc, wa, wo)],
        out_specs=row(d),
        out_shape=jax.ShapeDtypeStruct((n, d), F32),
        scratch_shapes=[pltpu.VMEM((tm + CONV_HALO, CONV_CH), F32),
                        pltpu.VMEM((2 * len(ATTN_GROUPS), GROUP_WIDTH // LANES, tm, LANES), F32),
                        pltpu.VMEM((SUBLANES - 1, tm + CONV_HALO - SUBLANES, CONV_CH), F32),
                        pltpu.VMEM((tm, CONV_CH), F32)],
        compiler_params=_params(1),
        name="mixer_out",
    )(u, u, *outs, *lses, gates, x2, dw, dwb, lng, lnb, wc, bc, wa, wo)


_CAND_WIDTHS = tuple(PEER_TOPK // (r1 + 1) for r1 in range(PEER_TOPK))
_CAND_BLOCK_ROWS = 8
_UNRANKED = float(PEER_N_KEYS)


def _pack_rows(x):
    return pltpu.bitcast(x, jnp.uint32)


def _unpack_rows(x):
    return pltpu.bitcast(x, BF16)


def _col_reduce(x, op, reduce_fn):
    parts = [x[i:i + 8] for i in range(0, x.shape[0], 8)]
    while len(parts) > 1:
        nxt = [op(parts[i], parts[i + 1]) for i in range(0, len(parts) - 1, 2)]
        parts = nxt + ([parts[-1]] if len(parts) % 2 else [])
    return reduce_fn(parts[0], axis=0, keepdims=True)


def _col_max(x):
    return _col_reduce(x, jnp.maximum, jnp.max)


def _col_min(x):
    return _col_reduce(x, jnp.minimum, jnp.min)


def _col_sum(x):
    return _col_reduce(x, jnp.add, jnp.sum)


def _extract_top(s, key_id, on_round):
    for r in range(PEER_TOPK):
        m = _col_max(s)
        idx = _col_min(jnp.where(s == m, key_id, 1e9))
        sel = key_id == idx
        s = jnp.where(sel, -jnp.inf, s)
        on_round(r, m, idx, sel)


def _rank_keys(s, key_id, iota16):
    state = [jnp.zeros(iota16.shape, F32), jnp.full(s.shape, _UNRANKED, F32)]

    def on_round(r, m, idx, sel):
        state[0] = jnp.where(iota16 == float(r), m, state[0])
        state[1] = jnp.where(sel, float(r), state[1])

    _extract_top(s, key_id, on_round)
    return state[0], state[1]


_MARK_SCALE = 2.0 ** 100
_CAND_MARK = -_MARK_SCALE


def _top_values_unique(scores, iota16):
    scores = list(scores)
    tops = [jnp.zeros(iota16.shape, F32) for _ in scores]
    for r in range(PEER_TOPK):
        for i, s in enumerate(scores):
            m = _col_max(s)
            scores[i] = jnp.where(s == m, -_MARK_SCALE * (1.0 + r / 32.0), s)
            tops[i] = jnp.where(iota16 == float(r), m, tops[i])
    return scores, tops


def _decode_marks(s):
    marked = s < -0.5 * _MARK_SCALE
    rank = jnp.where(marked, (s * (-1.0 / _MARK_SCALE) - 1.0) * 32.0, _UNRANKED)
    return rank, _col_sum(marked.astype(F32))


def _candidate_sums(ss1, ss2, iota8):
    blocks = [ss1[0:1, :] + ss2]
    for r1 in range(1, _CAND_BLOCK_ROWS):
        blk = ss1[r1:r1 + 1, :] + ss2[0:_CAND_BLOCK_ROWS, :]
        blocks.append(jnp.where(iota8 < float(_CAND_WIDTHS[r1]), blk, -jnp.inf))
    blocks.append(ss1[_CAND_BLOCK_ROWS:, :] + ss2[0:1, :])
    return jnp.concatenate(blocks, axis=0)


def _selection_iotas():
    key_id = lax.broadcasted_iota(jnp.int32, (PEER_N_KEYS, LANES), 0).astype(F32)
    iota16 = lax.broadcasted_iota(jnp.int32, (PEER_TOPK, LANES), 0).astype(F32)
    iota8 = lax.broadcasted_iota(jnp.int32, (_CAND_BLOCK_ROWS, LANES), 0).astype(F32)
    cand_id = jnp.concatenate(
        [iota16]
        + [iota8 + float(r1 * PEER_TOPK) for r1 in range(1, _CAND_BLOCK_ROWS)]
        + [(iota8 + float(_CAND_BLOCK_ROWS)) * float(PEER_TOPK)], axis=0)
    return key_id, iota16, iota8, cand_id


def _select_unique(s1, s2):
    _, iota16, iota8, _ = _selection_iotas()
    k = PEER_TOPK
    (m1, m2), (ss1, ss2) = _top_values_unique((s1, s2), iota16)
    rk1, n1 = _decode_marks(m1)
    rk2, n2 = _decode_marks(m2)
    cand = _candidate_sums(ss1, ss2, iota8)
    best = jnp.zeros((k, LANES), F32)
    for r in range(k):
        m = _col_max(cand)
        cand = jnp.where(cand == m, _CAND_MARK, cand)
        best = jnp.where(iota16 == float(r), m, best)
    picked = (cand == _CAND_MARK).astype(F32)
    per_row = [jnp.sum(picked[0:k, :], axis=0, keepdims=True)]
    for r1 in range(1, _CAND_BLOCK_ROWS):
        lo = k + (r1 - 1) * _CAND_BLOCK_ROWS
        per_row.append(jnp.sum(picked[lo:lo + _CAND_BLOCK_ROWS, :], axis=0, keepdims=True))
    cnt16 = jnp.concatenate(per_row + [picked[k + 7 * _CAND_BLOCK_ROWS:, :]], axis=0)
    n3 = jnp.sum(cnt16, axis=0, keepdims=True)
    tie = jnp.abs(n1 - k) + jnp.abs(n2 - k) + jnp.abs(n3 - k)
    return ss1, ss2, best, cnt16, rk1, rk2, tie


def _select_exact(s1, s2):
    key_id, iota16, iota8, cand_id = _selection_iotas()
    k = PEER_TOPK
    ss1, rk1 = _rank_keys(s1, key_id, iota16)
    ss2, rk2 = _rank_keys(s2, key_id, iota16)
    state = [jnp.zeros((k, LANES), F32), jnp.zeros((k, LANES), F32)]

    def on_round(r, m, idx, sel):
        state[0] = jnp.where(iota16 == float(r), m, state[0])
        row = jnp.floor(idx * (1.0 / k))
        state[1] = state[1] + (iota16 == row).astype(F32)

    _extract_top(_candidate_sums(ss1, ss2, iota8), cand_id, on_round)
    return ss1, ss2, state[0], state[1], rk1, rk2


def _store_gate_state(sel, s1, s2, slot, hd, lanes, rk2_s, b_s, cnt_s, a_s):
    ss1, ss2, best, cnt16, rk1, rk2 = sel
    z = jnp.sum(jnp.exp(best - best[0:1, :]), axis=0, keepdims=True)
    a_s[slot, hd, :, lanes] = jnp.exp(s1 - ss1[0:1, :])
    b_s[slot, hd, :, lanes] = _pack_rows((jnp.exp(s2 - ss2[0:1, :]) / z).astype(BF16))
    rk2_s[slot, hd, :, lanes] = _pack_rows(rk2.astype(BF16))
    cnt = jnp.zeros((PEER_N_KEYS, LANES), F32)
    for r1 in range(PEER_TOPK):
        cnt = jnp.where(rk1 == float(r1), cnt16[r1:r1 + 1, :], cnt)
    cnt_s[slot, hd, :, lanes] = cnt


def _peer_scores_kernel(h_ref, g2_ref, wqt_ref, keys_ref, hnt_ref, sc_ref, qt_ref):
    h = h_ref[...]
    ms = jnp.mean(h * h, axis=-1, keepdims=True)
    hnt = (h * lax.rsqrt(ms + EPS) * g2_ref[...]).T.astype(BF16)
    hnt_ref[...] = _pack_rows(hnt)
    qt_ref[...] = jnp.dot(wqt_ref[...], hnt, preferred_element_type=F32)
    for hp in range(2 * PEER_HEADS):
        qhp = qt_ref[hp * PEER_N_KEYS:(hp + 1) * PEER_N_KEYS, :].astype(BF16)
        sc_ref[hp] = jnp.dot(keys_ref[hp // 2, hp % 2], qhp, preferred_element_type=F32)


def _peer_scores(h2, g2, wqt, keys, tt):
    n, d = h2.shape
    nt = n // tt
    return pl.pallas_call(
        _peer_scores_kernel,
        grid=(nt,),
        in_specs=[pl.BlockSpec((tt, d), lambda i: (i, 0)), _full(g2.shape), _full(wqt.shape),
                  _full(keys.shape)],
        out_specs=[pl.BlockSpec((d // 2, tt), lambda i: (0, i)),
                   pl.BlockSpec((None, 2 * PEER_HEADS, PEER_N_KEYS, tt), lambda i: (i, 0, 0, 0))],
        out_shape=[jax.ShapeDtypeStruct((d // 2, n), jnp.uint32),
                   jax.ShapeDtypeStruct((nt, 2 * PEER_HEADS, PEER_N_KEYS, tt), F32)],
        scratch_shapes=[pltpu.VMEM((wqt.shape[0], tt), F32)],
        compiler_params=_params(1),
        name="peer_scores",
    )(h2, g2, wqt, keys)


def _peer_main_kernel(hnt_ref, sc_ref, u_ref, vt_ref, res_ref, y_ref,
                      yt_ref, ht0_ref, ht1_ref, act0_ref, act1_ref, rows_ref,
                      rk2_s, b_s, cnt_s, a_s, tie_ref,
                      *, keys_per_chunk, tt, n_chunks, n_items, n_tiles):
    g = pl.program_id(0)
    m = g - n_chunks
    item3 = m - 2
    first_of_tile = (item3 >= 0) & (item3 % n_chunks == 0)
    last_of_tile = (item3 >= 0) & (item3 % n_chunks == n_chunks - 1)
    lane_groups = tt // LANES
    tasks_per_step = PEER_HEADS * lane_groups // n_chunks

    @pl.when(g == 0)
    def _():
        for ref in (ht0_ref, ht1_ref):
            ref[...] = jnp.zeros(ref.shape, F32)
        for ref in (act0_ref, act1_ref):
            ref[...] = jnp.zeros(ref.shape, jnp.uint32)
        tie_ref[...] = jnp.zeros(tie_ref.shape, F32)
        for ref in (rk2_s, b_s, cnt_s, a_s):
            ref[...] = jnp.zeros(ref.shape, ref.dtype)

    @pl.when((m == 0) | first_of_tile)
    def _():
        yt_ref[...] = jnp.zeros(yt_ref.shape, F32)

    sel_step = jnp.maximum(g - 1, 0)
    sel_slot = (sel_step // n_chunks) % 2
    first_task = (sel_step % n_chunks) * tasks_per_step

    def select_task(t):
        task = first_task + t
        hd = task // lane_groups
        lanes = pl.ds(pl.multiple_of((task % lane_groups) * LANES, LANES), LANES)
        s1 = sc_ref[2 * hd, :, lanes]
        s2 = sc_ref[2 * hd + 1, :, lanes]
        *sel, tie = _select_unique(s1, s2)
        _store_gate_state(sel, s1, s2, sel_slot, hd, lanes, rk2_s, b_s, cnt_s, a_s)
        tie_ref[...] = jnp.maximum(tie_ref[...], tie)

    half = PEER_N_KEYS // 2
    item2 = jnp.clip(m - 1, 0, n_items - 1)
    chunk2 = item2 % n_chunks
    slot2 = (item2 // n_chunks) % 2

    def gate_block(c, k0, ht_old, act_new):
        lanes = slice(c * LANES, (c + 1) * LANES)
        ks = range(k0, k0 + KEYS_PER_PASS)
        gates = {k: jnp.zeros((PEER_N_KEYS, LANES), BF16) for k in ks}
        for hd in range(PEER_HEADS):
            rk2 = _unpack_rows(rk2_s[slot2, hd, :, lanes])
            b = _unpack_rows(b_s[slot2, hd, :, lanes])
            for k in ks:
                cnt_row = rows_ref[k, hd:hd + 1, lanes].astype(BF16)
                a_row = rows_ref[k, PEER_HEADS + hd:PEER_HEADS + hd + 1, lanes].astype(BF16)
                gates[k] = gates[k] + jnp.where(rk2 < cnt_row, b * a_row, 0.0)
        for k in ks:
            x = ht_old[k * PEER_N_KEYS:(k + 1) * PEER_N_KEYS, lanes]
            gelu = 0.5 * x * (1.0 + lax.erf(x * math.sqrt(0.5)))
            act_new[k * half:(k + 1) * half, lanes] = _pack_rows(gelu.astype(BF16) * gates[k])

    def step(ht_new, ht_old, act_new, act_old):
        for k in range(keys_per_chunk):
            i1 = chunk2 * keys_per_chunk + k
            for hd in range(PEER_HEADS):
                rows_ref[k, hd:hd + 1, :] = cnt_s[slot2, hd, pl.ds(i1, 1), :]
                rows_ref[k, PEER_HEADS + hd:PEER_HEADS + hd + 1, :] = a_s[slot2, hd, pl.ds(i1, 1), :]

        def u_piece(piece, q):
            cols = slice(piece * MXU_COLS, (piece + 1) * MXU_COLS)
            rp = u_ref.shape[0] // ROW_SPLIT
            ht_new[2 * q * rp:2 * (q + 1) * rp, cols] = jnp.dot(
                _unpack_rows(u_ref[q * rp:(q + 1) * rp, :]), _unpack_rows(hnt_ref[:, cols]),
                preferred_element_type=F32)

        def v_piece(piece, q):
            cols = slice(piece * MXU_COLS, (piece + 1) * MXU_COLS)
            rp = vt_ref.shape[0] // ROW_SPLIT
            yt_ref[2 * q * rp:2 * (q + 1) * rp, cols] += jnp.dot(
                _unpack_rows(vt_ref[q * rp:(q + 1) * rp, :]), _unpack_rows(act_old[:, cols]),
                preferred_element_type=F32)

        mxu_work = [functools.partial(f, p, q) for p in range(tt // MXU_COLS)
                    for f in (u_piece, v_piece) for q in range(ROW_SPLIT)]
        gates = [functools.partial(gate_block, c, k0, ht_old, act_new)
                 for c in range(tt // LANES) for k0 in range(0, keys_per_chunk, KEYS_PER_PASS)]
        selects = [functools.partial(select_task, t) for t in range(tasks_per_step)]
        every = len(gates) // len(selects)
        vpu_work = []
        for i, gate in enumerate(gates):
            vpu_work.append(gate)
            if i % every == every - 1:
                vpu_work.append(selects[i // every])
        done_m = done_v = 0
        while done_m < len(mxu_work) or done_v < len(vpu_work):
            behind = done_m * len(vpu_work) < done_v * len(mxu_work)
            if (behind and done_m < len(mxu_work)) or done_v == len(vpu_work):
                mxu_work[done_m]()
                done_m += 1
            else:
                vpu_work[done_v]()
                done_v += 1

    @pl.when((m >= 0) & (m % 2 == 0))
    def _():
        step(ht0_ref, ht1_ref, act1_ref, act0_ref)

    @pl.when((m >= 0) & (m % 2 == 1))
    def _():
        step(ht1_ref, ht0_ref, act0_ref, act1_ref)

    @pl.when(m < 0)
    def _():
        for t in range(tasks_per_step):
            select_task(t)

    @pl.when((g >= 1) & (sel_step % n_chunks == n_chunks - 1))
    def _():
        @pl.when(jnp.max(tie_ref[...]) > 0.5)
        def _():
            def redo(task, carry):
                hd = task // lane_groups
                lanes = pl.ds(pl.multiple_of((task % lane_groups) * LANES, LANES), LANES)
                s1 = sc_ref[2 * hd, :, lanes]
                s2 = sc_ref[2 * hd + 1, :, lanes]
                _store_gate_state(_select_exact(s1, s2), s1, s2, sel_slot, hd, lanes,
                                  rk2_s, b_s, cnt_s, a_s)
                return carry

            lax.fori_loop(0, PEER_HEADS * lane_groups, redo, 0)

        tie_ref[...] = jnp.zeros(tie_ref.shape, F32)

    @pl.when(last_of_tile)
    def _():
        y_ref[...] = res_ref[...] + yt_ref[...].T


def _peer_main(hnt, scores, u_packed, vt_packed, h2, tt, chunk):
    n, d = h2.shape
    n_tiles = n // tt
    n_chunks = 2 * u_packed.shape[0] // chunk
    n_items = n_tiles * n_chunks
    kpc = chunk // PEER_N_KEYS
    assert kpc % KEYS_PER_PASS == 0 and chunk % (2 * ROW_SPLIT) == 0 and d % (2 * ROW_SPLIT) == 0
    assert (PEER_HEADS * (tt // LANES)) % n_chunks == 0
    item = lambda g, lag: jnp.clip(g - n_chunks - lag, 0, n_items - 1)
    tile = lambda g, lag: item(g, lag) // n_chunks
    chunk_of = lambda g, lag: item(g, lag) % n_chunks
    sel_tile = lambda g: jnp.minimum(jnp.maximum(g - 1, 0) // n_chunks, n_tiles - 1)
    half = PEER_N_KEYS // 2
    state = lambda rows, dt: pltpu.VMEM((2, PEER_HEADS, rows, tt), dt)
    return pl.pallas_call(
        functools.partial(_peer_main_kernel, keys_per_chunk=kpc, tt=tt, n_chunks=n_chunks,
                          n_items=n_items, n_tiles=n_tiles),
        grid=(n_chunks + n_items + 2,),
        in_specs=[pl.BlockSpec((d // 2, tt), lambda g: (0, tile(g, 0))),
                  pl.BlockSpec((None,) + scores.shape[1:], lambda g: (sel_tile(g), 0, 0, 0)),
                  pl.BlockSpec((chunk // 2, d), lambda g: (chunk_of(g, 0), 0)),
                  pl.BlockSpec((d // 2, chunk), lambda g: (0, chunk_of(g, 2))),
                  pl.BlockSpec((tt, d), lambda g: (tile(g, 2), 0))],
        out_specs=pl.BlockSpec((tt, d), lambda g: (tile(g, 2), 0)),
        out_shape=jax.ShapeDtypeStruct((n, d), F32),
        scratch_shapes=[pltpu.VMEM((d, tt), F32),
                        pltpu.VMEM((chunk, tt), F32),
                        pltpu.VMEM((chunk, tt), F32),
                        pltpu.VMEM((chunk // 2, tt), jnp.uint32),
                        pltpu.VMEM((chunk // 2, tt), jnp.uint32),
                        pltpu.VMEM((kpc, 2 * PEER_HEADS, tt), F32),
                        state(half, jnp.uint32),
                        state(half, jnp.uint32),
                        state(PEER_N_KEYS, F32),
                        state(PEER_N_KEYS, F32),
                        pltpu.VMEM((1, LANES), F32)],
        compiler_params=_params(1),
        name="peer_main",
    )(hnt, scores, u_packed, vt_packed, h2)


def _pack_table_kernel(w_ref, o_ref, *, transpose):
    w = w_ref[...]
    o_ref[...] = _pack_rows((w.T if transpose else w).astype(BF16))


def _pack_table(w, transpose, rows=512):
    r, c = w.shape
    if transpose:
        out_shape, out_spec = (c // 2, r), pl.BlockSpec((c // 2, rows), lambda i: (0, i))
    else:
        out_shape, out_spec = (r // 2, c), pl.BlockSpec((rows // 2, c), lambda i: (i, 0))
    return pl.pallas_call(
        functools.partial(_pack_table_kernel, transpose=transpose),
        grid=(r // rows,),
        in_specs=[pl.BlockSpec((rows, c), lambda i: (i, 0))],
        out_specs=out_spec,
        out_shape=jax.ShapeDtypeStruct(out_shape, jnp.uint32),
        compiler_params=_params(1),
        name="pack_vt" if transpose else "pack_u",
    )(w)


def _layer(x2, batch, seq, w_in, conv_b_glu, conv_dw_w, conv_dw_b, conv_ln_g, conv_ln_b, conv_w_proj,
           conv_b_proj, q_norm_g, k_norm_g, rel_bias, attn_w_proj, mix_w_out, norm1_g, norm2_g,
           peer_w_q, peer_sub_keys, peer_u, peer_v):
    n, d = x2.shape
    row = lambda v: v.reshape(1, -1).astype(F32)
    c0 = 2 * CONV_CH
    cuts = [c0, c0 + ATTN_WIDTH, c0 + 2 * ATTN_WIDTH, c0 + 3 * ATTN_WIDTH]
    wglu, wq, wk, wv, wg = [w.astype(BF16) for w in jnp.split(w_in, cuts, axis=-1)]
    head_of = np.arange(ATTN_WIDTH) // HEAD_DIM
    pm = jnp.asarray((head_of[:, None] == head_of[None, :]) / HEAD_DIM, BF16)
    tile_heads = lambda g: jnp.tile(g.astype(F32), N_ATTN_HEADS).reshape(1, ATTN_WIDTH)

    u, (qs, ks, vs), gates = _inproj(x2, row(norm1_g), wglu, row(conv_b_glu), wq, wk, wv,
                                     tile_heads(q_norm_g), tile_heads(k_norm_g), pm, wg,
                                     tm=256, batch=batch, seq=seq)

    outs, lses = [], []
    for g, (_, dil) in enumerate(ATTN_GROUPS):
        o, lse = _attn_group(qs[g], ks[g], vs[g], _band_bias(rel_bias, g, dil), dil)
        outs.append(o)
        lses.append(lse)

    h2 = _mixer(u, outs, lses, gates, x2, conv_dw_w.reshape(CONV_WIDTH, CONV_CH).astype(F32),
                row(conv_dw_b), row(conv_ln_g), row(conv_ln_b), conv_w_proj.astype(BF16),
                row(conv_b_proj), attn_w_proj.astype(BF16), mix_w_out.astype(BF16),
                tm=min(512, seq), seq=seq)

    tt = min(512, n)
    hnt, scores = _peer_scores(h2, row(norm2_g), peer_w_q.T.astype(BF16),
                               peer_sub_keys.astype(BF16), tt)
    return _peer_main(hnt, scores, _pack_table(peer_u, transpose=False),
                      _pack_table(peer_v, transpose=True), h2, tt, chunk=512)


def kernel(x, w_in, conv_b_glu, conv_dw_w, conv_dw_b, conv_ln_g, conv_ln_b, conv_w_proj, conv_b_proj, q_norm_g, k_norm_g, rel_bias, attn_w_proj, mix_w_out, norm1_g, norm2_g, peer_w_q, peer_sub_keys, peer_u, peer_v):
    batch, seq, d = x.shape
    x2 = x.reshape(batch * seq, d)
    for l in range(w_in.shape[0]):
        x2 = _layer(x2, batch, seq, w_in[l], conv_b_glu[l], conv_dw_w[l], conv_dw_b[l], conv_ln_g[l],
                    conv_ln_b[l], conv_w_proj[l], conv_b_proj[l], q_norm_g[l], k_norm_g[l], rel_bias,
                    attn_w_proj[l], mix_w_out[l], norm1_g[l], norm2_g[l], peer_w_q[l],
                    peer_sub_keys[l], peer_u[l], peer_v[l])
    return x2.reshape(batch, seq, d)
```

```python
import functools
import math

import numpy as np
import jax
import jax.numpy as jnp
from jax import lax
from jax.experimental import pallas as pl
from jax.experimental.pallas import tpu as pltpu

F32 = jnp.float32
BF16 = jnp.bfloat16

HEAD_DIM = 64
HEADS_PER_GROUP = 4
ATTN_GROUPS = ((128, 1), (512, 4), (2048, 16))
N_ATTN_HEADS = HEADS_PER_GROUP * len(ATTN_GROUPS)
ATTN_WIDTH = N_ATTN_HEADS * HEAD_DIM
GROUP_WIDTH = HEADS_PER_GROUP * HEAD_DIM
KEY_STEPS = 128
NEG_INF = -1e30
CONV_CH = 512
CONV_WIDTH = 31
CONV_HALO = 32
REL_BUCKETS = 32
REL_MAX_DISTANCE = 2048
PEER_HEADS = 8
PEER_N_KEYS = 128
PEER_TOPK = 16
EPS = 1e-6

LANES = 128
SUBLANES = 8
CONV_ROW_CHUNK = 32
MXU_COLS = 256
ROW_SPLIT = 4
KEYS_PER_PASS = 4
VMEM_LIMIT = 56 * 1024 * 1024

_NT = (((1,), (1,)), ((), ()))


def _params(n_axes):
    return pltpu.CompilerParams(dimension_semantics=("arbitrary",) * n_axes,
                                vmem_limit_bytes=VMEM_LIMIT)


def _full(shape):
    n = len(shape)
    return pl.BlockSpec(shape, lambda *_: (0,) * n)


def _inproj_kernel(x_ref, g1_ref, wglu_ref, bglu_ref, wq_ref, wk_ref, wv_ref, qg_ref, kg_ref,
                   pm_ref, wg_ref, u_ref, *rest, tm):
    qkv_refs, gate_ref, stage = rest[:9], rest[9], rest[10]
    x = x_ref[...]
    ms = jnp.mean(x * x, axis=-1, keepdims=True)
    xn = (x * lax.rsqrt(ms + EPS) * g1_ref[...]).astype(BF16)

    glu = jnp.dot(xn, wglu_ref[...], preferred_element_type=F32) + bglu_ref[...]
    u_ref[...] = glu[:, :CONV_CH] * jax.nn.sigmoid(glu[:, CONV_CH:])

    def head_rmsnorm(w_ref, g_ref):
        y = jnp.dot(xn, w_ref[...], preferred_element_type=F32)
        msq = jnp.dot((y * y).astype(BF16), pm_ref[...], preferred_element_type=F32)
        return y * lax.rsqrt(msq + EPS) * g_ref[...]

    tensors = (head_rmsnorm(wq_ref, qg_ref) * (HEAD_DIM ** -0.5),
               head_rmsnorm(wk_ref, kg_ref),
               jnp.dot(xn, wv_ref[...], preferred_element_type=F32))
    for t, y in enumerate(tensors):
        for g, (_, dil) in enumerate(ATTN_GROUPS):
            out_ref = qkv_refs[t * len(ATTN_GROUPS) + g]
            yg = y[:, g * GROUP_WIDTH:(g + 1) * GROUP_WIDTH]
            if dil == 1:
                out_ref[0] = yg.astype(BF16)
                continue
            for s in range(GROUP_WIDTH // LANES):
                stage[s] = yg[:, s * LANES:(s + 1) * LANES]
            for r in range(dil):
                for s in range(GROUP_WIDTH // LANES):
                    out_ref[r, :, s * LANES:(s + 1) * LANES] = (
                        stage[s, pl.ds(r, tm // dil, stride=dil), :].astype(BF16))
    gate_ref[...] = jax.nn.sigmoid(
        jnp.dot(xn, wg_ref[...], preferred_element_type=F32)).astype(BF16)


def _inproj(x2, g1, wglu, bglu, wq, wk, wv, qg, kg, pm, wg, tm, batch, seq):
    n, d = x2.shape
    tiles_per_seq = seq // tm
    row = lambda w: pl.BlockSpec((tm, w), lambda i: (i, 0))
    grouped = lambda dil: pl.BlockSpec((None, dil, tm // dil, GROUP_WIDTH),
                                       lambda i: (i // tiles_per_seq, 0, i % tiles_per_seq, 0))
    grouped_shape = lambda dil: jax.ShapeDtypeStruct((batch, dil, seq // dil, GROUP_WIDTH), BF16)
    dils = [dil for _ in range(3) for _, dil in ATTN_GROUPS]
    outs = pl.pallas_call(
        functools.partial(_inproj_kernel, tm=tm),
        grid=(n // tm,),
        in_specs=[row(d), _full(g1.shape), _full(wglu.shape), _full(bglu.shape), _full(wq.shape),
                  _full(wk.shape), _full(wv.shape), _full(qg.shape), _full(kg.shape),
                  _full(pm.shape), _full(wg.shape)],
        out_specs=[row(CONV_CH)] + [grouped(dil) for dil in dils] + [row(2 * d)],
        out_shape=[jax.ShapeDtypeStruct((n, CONV_CH), F32)] + [grouped_shape(dil) for dil in dils]
                  + [jax.ShapeDtypeStruct((n, 2 * d), BF16)],
        scratch_shapes=[pltpu.VMEM((GROUP_WIDTH // LANES, tm, LANES), F32)],
        compiler_params=_params(1),
        name="inproj",
    )(x2, g1, wglu, bglu, wq, wk, wv, qg, kg, pm, wg)
    qkv = [list(outs[1 + 3 * t:4 + 3 * t]) for t in range(3)]
    return outs[0], qkv, outs[10]


def _attn_kernel(q_ref, kc_ref, kp_ref, vc_ref, vp_ref, bias_ref, o_ref, lse_ref, kbuf, vbuf,
                 *, nsub):
    first_blk = (pl.program_id(2) == 0).astype(jnp.int32)
    kbuf[0:KEY_STEPS, :] = kp_ref[...]
    kbuf[KEY_STEPS:, :] = kc_ref[...]
    vbuf[0:KEY_STEPS, :] = vp_ref[...]
    vbuf[KEY_STEPS:, :] = vc_ref[...]
    lane = lax.broadcasted_iota(jnp.int32, (1, LANES), 1)
    lo = lane < HEAD_DIM
    head_mask = (lo.astype(BF16), (~lo).astype(BF16))
    for pair in range(GROUP_WIDTH // LANES):
        cols = slice(pair * LANES, (pair + 1) * LANES)
        for sub in range(nsub):
            r0 = sub * KEY_STEPS
            qs = q_ref[r0:r0 + KEY_STEPS, cols]
            kw = kbuf[r0:r0 + 2 * KEY_STEPS, cols]
            vw = vbuf[r0:r0 + 2 * KEY_STEPS, cols]
            outs, lses = [], []
            for hh in range(2):
                s = lax.dot_general(qs * head_mask[hh], kw, _NT, preferred_element_type=F32)
                variant = first_blk if sub == 0 else 0
                s = s + bias_ref[variant, pair * 2 + hh]
                m = jnp.max(s, axis=1, keepdims=True)
                p = jnp.exp(s - m)
                l = jnp.sum(p, axis=1, keepdims=True)
                pv = jnp.dot(p.astype(BF16), vw, preferred_element_type=F32)
                outs.append(pv / l)
                lses.append(m + jnp.log(l))
            o_ref[r0:r0 + KEY_STEPS, cols] = jnp.where(lo, outs[0], outs[1]).astype(BF16)
            lse_ref[r0:r0 + KEY_STEPS, cols] = jnp.where(lo, lses[0], lses[1])


def _attn_group(q, k, v, bias, dil):
    batch, _, sub_len, _ = q.shape
    tq = min(512, sub_len)
    nsub = tq // KEY_STEPS
    cur = pl.BlockSpec((None, None, tq, GROUP_WIDTH), lambda b, r, i: (b, r, i, 0))
    prev = pl.BlockSpec((None, None, KEY_STEPS, GROUP_WIDTH),
                        lambda b, r, i: (b, r, jnp.maximum(i * nsub - 1, 0), 0))
    return pl.pallas_call(
        functools.partial(_attn_kernel, nsub=nsub),
        grid=(batch, dil, sub_len // tq),
        in_specs=[cur, cur, prev, cur, prev, _full(bias.shape)],
        out_specs=[cur, cur],
        out_shape=[jax.ShapeDtypeStruct(q.shape, BF16), jax.ShapeDtypeStruct(q.shape, F32)],
        scratch_shapes=[pltpu.VMEM((tq + KEY_STEPS, GROUP_WIDTH), BF16),
                        pltpu.VMEM((tq + KEY_STEPS, GROUP_WIDTH), BF16)],
        compiler_params=_params(3),
        name=f"attn_dil{dil}",
    )(q, k, k, v, v, bias)


def _t5_causal_bucket(distance):
    n = distance.astype(jnp.int32)
    max_exact = REL_BUCKETS // 2
    nf = jnp.maximum(n, 1).astype(F32)
    large = max_exact + (jnp.log(nf / max_exact) / math.log(REL_MAX_DISTANCE / max_exact)
                         * (REL_BUCKETS - max_exact)).astype(jnp.int32)
    large = jnp.minimum(large, REL_BUCKETS - 1)
    return jnp.where(n < max_exact, n, large)


def _band_bias(rel_bias, g, dil):
    steps = jnp.arange(KEY_STEPS + 1)
    per_step = rel_bias[_t5_causal_bucket(steps * dil)][:, g * HEADS_PER_GROUP:(g + 1) * HEADS_PER_GROUP].T
    k = KEY_STEPS
    pad = lambda w: jnp.full((HEADS_PER_GROUP, w), NEG_INF, F32)
    f = jnp.concatenate([pad(k - 1), per_step[:, ::-1].astype(F32), pad(k)], axis=1)
    width = f.shape[1]
    normal = jnp.tile(f, (1, k))[:, :k * (width - 1)].reshape(HEADS_PER_GROUP, k, width - 1)
    normal = normal[:, :, k - 1:k - 1 + 2 * k]
    before_start = np.arange(2 * k)[None, None, :] < k
    at_start = jnp.where(before_start, NEG_INF, normal)
    return jnp.stack([normal, at_start])


def _mixer_kernel(u_ref, halo_ref, o0_ref, o1_ref, o2_ref, l0_ref, l1_ref, l2_ref, gate_ref, x_ref,
                  dw_ref, dwb_ref, lng_ref, lnb_ref, wc_ref, bc_ref, wa_ref, wo_ref,
                  h_ref, uext, order_ref, shift_ref, conv_ref, *, tm, tiles_per_seq):
    d = x_ref.shape[1]

    def token_order(ref, dil, slot):
        if dil == 1:
            return ref[0].astype(F32)
        for r in range(dil):
            blk = ref[r].astype(F32)
            for s in range(GROUP_WIDTH // LANES):
                order_ref[slot, s, pl.ds(r, tm // dil, stride=dil), :] = blk[:, s * LANES:(s + 1) * LANES]
        return jnp.concatenate([order_ref[slot, s] for s in range(GROUP_WIDTH // LANES)], axis=1)

    at_seq_start = pl.program_id(0) % tiles_per_seq == 0
    uext[0:CONV_HALO, :] = jnp.where(at_seq_start, 0.0, halo_ref[...])
    uext[CONV_HALO:, :] = u_ref[...]
    shift_rows = shift_ref.shape[1]
    for b in range(1, SUBLANES):
        shift_ref[b - 1] = uext[b:b + shift_rows, :]

    def conv_rows(rc, carry):
        base = pl.multiple_of(rc * CONV_ROW_CHUNK, CONV_ROW_CHUNK)
        acc = jnp.broadcast_to(dwb_ref[...], (CONV_ROW_CHUNK, CONV_CH))
        for w in range(CONV_WIDTH):
            a, b = divmod(CONV_HALO - (CONV_WIDTH - 1) + w, SUBLANES)
            rows = pl.ds(base + a * SUBLANES, CONV_ROW_CHUNK)
            tap = uext[rows, :] if b == 0 else shift_ref[b - 1, rows, :]
            acc = acc + tap * dw_ref[w:w + 1, :]
        conv_ref[pl.ds(base, CONV_ROW_CHUNK), :] = acc
        return carry

    lax.fori_loop(0, tm // CONV_ROW_CHUNK, conv_rows, 0)
    c = conv_ref[...]
    mu = jnp.mean(c, axis=-1, keepdims=True)
    cc = c - mu
    var = jnp.mean(cc * cc, axis=-1, keepdims=True)
    c = cc * lax.rsqrt(var + EPS) * lng_ref[...] + lnb_ref[...]
    c = c * jax.nn.sigmoid(c)
    conv_out = jnp.dot(c.astype(BF16), wc_ref[...], preferred_element_type=F32) + bc_ref[...]

    dils = [dil for _, dil in ATTN_GROUPS]
    lses = [token_order(ref, dil, 2 * g) for g, (ref, dil) in enumerate(zip((l0_ref, l1_ref, l2_ref), dils))]
    outs = [token_order(ref, dil, 2 * g + 1) for g, (ref, dil) in enumerate(zip((o0_ref, o1_ref, o2_ref), dils))]
    m = jnp.maximum(jnp.maximum(lses[0], lses[1]), lses[2])
    es = [jnp.exp(l - m) for l in lses]
    inv = 1.0 / (es[0] + es[1] + es[2])
    attn_out = jnp.zeros((tm, d), F32)
    for g in range(3):
        og = (outs[g] * (es[g] * inv)).astype(BF16)
        attn_out = attn_out + jnp.dot(og, wa_ref[g * GROUP_WIDTH:(g + 1) * GROUP_WIDTH, :],
                                      preferred_element_type=F32)

    merged = (gate_ref[:, :d].astype(F32) * conv_out + gate_ref[:, d:].astype(F32) * attn_out)
    h_ref[...] = x_ref[...] + jnp.dot(merged.astype(BF16), wo_ref[...], preferred_element_type=F32)


def _mixer(u, outs, lses, gates, x2, dw, dwb, lng, lnb, wc, bc, wa, wo, tm, seq):
    n, d = x2.shape
    row = lambda w: pl.BlockSpec((tm, w), lambda i: (i, 0))
    halo = pl.BlockSpec((CONV_HALO, CONV_CH),
                        lambda i: (jnp.maximum(i * (tm // CONV_HALO) - 1, 0), 0))
    tiles_per_seq = seq // tm
    grouped = lambda dil: pl.BlockSpec((None, dil, tm // dil, GROUP_WIDTH),
                                       lambda i: (i // tiles_per_seq, 0, i % tiles_per_seq, 0))
    group_specs = [grouped(dil) for _, dil in ATTN_GROUPS]
    return pl.pallas_call(
        functools.partial(_mixer_kernel, tm=tm, tiles_per_seq=tiles_per_seq),
        grid=(n // tm,),
        in_specs=[row(CONV_CH), halo] + group_specs * 2 + [row(2 * d), row(d)]
                 + [_full(a.shape) for a in (dw, dwb, lng, lnb, wc, bc, wa, wo)],
        out_specs=row(d),
        out_shape=jax.ShapeDtypeStruct((n, d), F32),
        scratch_shapes=[pltpu.VMEM((tm + CONV_HALO, CONV_CH), F32),
                        pltpu.VMEM((2 * len(ATTN_GROUPS), GROUP_WIDTH // LANES, tm, LANES), F32),
                        pltpu.VMEM((SUBLANES - 1, tm + CONV_HALO - SUBLANES, CONV_CH), F32),
                        pltpu.VMEM((tm, CONV_CH), F32)],
        compiler_params=_params(1),
        name="mixer_out",
    )(u, u, *outs, *lses, gates, x2, dw, dwb, lng, lnb, wc, bc, wa, wo)


_CAND_WIDTHS = tuple(PEER_TOPK // (r1 + 1) for r1 in range(PEER_TOPK))
_CAND_BLOCK_ROWS = 8
_UNRANKED = float(PEER_N_KEYS)


def _pack_rows(x):
    return pltpu.bitcast(x, jnp.uint32)


def _unpack_rows(x):
    return pltpu.bitcast(x, BF16)


def _col_reduce(x, op, reduce_fn):
    parts = [x[i:i + 8] for i in range(0, x.shape[0], 8)]
    while len(parts) > 1:
        nxt = [op(parts[i], parts[i + 1]) for i in range(0, len(parts) - 1, 2)]
        parts = nxt + ([parts[-1]] if len(parts) % 2 else [])
    return reduce_fn(parts[0], axis=0, keepdims=True)


def _col_max(x):
    return _col_reduce(x, jnp.maximum, jnp.max)


def _col_min(x):
    return _col_reduce(x, jnp.minimum, jnp.min)


def _col_sum(x):
    return _col_reduce(x, jnp.add, jnp.sum)


def _extract_top(s, key_id, on_round):
    for r in range(PEER_TOPK):
        m = _col_max(s)
        idx = _col_min(jnp.where(s == m, key_id, 1e9))
        sel = key_id == idx
        s = jnp.where(sel, -jnp.inf, s)
        on_round(r, m, idx, sel)


def _rank_keys(s, key_id, iota16):
    state = [jnp.zeros(iota16.shape, F32), jnp.full(s.shape, _UNRANKED, F32)]

    def on_round(r, m, idx, sel):
        state[0] = jnp.where(iota16 == float(r), m, state[0])
        state[1] = jnp.where(sel, float(r), state[1])

    _extract_top(s, key_id, on_round)
    return state[0], state[1]


_MARK_SCALE = 2.0 ** 100
_CAND_MARK = -_MARK_SCALE


def _top_values_unique(scores, iota16):
    scores = list(scores)
    tops = [jnp.zeros(iota16.shape, F32) for _ in scores]
    for r in range(PEER_TOPK):
        for i, s in enumerate(scores):
            m = _col_max(s)
            scores[i] = jnp.where(s == m, -_MARK_SCALE * (1.0 + r / 32.0), s)
            tops[i] = jnp.where(iota16 == float(r), m, tops[i])
    return scores, tops


def _decode_marks(s):
    marked = s < -0.5 * _MARK_SCALE
    rank = jnp.where(marked, (s * (-1.0 / _MARK_SCALE) - 1.0) * 32.0, _UNRANKED)
    return rank, _col_sum(marked.astype(F32))


def _candidate_sums(ss1, ss2, iota8):
    blocks = [ss1[0:1, :] + ss2]
    for r1 in range(1, _CAND_BLOCK_ROWS):
        blk = ss1[r1:r1 + 1, :] + ss2[0:_CAND_BLOCK_ROWS, :]
        blocks.append(jnp.where(iota8 < float(_CAND_WIDTHS[r1]), blk, -jnp.inf))
    blocks.append(ss1[_CAND_BLOCK_ROWS:, :] + ss2[0:1, :])
    return jnp.concatenate(blocks, axis=0)


def _peer_prep_kernel(h_ref, g2_ref, wqt_ref, keys_ref,
                      hnt_ref, rk2_ref, b_ref, cnt_ref, a_ref,
                      qt_ref, s1_ref, s2_ref, tk_ref, *, tt):
    h = h_ref[...]
    ms = jnp.mean(h * h, axis=-1, keepdims=True)
    hnt = (h * lax.rsqrt(ms + EPS) * g2_ref[...]).T.astype(BF16)
    hnt_ref[...] = _pack_rows(hnt)
    qt_ref[...] = jnp.dot(wqt_ref[...], hnt, preferred_element_type=F32)

    key_id = lax.broadcasted_iota(jnp.int32, (PEER_N_KEYS, LANES), 0).astype(F32)
    iota16 = lax.broadcasted_iota(jnp.int32, (PEER_TOPK, LANES), 0).astype(F32)
    iota8 = lax.broadcasted_iota(jnp.int32, (_CAND_BLOCK_ROWS, LANES), 0).astype(F32)
    cand_id = jnp.concatenate(
        [iota16]
        + [iota8 + float(r1 * PEER_TOPK) for r1 in range(1, _CAND_BLOCK_ROWS)]
        + [(iota8 + float(_CAND_BLOCK_ROWS)) * float(PEER_TOPK)], axis=0)

    def head_body(hd, carry):
        for p, dst_ref in ((0, s1_ref), (1, s2_ref)):
            base = pl.multiple_of((hd * 2 + p) * PEER_N_KEYS, PEER_N_KEYS)
            qhp = qt_ref[pl.ds(base, PEER_N_KEYS), :].astype(BF16)
            dst_ref[...] = jnp.dot(keys_ref[hd, p], qhp, preferred_element_type=F32)

        def lane_group(c, carry2):
            lanes = pl.ds(pl.multiple_of(c * LANES, LANES), LANES)
            s1 = s1_ref[:, lanes]
            s2 = s2_ref[:, lanes]
            k = PEER_TOPK
            row_ss1, row_ss2, row_best, row_cnt = (slice(i * k, (i + 1) * k) for i in range(4))
            row_rk1 = slice(4 * k, 4 * k + PEER_N_KEYS)
            row_rk2 = slice(4 * k + PEER_N_KEYS, 4 * k + 2 * PEER_N_KEYS)

            (m1, m2), (ss1, ss2) = _top_values_unique((s1, s2), iota16)
            rk1, n1 = _decode_marks(m1)
            rk2, n2 = _decode_marks(m2)
            cand = _candidate_sums(ss1, ss2, iota8)
            best = jnp.zeros((k, LANES), F32)
            for r in range(k):
                m = _col_max(cand)
                cand = jnp.where(cand == m, _CAND_MARK, cand)
                best = jnp.where(iota16 == float(r), m, best)
            picked = (cand == _CAND_MARK).astype(F32)
            per_row = [jnp.sum(picked[0:k, :], axis=0, keepdims=True)]
            for r1 in range(1, _CAND_BLOCK_ROWS):
                lo = k + (r1 - 1) * _CAND_BLOCK_ROWS
                per_row.append(jnp.sum(picked[lo:lo + _CAND_BLOCK_ROWS, :], axis=0, keepdims=True))
            cnt16 = jnp.concatenate(per_row + [picked[k + 7 * _CAND_BLOCK_ROWS:, :]], axis=0)
            n3 = jnp.sum(cnt16, axis=0, keepdims=True)
            tk_ref[row_ss1, :] = ss1
            tk_ref[row_ss2, :] = ss2
            tk_ref[row_best, :] = best
            tk_ref[row_cnt, :] = cnt16
            tk_ref[row_rk1, :] = rk1
            tk_ref[row_rk2, :] = rk2
            had_tie = jnp.max(jnp.abs(n1 - k) + jnp.abs(n2 - k) + jnp.abs(n3 - k)) > 0.5

            @pl.when(had_tie)
            def _():
                ss1x, rk1x = _rank_keys(s1, key_id, iota16)
                ss2x, rk2x = _rank_keys(s2, key_id, iota16)
                state = [jnp.zeros((k, LANES), F32), jnp.zeros((k, LANES), F32)]

                def on_round(r, m, idx, sel):
                    state[0] = jnp.where(iota16 == float(r), m, state[0])
                    row = jnp.floor(idx * (1.0 / k))
                    state[1] = state[1] + (iota16 == row).astype(F32)

                _extract_top(_candidate_sums(ss1x, ss2x, iota8), cand_id, on_round)
                tk_ref[row_ss1, :] = ss1x
                tk_ref[row_ss2, :] = ss2x
                tk_ref[row_best, :] = state[0]
                tk_ref[row_cnt, :] = state[1]
                tk_ref[row_rk1, :] = rk1x
                tk_ref[row_rk2, :] = rk2x

            ss1, ss2, best, cnt16 = (tk_ref[rows, :] for rows in (row_ss1, row_ss2, row_best, row_cnt))
            rk1 = tk_ref[row_rk1, :]
            rk2 = tk_ref[row_rk2, :]
            z = jnp.sum(jnp.exp(best - best[0:1, :]), axis=0, keepdims=True)

            a_ref[hd, :, lanes] = jnp.exp(s1 - ss1[0:1, :])
            b_ref[hd, :, lanes] = _pack_rows((jnp.exp(s2 - ss2[0:1, :]) / z).astype(BF16))
            rk2_ref[hd, :, lanes] = _pack_rows(rk2.astype(BF16))
            cnt = jnp.zeros((PEER_N_KEYS, LANES), F32)
            for r1 in range(PEER_TOPK):
                cnt = jnp.where(rk1 == float(r1), cnt16[r1:r1 + 1, :], cnt)
            cnt_ref[hd, :, lanes] = cnt
            return carry2

        lax.fori_loop(0, tt // LANES, lane_group, 0)
        return carry

    lax.fori_loop(0, PEER_HEADS, head_body, 0)


def _peer_prep(h2, g2, wqt, keys, tt):
    n, d = h2.shape
    nt = n // tt
    state = lambda rows: pl.BlockSpec((None, PEER_HEADS, rows, tt), lambda i: (i, 0, 0, 0))
    shape = lambda rows, dt: jax.ShapeDtypeStruct((nt, PEER_HEADS, rows, tt), dt)
    half = PEER_N_KEYS // 2
    return pl.pallas_call(
        functools.partial(_peer_prep_kernel, tt=tt),
        grid=(nt,),
        in_specs=[pl.BlockSpec((tt, d), lambda i: (i, 0)), _full(g2.shape), _full(wqt.shape),
                  _full(keys.shape)],
        out_specs=[pl.BlockSpec((d // 2, tt), lambda i: (0, i)), state(half), state(half),
                   state(PEER_N_KEYS), state(PEER_N_KEYS)],
        out_shape=[jax.ShapeDtypeStruct((d // 2, n), jnp.uint32), shape(half, jnp.uint32),
                   shape(half, jnp.uint32), shape(PEER_N_KEYS, F32), shape(PEER_N_KEYS, F32)],
        scratch_shapes=[pltpu.VMEM((wqt.shape[0], tt), F32),
                        pltpu.VMEM((PEER_N_KEYS, tt), F32),
                        pltpu.VMEM((PEER_N_KEYS, tt), F32),
                        pltpu.VMEM((4 * PEER_TOPK + 2 * PEER_N_KEYS, LANES), F32)],
        compiler_params=_params(1),
        name="peer_prep",
    )(h2, g2, wqt, keys)


def _peer_main_kernel(hnt_ref, rk2_ref, b_ref, cnt_ref, a_ref, u_ref, vt_ref, res_ref,
                      y_ref, yt_ref, ht0_ref, ht1_ref, act0_ref, act1_ref, rows_ref,
                      *, keys_per_chunk, tt, n_chunks, n_items):
    g = pl.program_id(0)
    item3 = g - 2
    first_of_tile = (item3 >= 0) & (item3 % n_chunks == 0)
    last_of_tile = (item3 >= 0) & (item3 % n_chunks == n_chunks - 1)

    @pl.when(g == 0)
    def _():
        for ref in (ht0_ref, ht1_ref):
            ref[...] = jnp.zeros(ref.shape, F32)
        for ref in (act0_ref, act1_ref):
            ref[...] = jnp.zeros(ref.shape, jnp.uint32)

    @pl.when((g == 0) | first_of_tile)
    def _():
        yt_ref[...] = jnp.zeros(yt_ref.shape, F32)

    half = PEER_N_KEYS // 2
    chunk2 = jnp.clip(g - 1, 0, n_items - 1) % n_chunks

    def gate_block(c, k0, ht_old, act_new):
        lanes = slice(c * LANES, (c + 1) * LANES)
        ks = range(k0, k0 + KEYS_PER_PASS)
        gates = {k: jnp.zeros((PEER_N_KEYS, LANES), BF16) for k in ks}
        for hd in range(PEER_HEADS):
            rk2 = _unpack_rows(rk2_ref[hd, :, lanes])
            b = _unpack_rows(b_ref[hd, :, lanes])
            for k in ks:
                cnt_row = rows_ref[k, hd:hd + 1, lanes].astype(BF16)
                a_row = rows_ref[k, PEER_HEADS + hd:PEER_HEADS + hd + 1, lanes].astype(BF16)
                gates[k] = gates[k] + jnp.where(rk2 < cnt_row, b * a_row, 0.0)
        for k in ks:
            x = ht_old[k * PEER_N_KEYS:(k + 1) * PEER_N_KEYS, lanes]
            gelu = 0.5 * x * (1.0 + lax.erf(x * math.sqrt(0.5)))
            act_new[k * half:(k + 1) * half, lanes] = _pack_rows(gelu.astype(BF16) * gates[k])

    def step(ht_new, ht_old, act_new, act_old):
        for k in range(keys_per_chunk):
            i1 = chunk2 * keys_per_chunk + k
            for hd in range(PEER_HEADS):
                rows_ref[k, hd:hd + 1, :] = cnt_ref[hd, pl.ds(i1, 1), :]
                rows_ref[k, PEER_HEADS + hd:PEER_HEADS + hd + 1, :] = a_ref[hd, pl.ds(i1, 1), :]

        def u_piece(piece, q):
            cols = slice(piece * MXU_COLS, (piece + 1) * MXU_COLS)
            rp = u_ref.shape[0] // ROW_SPLIT
            ht_new[2 * q * rp:2 * (q + 1) * rp, cols] = jnp.dot(
                _unpack_rows(u_ref[q * rp:(q + 1) * rp, :]), _unpack_rows(hnt_ref[:, cols]),
                preferred_element_type=F32)

        def v_piece(piece, q):
            cols = slice(piece * MXU_COLS, (piece + 1) * MXU_COLS)
            rp = vt_ref.shape[0] // ROW_SPLIT
            yt_ref[2 * q * rp:2 * (q + 1) * rp, cols] += jnp.dot(
                _unpack_rows(vt_ref[q * rp:(q + 1) * rp, :]), _unpack_rows(act_old[:, cols]),
                preferred_element_type=F32)

        mxu_work = [functools.partial(f, p, q) for p in range(tt // MXU_COLS)
                    for f in (u_piece, v_piece) for q in range(ROW_SPLIT)]
        vpu_work = [functools.partial(gate_block, c, k0, ht_old, act_new)
                    for c in range(tt // LANES) for k0 in range(0, keys_per_chunk, KEYS_PER_PASS)]
        done_m = done_v = 0
        while done_m < len(mxu_work) or done_v < len(vpu_work):
            behind = done_m * len(vpu_work) < done_v * len(mxu_work)
            if (behind and done_m < len(mxu_work)) or done_v == len(vpu_work):
                mxu_work[done_m]()
                done_m += 1
            else:
                vpu_work[done_v]()
                done_v += 1

    @pl.when(g % 2 == 0)
    def _():
        step(ht0_ref, ht1_ref, act1_ref, act0_ref)

    @pl.when(g % 2 == 1)
    def _():
        step(ht1_ref, ht0_ref, act0_ref, act1_ref)

    @pl.when(last_of_tile)
    def _():
        y_ref[...] = res_ref[...] + yt_ref[...].T


def _peer_main(hnt, rk2, b, cnt, a, u_packed, vt_packed, h2, tt, chunk):
    n, d = h2.shape
    n_chunks = 2 * u_packed.shape[0] // chunk
    n_items = (n // tt) * n_chunks
    kpc = chunk // PEER_N_KEYS
    assert kpc % KEYS_PER_PASS == 0 and chunk % (2 * ROW_SPLIT) == 0 and d % (2 * ROW_SPLIT) == 0
    item = lambda g, lag: jnp.clip(g - lag, 0, n_items - 1)
    tile = lambda g, lag: item(g, lag) // n_chunks
    chunk_of = lambda g, lag: item(g, lag) % n_chunks
    state = lambda arr: pl.BlockSpec((None,) + arr.shape[1:], lambda g: (tile(g, 1), 0, 0, 0))
    return pl.pallas_call(
        functools.partial(_peer_main_kernel, keys_per_chunk=kpc, tt=tt, n_chunks=n_chunks,
                          n_items=n_items),
        grid=(n_items + 2,),
        in_specs=[pl.BlockSpec((d // 2, tt), lambda g: (0, tile(g, 0))), state(rk2), state(b),
                  state(cnt), state(a),
                  pl.BlockSpec((chunk // 2, d), lambda g: (chunk_of(g, 0), 0)),
                  pl.BlockSpec((d // 2, chunk), lambda g: (0, chunk_of(g, 2))),
                  pl.BlockSpec((tt, d), lambda g: (tile(g, 2), 0))],
        out_specs=pl.BlockSpec((tt, d), lambda g: (tile(g, 2), 0)),
        out_shape=jax.ShapeDtypeStruct((n, d), F32),
        scratch_shapes=[pltpu.VMEM((d, tt), F32),
                        pltpu.VMEM((chunk, tt), F32),
                        pltpu.VMEM((chunk, tt), F32),
                        pltpu.VMEM((chunk // 2, tt), jnp.uint32),
                        pltpu.VMEM((chunk // 2, tt), jnp.uint32),
                        pltpu.VMEM((kpc, 2 * PEER_HEADS, tt), F32)],
        compiler_params=_params(1),
        name="peer_main",
    )(hnt, rk2, b, cnt, a, u_packed, vt_packed, h2)


def _pack_table_kernel(w_ref, o_ref, *, transpose):
    w = w_ref[...]
    o_ref[...] = _pack_rows((w.T if transpose else w).astype(BF16))


def _pack_table(w, transpose, rows=512):
    r, c = w.shape
    if transpose:
        out_shape, out_spec = (c // 2, r), pl.BlockSpec((c // 2, rows), lambda i: (0, i))
    else:
        out_shape, out_spec = (r // 2, c), pl.BlockSpec((rows // 2, c), lambda i: (i, 0))
    return pl.pallas_call(
        functools.partial(_pack_table_kernel, transpose=transpose),
        grid=(r // rows,),
        in_specs=[pl.BlockSpec((rows, c), lambda i: (i, 0))],
        out_specs=out_spec,
        out_shape=jax.ShapeDtypeStruct(out_shape, jnp.uint32),
        compiler_params=_params(1),
        name="pack_vt" if transpose else "pack_u",
    )(w)


def _layer(x2, batch, seq, w_in, conv_b_glu, conv_dw_w, conv_dw_b, conv_ln_g, conv_ln_b, conv_w_proj,
           conv_b_proj, q_norm_g, k_norm_g, rel_bias, attn_w_proj, mix_w_out, norm1_g, norm2_g,
           peer_w_q, peer_sub_keys, peer_u, peer_v):
    n, d = x2.shape
    row = lambda v: v.reshape(1, -1).astype(F32)
    c0 = 2 * CONV_CH
    cuts = [c0, c0 + ATTN_WIDTH, c0 + 2 * ATTN_WIDTH, c0 + 3 * ATTN_WIDTH]
    wglu, wq, wk, wv, wg = [w.astype(BF16) for w in jnp.split(w_in, cuts, axis=-1)]
    head_of = np.arange(ATTN_WIDTH) // HEAD_DIM
    pm = jnp.asarray((head_of[:, None] == head_of[None, :]) / HEAD_DIM, BF16)
    tile_heads = lambda g: jnp.tile(g.astype(F32), N_ATTN_HEADS).reshape(1, ATTN_WIDTH)

    u, (qs, ks, vs), gates = _inproj(x2, row(norm1_g), wglu, row(conv_b_glu), wq, wk, wv,
                                     tile_heads(q_norm_g), tile_heads(k_norm_g), pm, wg,
                                     tm=512, batch=batch, seq=seq)

    outs, lses = [], []
    for g, (_, dil) in enumerate(ATTN_GROUPS):
        o, lse = _attn_group(qs[g], ks[g], vs[g], _band_bias(rel_bias, g, dil), dil)
        outs.append(o)
        lses.append(lse)

    h2 = _mixer(u, outs, lses, gates, x2, conv_dw_w.reshape(CONV_WIDTH, CONV_CH).astype(F32),
                row(conv_dw_b), row(conv_ln_g), row(conv_ln_b), conv_w_proj.astype(BF16),
                row(conv_b_proj), attn_w_proj.astype(BF16), mix_w_out.astype(BF16),
                tm=min(512, seq), seq=seq)

    tt = min(512, n)
    hnt, rk2, b, cnt, a = _peer_prep(h2, row(norm2_g), peer_w_q.T.astype(BF16),
                                     peer_sub_keys.astype(BF16), tt)
    return _peer_main(hnt, rk2, b, cnt, a, _pack_table(peer_u, transpose=False),
                      _pack_table(peer_v, transpose=True), h2, tt, chunk=1024)


def kernel(x, w_in, conv_b_glu, conv_dw_w, conv_dw_b, conv_ln_g, conv_ln_b, conv_w_proj, conv_b_proj, q_norm_g, k_norm_g, rel_bias, attn_w_proj, mix_w_out, norm1_g, norm2_g, peer_w_q, peer_sub_keys, peer_u, peer_v):
    batch, seq, d = x.shape
    x2 = x.reshape(batch * seq, d)
    for l in range(w_in.shape[0]):
        x2 = _layer(x2, batch, seq, w_in[l], conv_b_glu[l], conv_dw_w[l], conv_dw_b[l], conv_ln_g[l],
                    conv_ln_b[l], conv_w_proj[l], conv_b_proj[l], q_norm_g[l], k_norm_g[l], rel_bias,
                    attn_w_proj[l], mix_w_out[l], norm1_g[l], norm2_g[l], peer_w_q[l],
                    peer_sub_keys[l], peer_u[l], peer_v[l])
    return x2.reshape(batch, seq, d)
```

```python
import functools
import math

import numpy as np
import jax
import jax.numpy as jnp
from jax import lax
from jax.experimental import pallas as pl
from jax.experimental.pallas import tpu as pltpu

F32 = jnp.float32
BF16 = jnp.bfloat16

HEAD_DIM = 64
HEADS_PER_GROUP = 4
ATTN_GROUPS = ((128, 1), (512, 4), (2048, 16))
N_ATTN_HEADS = HEADS_PER_GROUP * len(ATTN_GROUPS)
ATTN_WIDTH = N_ATTN_HEADS * HEAD_DIM
GROUP_WIDTH = HEADS_PER_GROUP * HEAD_DIM
KEY_STEPS = 128
NEG_INF = -1e30
CONV_CH = 512
CONV_WIDTH = 31
CONV_HALO = 32
REL_BUCKETS = 32
REL_MAX_DISTANCE = 2048
PEER_HEADS = 8
PEER_N_KEYS = 128
PEER_TOPK = 16
EPS = 1e-6

LANES = 128
SUBLANES = 8
CONV_ROW_CHUNK = 32
MXU_COLS = 256
ROW_SPLIT = 4
KEYS_PER_PASS = 4
VMEM_LIMIT = 56 * 1024 * 1024

_NT = (((1,), (1,)), ((), ()))


def _params(n_axes):
    return pltpu.CompilerParams(dimension_semantics=("arbitrary",) * n_axes,
                                vmem_limit_bytes=VMEM_LIMIT)


def _full(shape):
    n = len(shape)
    return pl.BlockSpec(shape, lambda *_: (0,) * n)


def _inproj_kernel(x_ref, g1_ref, wglu_ref, bglu_ref, wq_ref, wk_ref, wv_ref, qg_ref, kg_ref,
                   pm_ref, wg_ref, u_ref, *rest, tm):
    qkv_refs, gate_ref, stage = rest[:9], rest[9], rest[10]
    x = x_ref[...]
    ms = jnp.mean(x * x, axis=-1, keepdims=True)
    xn = (x * lax.rsqrt(ms + EPS) * g1_ref[...]).astype(BF16)

    glu = jnp.dot(xn, wglu_ref[...], preferred_element_type=F32) + bglu_ref[...]
    u_ref[...] = glu[:, :CONV_CH] * jax.nn.sigmoid(glu[:, CONV_CH:])

    def head_rmsnorm(w_ref, g_ref):
        y = jnp.dot(xn, w_ref[...], preferred_element_type=F32)
        msq = jnp.dot((y * y).astype(BF16), pm_ref[...], preferred_element_type=F32)
        return y * lax.rsqrt(msq + EPS) * g_ref[...]

    tensors = (head_rmsnorm(wq_ref, qg_ref) * (HEAD_DIM ** -0.5),
               head_rmsnorm(wk_ref, kg_ref),
               jnp.dot(xn, wv_ref[...], preferred_element_type=F32))
    for t, y in enumerate(tensors):
        for g, (_, dil) in enumerate(ATTN_GROUPS):
            out_ref = qkv_refs[t * len(ATTN_GROUPS) + g]
            yg = y[:, g * GROUP_WIDTH:(g + 1) * GROUP_WIDTH]
            if dil == 1:
                out_ref[0] = yg.astype(BF16)
                continue
            for s in range(GROUP_WIDTH // LANES):
                stage[s] = yg[:, s * LANES:(s + 1) * LANES]
            for r in range(dil):
                for s in range(GROUP_WIDTH // LANES):
                    out_ref[r, :, s * LANES:(s + 1) * LANES] = (
                        stage[s, pl.ds(r, tm // dil, stride=dil), :].astype(BF16))
    gate_ref[...] = jax.nn.sigmoid(
        jnp.dot(xn, wg_ref[...], preferred_element_type=F32)).astype(BF16)


def _inproj(x2, g1, wglu, bglu, wq, wk, wv, qg, kg, pm, wg, tm, batch, seq):
    n, d = x2.shape
    tiles_per_seq = seq // tm
    row = lambda w: pl.BlockSpec((tm, w), lambda i: (i, 0))
    grouped = lambda dil: pl.BlockSpec((None, dil, tm // dil, GROUP_WIDTH),
                                       lambda i: (i // tiles_per_seq, 0, i % tiles_per_seq, 0))
    grouped_shape = lambda dil: jax.ShapeDtypeStruct((batch, dil, seq // dil, GROUP_WIDTH), BF16)
    dils = [dil for _ in range(3) for _, dil in ATTN_GROUPS]
    outs = pl.pallas_call(
        functools.partial(_inproj_kernel, tm=tm),
        grid=(n // tm,),
        in_specs=[row(d), _full(g1.shape), _full(wglu.shape), _full(bglu.shape), _full(wq.shape),
                  _full(wk.shape), _full(wv.shape), _full(qg.shape), _full(kg.shape),
                  _full(pm.shape), _full(wg.shape)],
        out_specs=[row(CONV_CH)] + [grouped(dil) for dil in dils] + [row(2 * d)],
        out_shape=[jax.ShapeDtypeStruct((n, CONV_CH), F32)] + [grouped_shape(dil) for dil in dils]
                  + [jax.ShapeDtypeStruct((n, 2 * d), BF16)],
        scratch_shapes=[pltpu.VMEM((GROUP_WIDTH // LANES, tm, LANES), F32)],
        compiler_params=_params(1),
        name="inproj",
    )(x2, g1, wglu, bglu, wq, wk, wv, qg, kg, pm, wg)
    qkv = [list(outs[1 + 3 * t:4 + 3 * t]) for t in range(3)]
    return outs[0], qkv, outs[10]


def _attn_kernel(q_ref, kc_ref, kp_ref, vc_ref, vp_ref, bias_ref, o_ref, lse_ref, kbuf, vbuf,
                 *, nsub):
    first_blk = (pl.program_id(2) == 0).astype(jnp.int32)
    kbuf[0:KEY_STEPS, :] = kp_ref[...]
    kbuf[KEY_STEPS:, :] = kc_ref[...]
    vbuf[0:KEY_STEPS, :] = vp_ref[...]
    vbuf[KEY_STEPS:, :] = vc_ref[...]
    lane = lax.broadcasted_iota(jnp.int32, (1, LANES), 1)
    lo = lane < HEAD_DIM
    head_mask = (lo.astype(BF16), (~lo).astype(BF16))
    for pair in range(GROUP_WIDTH // LANES):
        cols = slice(pair * LANES, (pair + 1) * LANES)
        for sub in range(nsub):
            r0 = sub * KEY_STEPS
            qs = q_ref[r0:r0 + KEY_STEPS, cols]
            kw = kbuf[r0:r0 + 2 * KEY_STEPS, cols]
            vw = vbuf[r0:r0 + 2 * KEY_STEPS, cols]
            outs, lses = [], []
            for hh in range(2):
                s = lax.dot_general(qs * head_mask[hh], kw, _NT, preferred_element_type=F32)
                variant = first_blk if sub == 0 else 0
                s = s + bias_ref[variant, pair * 2 + hh]
                m = jnp.max(s, axis=1, keepdims=True)
                p = jnp.exp(s - m)
                l = jnp.sum(p, axis=1, keepdims=True)
                pv = jnp.dot(p.astype(BF16), vw, preferred_element_type=F32)
                outs.append(pv / l)
                lses.append(m + jnp.log(l))
            o_ref[r0:r0 + KEY_STEPS, cols] = jnp.where(lo, outs[0], outs[1]).astype(BF16)
            lse_ref[r0:r0 + KEY_STEPS, cols] = jnp.where(lo, lses[0], lses[1])


def _attn_group(q, k, v, bias, dil):
    batch, _, sub_len, _ = q.shape
    tq = min(512, sub_len)
    nsub = tq // KEY_STEPS
    cur = pl.BlockSpec((None, None, tq, GROUP_WIDTH), lambda b, r, i: (b, r, i, 0))
    prev = pl.BlockSpec((None, None, KEY_STEPS, GROUP_WIDTH),
                        lambda b, r, i: (b, r, jnp.maximum(i * nsub - 1, 0), 0))
    return pl.pallas_call(
        functools.partial(_attn_kernel, nsub=nsub),
        grid=(batch, dil, sub_len // tq),
        in_specs=[cur, cur, prev, cur, prev, _full(bias.shape)],
        out_specs=[cur, cur],
        out_shape=[jax.ShapeDtypeStruct(q.shape, BF16), jax.ShapeDtypeStruct(q.shape, F32)],
        scratch_shapes=[pltpu.VMEM((tq + KEY_STEPS, GROUP_WIDTH), BF16),
                        pltpu.VMEM((tq + KEY_STEPS, GROUP_WIDTH), BF16)],
        compiler_params=_params(3),
        name=f"attn_dil{dil}",
    )(q, k, k, v, v, bias)


def _t5_causal_bucket(distance):
    n = distance.astype(jnp.int32)
    max_exact = REL_BUCKETS // 2
    nf = jnp.maximum(n, 1).astype(F32)
    large = max_exact + (jnp.log(nf / max_exact) / math.log(REL_MAX_DISTANCE / max_exact)
                         * (REL_BUCKETS - max_exact)).astype(jnp.int32)
    large = jnp.minimum(large, REL_BUCKETS - 1)
    return jnp.where(n < max_exact, n, large)


def _band_bias(rel_bias, g, dil):
    steps = jnp.arange(KEY_STEPS + 1)
    per_step = rel_bias[_t5_causal_bucket(steps * dil)][:, g * HEADS_PER_GROUP:(g + 1) * HEADS_PER_GROUP].T
    k = KEY_STEPS
    pad = lambda w: jnp.full((HEADS_PER_GROUP, w), NEG_INF, F32)
    f = jnp.concatenate([pad(k - 1), per_step[:, ::-1].astype(F32), pad(k)], axis=1)
    width = f.shape[1]
    normal = jnp.tile(f, (1, k))[:, :k * (width - 1)].reshape(HEADS_PER_GROUP, k, width - 1)
    normal = normal[:, :, k - 1:k - 1 + 2 * k]
    before_start = np.arange(2 * k)[None, None, :] < k
    at_start = jnp.where(before_start, NEG_INF, normal)
    return jnp.stack([normal, at_start])


def _mixer_kernel(u_ref, halo_ref, o0_ref, o1_ref, o2_ref, l0_ref, l1_ref, l2_ref, gate_ref, x_ref,
                  dw_ref, dwb_ref, lng_ref, lnb_ref, wc_ref, bc_ref, wa_ref, wo_ref,
                  h_ref, uext, order_ref, shift_ref, conv_ref, *, tm, tiles_per_seq):
    d = x_ref.shape[1]

    def token_order(ref, dil, slot):
        if dil == 1:
            return ref[0].astype(F32)
        for r in range(dil):
            blk = ref[r].astype(F32)
            for s in range(GROUP_WIDTH // LANES):
                order_ref[slot, s, pl.ds(r, tm // dil, stride=dil), :] = blk[:, s * LANES:(s + 1) * LANES]
        return jnp.concatenate([order_ref[slot, s] for s in range(GROUP_WIDTH // LANES)], axis=1)

    at_seq_start = pl.program_id(0) % tiles_per_seq == 0
    uext[0:CONV_HALO, :] = jnp.where(at_seq_start, 0.0, halo_ref[...])
    uext[CONV_HALO:, :] = u_ref[...]
    shift_rows = shift_ref.shape[1]
    for b in range(1, SUBLANES):
        shift_ref[b - 1] = uext[b:b + shift_rows, :]

    def conv_rows(rc, carry):
        base = pl.multiple_of(rc * CONV_ROW_CHUNK, CONV_ROW_CHUNK)
        acc = jnp.broadcast_to(dwb_ref[...], (CONV_ROW_CHUNK, CONV_CH))
        for w in range(CONV_WIDTH):
            a, b = divmod(CONV_HALO - (CONV_WIDTH - 1) + w, SUBLANES)
            rows = pl.ds(base + a * SUBLANES, CONV_ROW_CHUNK)
            tap = uext[rows, :] if b == 0 else shift_ref[b - 1, rows, :]
            acc = acc + tap * dw_ref[w:w + 1, :]
        conv_ref[pl.ds(base, CONV_ROW_CHUNK), :] = acc
        return carry

    lax.fori_loop(0, tm // CONV_ROW_CHUNK, conv_rows, 0)

    dils = [dil for _, dil in ATTN_GROUPS]
    lses = [token_order(ref, dil, 2 * g) for g, (ref, dil) in enumerate(zip((l0_ref, l1_ref, l2_ref), dils))]
    outs = [token_order(ref, dil, 2 * g + 1) for g, (ref, dil) in enumerate(zip((o0_ref, o1_ref, o2_ref), dils))]

    for hf in range(2):
        rows = slice(hf * tm // 2, (hf + 1) * tm // 2)
        c = conv_ref[rows, :]
        mu = jnp.mean(c, axis=-1, keepdims=True)
        cc = c - mu
        var = jnp.mean(cc * cc, axis=-1, keepdims=True)
        c = cc * lax.rsqrt(var + EPS) * lng_ref[...] + lnb_ref[...]
        c = c * jax.nn.sigmoid(c)
        conv_out = jnp.dot(c.astype(BF16), wc_ref[...], preferred_element_type=F32) + bc_ref[...]

        lse_h = [l[rows] for l in lses]
        m = jnp.maximum(jnp.maximum(lse_h[0], lse_h[1]), lse_h[2])
        es = [jnp.exp(l - m) for l in lse_h]
        inv = 1.0 / (es[0] + es[1] + es[2])
        attn_out = jnp.zeros((tm // 2, d), F32)
        for g in range(3):
            og = (outs[g][rows] * (es[g] * inv)).astype(BF16)
            attn_out = attn_out + jnp.dot(og, wa_ref[g * GROUP_WIDTH:(g + 1) * GROUP_WIDTH, :],
                                          preferred_element_type=F32)

        merged = (gate_ref[rows, :d].astype(F32) * conv_out
                  + gate_ref[rows, d:].astype(F32) * attn_out)
        h_ref[rows, :] = x_ref[rows, :] + jnp.dot(merged.astype(BF16), wo_ref[...],
                                                  preferred_element_type=F32)


def _mixer(u, outs, lses, gates, x2, dw, dwb, lng, lnb, wc, bc, wa, wo, tm, seq):
    n, d = x2.shape
    row = lambda w: pl.BlockSpec((tm, w), lambda i: (i, 0))
    halo = pl.BlockSpec((CONV_HALO, CONV_CH),
                        lambda i: (jnp.maximum(i * (tm // CONV_HALO) - 1, 0), 0))
    tiles_per_seq = seq // tm
    grouped = lambda dil: pl.BlockSpec((None, dil, tm // dil, GROUP_WIDTH),
                                       lambda i: (i // tiles_per_seq, 0, i % tiles_per_seq, 0))
    group_specs = [grouped(dil) for _, dil in ATTN_GROUPS]
    return pl.pallas_call(
        functools.partial(_mixer_kernel, tm=tm, tiles_per_seq=tiles_per_seq),
        grid=(n // tm,),
        in_specs=[row(CONV_CH), halo] + group_specs * 2 + [row(2 * d), row(d)]
                 + [_full(a.shape) for a in (dw, dwb, lng, lnb, wc, bc, wa, wo)],
        out_specs=row(d),
        out_shape=jax.ShapeDtypeStruct((n, d), F32),
        scratch_shapes=[pltpu.VMEM((tm + CONV_HALO, CONV_CH), F32),
                        pltpu.VMEM((2 * len(ATTN_GROUPS), GROUP_WIDTH // LANES, tm, LANES), F32),
                        pltpu.VMEM((SUBLANES - 1, tm + CONV_HALO - SUBLANES, CONV_CH), F32),
                        pltpu.VMEM((tm, CONV_CH), F32)],
        compiler_params=_params(1),
        name="mixer_out",
    )(u, u, *outs, *lses, gates, x2, dw, dwb, lng, lnb, wc, bc, wa, wo)


_CAND_WIDTHS = tuple(PEER_TOPK // (r1 + 1) for r1 in range(PEER_TOPK))
_CAND_BLOCK_ROWS = 8
_UNRANKED = float(PEER_N_KEYS)


def _pack_rows(x):
    return pltpu.bitcast(x, jnp.uint32)


def _unpack_rows(x):
    return pltpu.bitcast(x, BF16)


def _col_reduce(x, op, reduce_fn):
    parts = [x[i:i + 8] for i in range(0, x.shape[0], 8)]
    while len(parts) > 1:
        nxt = [op(parts[i], parts[i + 1]) for i in range(0, len(parts) - 1, 2)]
        parts = nxt + ([parts[-1]] if len(parts) % 2 else [])
    return reduce_fn(parts[0], axis=0, keepdims=True)


def _col_max(x):
    return _col_reduce(x, jnp.maximum, jnp.max)


def _col_min(x):
    return _col_reduce(x, jnp.minimum, jnp.min)


def _col_sum(x):
    return _col_reduce(x, jnp.add, jnp.sum)


def _extract_top(s, key_id, on_round):
    for r in range(PEER_TOPK):
        m = _col_max(s)
        idx = _col_min(jnp.where(s == m, key_id, 1e9))
        sel = key_id == idx
        s = jnp.where(sel, -jnp.inf, s)
        on_round(r, m, idx, sel)


def _rank_keys(s, key_id, iota16):
    state = [jnp.zeros(iota16.shape, F32), jnp.full(s.shape, _UNRANKED, F32)]

    def on_round(r, m, idx, sel):
        state[0] = jnp.where(iota16 == float(r), m, state[0])
        state[1] = jnp.where(sel, float(r), state[1])

    _extract_top(s, key_id, on_round)
    return state[0], state[1]


_MARK_SCALE = 2.0 ** 100
_CAND_MARK = -_MARK_SCALE


def _top_values_unique(scores, iota16):
    scores = list(scores)
    tops = [jnp.zeros(iota16.shape, F32) for _ in scores]
    for r in range(PEER_TOPK):
        for i, s in enumerate(scores):
            m = _col_max(s)
            scores[i] = jnp.where(s == m, -_MARK_SCALE * (1.0 + r / 32.0), s)
            tops[i] = jnp.where(iota16 == float(r), m, tops[i])
    return scores, tops


def _decode_marks(s):
    marked = s < -0.5 * _MARK_SCALE
    rank = jnp.where(marked, (s * (-1.0 / _MARK_SCALE) - 1.0) * 32.0, _UNRANKED)
    return rank, _col_sum(marked.astype(F32))


def _candidate_sums(ss1, ss2, iota8):
    blocks = [ss1[0:1, :] + ss2]
    for r1 in range(1, _CAND_BLOCK_ROWS):
        blk = ss1[r1:r1 + 1, :] + ss2[0:_CAND_BLOCK_ROWS, :]
        blocks.append(jnp.where(iota8 < float(_CAND_WIDTHS[r1]), blk, -jnp.inf))
    blocks.append(ss1[_CAND_BLOCK_ROWS:, :] + ss2[0:1, :])
    return jnp.concatenate(blocks, axis=0)


def _peer_prep_kernel(h_ref, g2_ref, wqt_ref, keys_ref,
                      hnt_ref, rk2_ref, b_ref, cnt_ref, a_ref,
                      qt_ref, s1_ref, s2_ref, tk_ref, *, tt):
    h = h_ref[...]
    ms = jnp.mean(h * h, axis=-1, keepdims=True)
    hnt = (h * lax.rsqrt(ms + EPS) * g2_ref[...]).T.astype(BF16)
    hnt_ref[...] = _pack_rows(hnt)
    qt_ref[...] = jnp.dot(wqt_ref[...], hnt, preferred_element_type=F32)

    key_id = lax.broadcasted_iota(jnp.int32, (PEER_N_KEYS, LANES), 0).astype(F32)
    iota16 = lax.broadcasted_iota(jnp.int32, (PEER_TOPK, LANES), 0).astype(F32)
    iota8 = lax.broadcasted_iota(jnp.int32, (_CAND_BLOCK_ROWS, LANES), 0).astype(F32)
    cand_id = jnp.concatenate(
        [iota16]
        + [iota8 + float(r1 * PEER_TOPK) for r1 in range(1, _CAND_BLOCK_ROWS)]
        + [(iota8 + float(_CAND_BLOCK_ROWS)) * float(PEER_TOPK)], axis=0)

    def head_body(hd, carry):
        for p, dst_ref in ((0, s1_ref), (1, s2_ref)):
            base = pl.multiple_of((hd * 2 + p) * PEER_N_KEYS, PEER_N_KEYS)
            qhp = qt_ref[pl.ds(base, PEER_N_KEYS), :].astype(BF16)
            dst_ref[...] = jnp.dot(keys_ref[hd, p], qhp, preferred_element_type=F32)

        def lane_group_pair(cp, carry2):
            lane_sets = [pl.ds(pl.multiple_of((2 * cp + j) * LANES, LANES), LANES) for j in range(2)]
            k = PEER_TOPK
            row_ss1, row_ss2, row_best, row_cnt = (slice(i * k, (i + 1) * k) for i in range(4))
            row_rk1 = slice(4 * k, 4 * k + PEER_N_KEYS)
            row_rk2 = slice(4 * k + PEER_N_KEYS, 4 * k + 2 * PEER_N_KEYS)

            tops, ties = {}, []
            for half, (s_ref, row_ss, row_rk) in enumerate(((s1_ref, row_ss1, row_rk1),
                                                            (s2_ref, row_ss2, row_rk2))):
                marked, tops[half] = _top_values_unique([s_ref[:, l] for l in lane_sets], iota16)
                for j in range(2):
                    rk, n = _decode_marks(marked[j])
                    tk_ref[j, row_ss, :] = tops[half][j]
                    tk_ref[j, row_rk, :] = rk
                    ties.append(jnp.abs(n - k))
            cands = [_candidate_sums(tops[0][j], tops[1][j], iota8) for j in range(2)]
            bests = [jnp.zeros((k, LANES), F32) for _ in range(2)]
            for r in range(k):
                for j in range(2):
                    m = _col_max(cands[j])
                    cands[j] = jnp.where(cands[j] == m, _CAND_MARK, cands[j])
                    bests[j] = jnp.where(iota16 == float(r), m, bests[j])
            for j in range(2):
                picked = (cands[j] == _CAND_MARK).astype(F32)
                per_row = [jnp.sum(picked[0:k, :], axis=0, keepdims=True)]
                for r1 in range(1, _CAND_BLOCK_ROWS):
                    lo = k + (r1 - 1) * _CAND_BLOCK_ROWS
                    per_row.append(jnp.sum(picked[lo:lo + _CAND_BLOCK_ROWS, :], axis=0, keepdims=True))
                cnt16 = jnp.concatenate(per_row + [picked[k + 7 * _CAND_BLOCK_ROWS:, :]], axis=0)
                ties.append(jnp.abs(jnp.sum(cnt16, axis=0, keepdims=True) - k))
                tk_ref[j, row_best, :] = bests[j]
                tk_ref[j, row_cnt, :] = cnt16
            had_tie = jnp.max(functools.reduce(jnp.add, ties)) > 0.5

            @pl.when(had_tie)
            def _():
                for j in range(2):
                    ss1x, rk1x = _rank_keys(s1_ref[:, lane_sets[j]], key_id, iota16)
                    ss2x, rk2x = _rank_keys(s2_ref[:, lane_sets[j]], key_id, iota16)
                    state = [jnp.zeros((k, LANES), F32), jnp.zeros((k, LANES), F32)]

                    def on_round(r, m, idx, sel):
                        state[0] = jnp.where(iota16 == float(r), m, state[0])
                        row = jnp.floor(idx * (1.0 / k))
                        state[1] = state[1] + (iota16 == row).astype(F32)

                    _extract_top(_candidate_sums(ss1x, ss2x, iota8), cand_id, on_round)
                    tk_ref[j, row_ss1, :] = ss1x
                    tk_ref[j, row_ss2, :] = ss2x
                    tk_ref[j, row_best, :] = state[0]
                    tk_ref[j, row_cnt, :] = state[1]
                    tk_ref[j, row_rk1, :] = rk1x
                    tk_ref[j, row_rk2, :] = rk2x

            for j, lanes in enumerate(lane_sets):
                ss1, ss2, best, cnt16 = (tk_ref[j, rows, :]
                                         for rows in (row_ss1, row_ss2, row_best, row_cnt))
                rk1 = tk_ref[j, row_rk1, :]
                rk2 = tk_ref[j, row_rk2, :]
                z = jnp.sum(jnp.exp(best - best[0:1, :]), axis=0, keepdims=True)

                a_ref[hd, :, lanes] = jnp.exp(s1_ref[:, lanes] - ss1[0:1, :])
                b_ref[hd, :, lanes] = _pack_rows((jnp.exp(s2_ref[:, lanes] - ss2[0:1, :]) / z).astype(BF16))
                rk2_ref[hd, :, lanes] = _pack_rows(rk2.astype(BF16))
                cnt = jnp.zeros((PEER_N_KEYS, LANES), F32)
                for r1 in range(PEER_TOPK):
                    cnt = jnp.where(rk1 == float(r1), cnt16[r1:r1 + 1, :], cnt)
                cnt_ref[hd, :, lanes] = cnt
            return carry2

        lax.fori_loop(0, tt // (2 * LANES), lane_group_pair, 0)
        return carry

    lax.fori_loop(0, PEER_HEADS, head_body, 0)


def _peer_prep(h2, g2, wqt, keys, tt):
    n, d = h2.shape
    nt = n // tt
    state = lambda rows: pl.BlockSpec((None, PEER_HEADS, rows, tt), lambda i: (i, 0, 0, 0))
    shape = lambda rows, dt: jax.ShapeDtypeStruct((nt, PEER_HEADS, rows, tt), dt)
    half = PEER_N_KEYS // 2
    return pl.pallas_call(
        functools.partial(_peer_prep_kernel, tt=tt),
        grid=(nt,),
        in_specs=[pl.BlockSpec((tt, d), lambda i: (i, 0)), _full(g2.shape), _full(wqt.shape),
                  _full(keys.shape)],
        out_specs=[pl.BlockSpec((d // 2, tt), lambda i: (0, i)), state(half), state(half),
                   state(PEER_N_KEYS), state(PEER_N_KEYS)],
        out_shape=[jax.ShapeDtypeStruct((d // 2, n), jnp.uint32), shape(half, jnp.uint32),
                   shape(half, jnp.uint32), shape(PEER_N_KEYS, F32), shape(PEER_N_KEYS, F32)],
        scratch_shapes=[pltpu.VMEM((wqt.shape[0], tt), F32),
                        pltpu.VMEM((PEER_N_KEYS, tt), F32),
                        pltpu.VMEM((PEER_N_KEYS, tt), F32),
                        pltpu.VMEM((2, 4 * PEER_TOPK + 2 * PEER_N_KEYS, LANES), F32)],
        compiler_params=_params(1),
        name="peer_prep",
    )(h2, g2, wqt, keys)


def _peer_main_kernel(hnt_ref, rk2_ref, b_ref, cnt_ref, a_ref, u_ref, vt_ref, res_ref,
                      y_ref, yt_ref, ht0_ref, ht1_ref, act0_ref, act1_ref, rows_ref,
                      *, keys_per_chunk, tt, n_chunks, n_items):
    g = pl.program_id(0)
    item3 = g - 2
    first_of_tile = (item3 >= 0) & (item3 % n_chunks == 0)
    last_of_tile = (item3 >= 0) & (item3 % n_chunks == n_chunks - 1)

    @pl.when(g == 0)
    def _():
        for ref in (ht0_ref, ht1_ref):
            ref[...] = jnp.zeros(ref.shape, F32)
        for ref in (act0_ref, act1_ref):
            ref[...] = jnp.zeros(ref.shape, jnp.uint32)

    @pl.when((g == 0) | first_of_tile)
    def _():
        yt_ref[...] = jnp.zeros(yt_ref.shape, F32)

    half = PEER_N_KEYS // 2
    chunk2 = jnp.clip(g - 1, 0, n_items - 1) % n_chunks

    def gate_block(c, k0, ht_old, act_new):
        lanes = slice(c * LANES, (c + 1) * LANES)
        ks = range(k0, k0 + KEYS_PER_PASS)
        gates = {k: jnp.zeros((PEER_N_KEYS, LANES), BF16) for k in ks}
        for hd in range(PEER_HEADS):
            rk2 = _unpack_rows(rk2_ref[hd, :, lanes])
            b = _unpack_rows(b_ref[hd, :, lanes])
            for k in ks:
                cnt_row = rows_ref[k, hd:hd + 1, lanes].astype(BF16)
                a_row = rows_ref[k, PEER_HEADS + hd:PEER_HEADS + hd + 1, lanes].astype(BF16)
                gates[k] = gates[k] + jnp.where(rk2 < cnt_row, b * a_row, 0.0)
        for k in ks:
            x = ht_old[k * PEER_N_KEYS:(k + 1) * PEER_N_KEYS, lanes]
            gelu = 0.5 * x * (1.0 + lax.erf(x * math.sqrt(0.5)))
            act_new[k * half:(k + 1) * half, lanes] = _pack_rows(gelu.astype(BF16) * gates[k])

    def step(ht_new, ht_old, act_new, act_old):
        for k in range(keys_per_chunk):
            i1 = chunk2 * keys_per_chunk + k
            for hd in range(PEER_HEADS):
                rows_ref[k, hd:hd + 1, :] = cnt_ref[hd, pl.ds(i1, 1), :]
                rows_ref[k, PEER_HEADS + hd:PEER_HEADS + hd + 1, :] = a_ref[hd, pl.ds(i1, 1), :]

        def u_piece(piece, q):
            cols = slice(piece * MXU_COLS, (piece + 1) * MXU_COLS)
            rp = u_ref.shape[0] // ROW_SPLIT
            ht_new[2 * q * rp:2 * (q + 1) * rp, cols] = jnp.dot(
                _unpack_rows(u_ref[q * rp:(q + 1) * rp, :]), _unpack_rows(hnt_ref[:, cols]),
                preferred_element_type=F32)

        def v_piece(piece, q):
            cols = slice(piece * MXU_COLS, (piece + 1) * MXU_COLS)
            rp = vt_ref.shape[0] // ROW_SPLIT
            yt_ref[2 * q * rp:2 * (q + 1) * rp, cols] += jnp.dot(
                _unpack_rows(vt_ref[q * rp:(q + 1) * rp, :]), _unpack_rows(act_old[:, cols]),
                preferred_element_type=F32)

        mxu_work = [functools.partial(f, p, q) for p in range(tt // MXU_COLS)
                    for f in (u_piece, v_piece) for q in range(ROW_SPLIT)]
        vpu_work = [functools.partial(gate_block, c, k0, ht_old, act_new)
                    for c in range(tt // LANES) for k0 in range(0, keys_per_chunk, KEYS_PER_PASS)]
        done_m = done_v = 0
        while done_m < len(mxu_work) or done_v < len(vpu_work):
            behind = done_m * len(vpu_work) < done_v * len(mxu_work)
            if (behind and done_m < len(mxu_work)) or done_v == len(vpu_work):
                mxu_work[done_m]()
                done_m += 1
            else:
                vpu_work[done_v]()
                done_v += 1

    @pl.when(g % 2 == 0)
    def _():
        step(ht0_ref, ht1_ref, act1_ref, act0_ref)

    @pl.when(g % 2 == 1)
    def _():
        step(ht1_ref, ht0_ref, act0_ref, act1_ref)

    @pl.when(last_of_tile)
    def _():
        y_ref[...] = res_ref[...] + yt_ref[...].T


def _peer_main(hnt, rk2, b, cnt, a, u_packed, vt_packed, h2, tt, chunk):
    n, d = h2.shape
    n_chunks = 2 * u_packed.shape[0] // chunk
    n_items = (n // tt) * n_chunks
    kpc = chunk // PEER_N_KEYS
    assert kpc % KEYS_PER_PASS == 0 and chunk % (2 * ROW_SPLIT) == 0 and d % (2 * ROW_SPLIT) == 0
    item = lambda g, lag: jnp.clip(g - lag, 0, n_items - 1)
    tile = lambda g, lag: item(g, lag) // n_chunks
    chunk_of = lambda g, lag: item(g, lag) % n_chunks
    state = lambda arr: pl.BlockSpec((None,) + arr.shape[1:], lambda g: (tile(g, 1), 0, 0, 0))
    return pl.pallas_call(
        functools.partial(_peer_main_kernel, keys_per_chunk=kpc, tt=tt, n_chunks=n_chunks,
                          n_items=n_items),
        grid=(n_items + 2,),
        in_specs=[pl.BlockSpec((d // 2, tt), lambda g: (0, tile(g, 0))), state(rk2), state(b),
                  state(cnt), state(a),
                  pl.BlockSpec((chunk // 2, d), lambda g: (chunk_of(g, 0), 0)),
                  pl.BlockSpec((d // 2, chunk), lambda g: (0, chunk_of(g, 2))),
                  pl.BlockSpec((tt, d), lambda g: (tile(g, 2), 0))],
        out_specs=pl.BlockSpec((tt, d), lambda g: (tile(g, 2), 0)),
        out_shape=jax.ShapeDtypeStruct((n, d), F32),
        scratch_shapes=[pltpu.VMEM((d, tt), F32),
                        pltpu.VMEM((chunk, tt), F32),
                        pltpu.VMEM((chunk, tt), F32),
                        pltpu.VMEM((chunk // 2, tt), jnp.uint32),
                        pltpu.VMEM((chunk // 2, tt), jnp.uint32),
                        pltpu.VMEM((kpc, 2 * PEER_HEADS, tt), F32)],
        compiler_params=_params(1),
        name="peer_main",
    )(hnt, rk2, b, cnt, a, u_packed, vt_packed, h2)


def _pack_table_kernel(w_ref, o_ref, *, transpose):
    w = w_ref[...]
    o_ref[...] = _pack_rows((w.T if transpose else w).astype(BF16))


def _pack_table(w, transpose, rows=512):
    r, c = w.shape
    if transpose:
        out_shape, out_spec = (c // 2, r), pl.BlockSpec((c // 2, rows), lambda i: (0, i))
    else:
        out_shape, out_spec = (r // 2, c), pl.BlockSpec((rows // 2, c), lambda i: (i, 0))
    return pl.pallas_call(
        functools.partial(_pack_table_kernel, transpose=transpose),
        grid=(r // rows,),
        in_specs=[pl.BlockSpec((rows, c), lambda i: (i, 0))],
        out_specs=out_spec,
        out_shape=jax.ShapeDtypeStruct(out_shape, jnp.uint32),
        compiler_params=_params(1),
        name="pack_vt" if transpose else "pack_u",
    )(w)


def _layer(x2, batch, seq, w_in, conv_b_glu, conv_dw_w, conv_dw_b, conv_ln_g, conv_ln_b, conv_w_proj,
           conv_b_proj, q_norm_g, k_norm_g, rel_bias, attn_w_proj, mix_w_out, norm1_g, norm2_g,
           peer_w_q, peer_sub_keys, peer_u, peer_v):
    n, d = x2.shape
    row = lambda v: v.reshape(1, -1).astype(F32)
    c0 = 2 * CONV_CH
    cuts = [c0, c0 + ATTN_WIDTH, c0 + 2 * ATTN_WIDTH, c0 + 3 * ATTN_WIDTH]
    wglu, wq, wk, wv, wg = [w.astype(BF16) for w in jnp.split(w_in, cuts, axis=-1)]
    head_of = np.arange(ATTN_WIDTH) // HEAD_DIM
    pm = jnp.asarray((head_of[:, None] == head_of[None, :]) / HEAD_DIM, BF16)
    tile_heads = lambda g: jnp.tile(g.astype(F32), N_ATTN_HEADS).reshape(1, ATTN_WIDTH)

    u, (qs, ks, vs), gates = _inproj(x2, row(norm1_g), wglu, row(conv_b_glu), wq, wk, wv,
                                     tile_heads(q_norm_g), tile_heads(k_norm_g), pm, wg,
                                     tm=512, batch=batch, seq=seq)

    outs, lses = [], []
    for g, (_, dil) in enumerate(ATTN_GROUPS):
        o, lse = _attn_group(qs[g], ks[g], vs[g], _band_bias(rel_bias, g, dil), dil)
        outs.append(o)
        lses.append(lse)

    h2 = _mixer(u, outs, lses, gates, x2, conv_dw_w.reshape(CONV_WIDTH, CONV_CH).astype(F32),
                row(conv_dw_b), row(conv_ln_g), row(conv_ln_b), conv_w_proj.astype(BF16),
                row(conv_b_proj), attn_w_proj.astype(BF16), mix_w_out.astype(BF16),
                tm=min(512, seq), seq=seq)

    tt = min(512, n)
    hnt, rk2, b, cnt, a = _peer_prep(h2, row(norm2_g), peer_w_q.T.astype(BF16),
                                     peer_sub_keys.astype(BF16), tt)
    return _peer_main(hnt, rk2, b, cnt, a, _pack_table(peer_u, transpose=False),
                      _pack_table(peer_v, transpose=True), h2, tt, chunk=1024)


def kernel(x, w_in, conv_b_glu, conv_dw_w, conv_dw_b, conv_ln_g, conv_ln_b, conv_w_proj, conv_b_proj, q_norm_g, k_norm_g, rel_bias, attn_w_proj, mix_w_out, norm1_g, norm2_g, peer_w_q, peer_sub_keys, peer_u, peer_v):
    batch, seq, d = x.shape
    x2 = x.reshape(batch * seq, d)
    for l in range(w_in.shape[0]):
        x2 = _layer(x2, batch, seq, w_in[l], conv_b_glu[l], conv_dw_w[l], conv_dw_b[l], conv_ln_g[l],
                    conv_ln_b[l], conv_w_proj[l], conv_b_proj[l], q_norm_g[l], k_norm_g[l], rel_bias,
                    attn_w_proj[l], mix_w_out[l], norm1_g[l], norm2_g[l], peer_w_q[l],
                    peer_sub_keys[l], peer_u[l], peer_v[l])
    return x2.reshape(batch, seq, d)
```

```python
import functools
import math

import numpy as np
import jax
import jax.numpy as jnp
from jax import lax
from jax.experimental import pallas as pl
from jax.experimental.pallas import tpu as pltpu

F32 = jnp.float32
BF16 = jnp.bfloat16

HEAD_DIM = 64
HEADS_PER_GROUP = 4
ATTN_GROUPS = ((128, 1), (512, 4), (2048, 16))
N_ATTN_HEADS = HEADS_PER_GROUP * len(ATTN_GROUPS)
ATTN_WIDTH = N_ATTN_HEADS * HEAD_DIM
GROUP_WIDTH = HEADS_PER_GROUP * HEAD_DIM
KEY_STEPS = 128
NEG_INF = -1e30
CONV_CH = 512
CONV_WIDTH = 31
CONV_HALO = 32
REL_BUCKETS = 32
REL_MAX_DISTANCE = 2048
PEER_HEADS = 8
PEER_N_KEYS = 128
PEER_TOPK = 16
EPS = 1e-6

LANES = 128
SUBLANES = 8
CONV_ROW_CHUNK = 32
MXU_COLS = 256
ROW_SPLIT = 4
KEYS_PER_PASS = 4
VMEM_LIMIT = 56 * 1024 * 1024

_NT = (((1,), (1,)), ((), ()))


def _params(n_axes):
    return pltpu.CompilerParams(dimension_semantics=("arbitrary",) * n_axes,
                                vmem_limit_bytes=VMEM_LIMIT)


def _full(shape):
    n = len(shape)
    return pl.BlockSpec(shape, lambda *_: (0,) * n)


def _inproj_kernel(x_ref, g1_ref, wglu_ref, bglu_ref, wq_ref, wk_ref, wv_ref, qg_ref, kg_ref,
                   pm_ref, wg_ref, u_ref, *rest, tm):
    qkv_refs, gate_ref, stage = rest[:9], rest[9], rest[10]
    x = x_ref[...]
    ms = jnp.mean(x * x, axis=-1, keepdims=True)
    xn = (x * lax.rsqrt(ms + EPS) * g1_ref[...]).astype(BF16)

    glu = jnp.dot(xn, wglu_ref[...], preferred_element_type=F32) + bglu_ref[...]
    u_ref[...] = glu[:, :CONV_CH] * jax.nn.sigmoid(glu[:, CONV_CH:])

    def head_rmsnorm(w_ref, g_ref):
        y = jnp.dot(xn, w_ref[...], preferred_element_type=F32)
        msq = jnp.dot((y * y).astype(BF16), pm_ref[...], preferred_element_type=F32)
        return y * lax.rsqrt(msq + EPS) * g_ref[...]

    tensors = (head_rmsnorm(wq_ref, qg_ref) * (HEAD_DIM ** -0.5),
               head_rmsnorm(wk_ref, kg_ref),
               jnp.dot(xn, wv_ref[...], preferred_element_type=F32))
    for t, y in enumerate(tensors):
        for g, (_, dil) in enumerate(ATTN_GROUPS):
            out_ref = qkv_refs[t * len(ATTN_GROUPS) + g]
            yg = y[:, g * GROUP_WIDTH:(g + 1) * GROUP_WIDTH]
            if dil == 1:
                out_ref[0] = yg.astype(BF16)
                continue
            for s in range(GROUP_WIDTH // LANES):
                stage[s] = yg[:, s * LANES:(s + 1) * LANES]
            for r in range(dil):
                for s in range(GROUP_WIDTH // LANES):
                    out_ref[r, :, s * LANES:(s + 1) * LANES] = (
                        stage[s, pl.ds(r, tm // dil, stride=dil), :].astype(BF16))
    gate_ref[...] = jax.nn.sigmoid(
        jnp.dot(xn, wg_ref[...], preferred_element_type=F32)).astype(BF16)


def _inproj(x2, g1, wglu, bglu, wq, wk, wv, qg, kg, pm, wg, tm, batch, seq):
    n, d = x2.shape
    tiles_per_seq = seq // tm
    row = lambda w: pl.BlockSpec((tm, w), lambda i: (i, 0))
    grouped = lambda dil: pl.BlockSpec((None, dil, tm // dil, GROUP_WIDTH),
                                       lambda i: (i // tiles_per_seq, 0, i % tiles_per_seq, 0))
    grouped_shape = lambda dil: jax.ShapeDtypeStruct((batch, dil, seq // dil, GROUP_WIDTH), BF16)
    dils = [dil for _ in range(3) for _, dil in ATTN_GROUPS]
    outs = pl.pallas_call(
        functools.partial(_inproj_kernel, tm=tm),
        grid=(n // tm,),
        in_specs=[row(d), _full(g1.shape), _full(wglu.shape), _full(bglu.shape), _full(wq.shape),
                  _full(wk.shape), _full(wv.shape), _full(qg.shape), _full(kg.shape),
                  _full(pm.shape), _full(wg.shape)],
        out_specs=[row(CONV_CH)] + [grouped(dil) for dil in dils] + [row(2 * d)],
        out_shape=[jax.ShapeDtypeStruct((n, CONV_CH), F32)] + [grouped_shape(dil) for dil in dils]
                  + [jax.ShapeDtypeStruct((n, 2 * d), BF16)],
        scratch_shapes=[pltpu.VMEM((GROUP_WIDTH // LANES, tm, LANES), F32)],
        compiler_params=_params(1),
        name="inproj",
    )(x2, g1, wglu, bglu, wq, wk, wv, qg, kg, pm, wg)
    qkv = [list(outs[1 + 3 * t:4 + 3 * t]) for t in range(3)]
    return outs[0], qkv, outs[10]


def _attn_kernel(q_ref, kc_ref, kp_ref, vc_ref, vp_ref, bias_ref, o_ref, lse_ref, kbuf, vbuf,
                 *, nsub):
    first_blk = (pl.program_id(2) == 0).astype(jnp.int32)
    kbuf[0:KEY_STEPS, :] = kp_ref[...]
    kbuf[KEY_STEPS:, :] = kc_ref[...]
    vbuf[0:KEY_STEPS, :] = vp_ref[...]
    vbuf[KEY_STEPS:, :] = vc_ref[...]
    lane = lax.broadcasted_iota(jnp.int32, (1, LANES), 1)
    lo = lane < HEAD_DIM
    head_mask = (lo.astype(BF16), (~lo).astype(BF16))
    for pair in range(GROUP_WIDTH // LANES):
        cols = slice(pair * LANES, (pair + 1) * LANES)
        for sub in range(nsub):
            r0 = sub * KEY_STEPS
            qs = q_ref[r0:r0 + KEY_STEPS, cols]
            kw = kbuf[r0:r0 + 2 * KEY_STEPS, cols]
            vw = vbuf[r0:r0 + 2 * KEY_STEPS, cols]
            outs, lses = [], []
            for hh in range(2):
                s = lax.dot_general(qs * head_mask[hh], kw, _NT, preferred_element_type=F32)
                variant = first_blk if sub == 0 else 0
                s = s + bias_ref[variant, pair * 2 + hh]
                m = jnp.max(s, axis=1, keepdims=True)
                p = jnp.exp(s - m)
                l = jnp.sum(p, axis=1, keepdims=True)
                pv = jnp.dot(p.astype(BF16), vw, preferred_element_type=F32)
                outs.append(pv / l)
                lses.append(m + jnp.log(l))
            o_ref[r0:r0 + KEY_STEPS, cols] = jnp.where(lo, outs[0], outs[1]).astype(BF16)
            lse_ref[r0:r0 + KEY_STEPS, cols] = jnp.where(lo, lses[0], lses[1])


def _attn_group(q, k, v, bias, dil):
    batch, _, sub_len, _ = q.shape
    tq = min(1024, sub_len)
    nsub = tq // KEY_STEPS
    cur = pl.BlockSpec((None, None, tq, GROUP_WIDTH), lambda b, r, i: (b, r, i, 0))
    prev = pl.BlockSpec((None, None, KEY_STEPS, GROUP_WIDTH),
                        lambda b, r, i: (b, r, jnp.maximum(i * nsub - 1, 0), 0))
    return pl.pallas_call(
        functools.partial(_attn_kernel, nsub=nsub),
        grid=(batch, dil, sub_len // tq),
        in_specs=[cur, cur, prev, cur, prev, _full(bias.shape)],
        out_specs=[cur, cur],
        out_shape=[jax.ShapeDtypeStruct(q.shape, BF16), jax.ShapeDtypeStruct(q.shape, F32)],
        scratch_shapes=[pltpu.VMEM((tq + KEY_STEPS, GROUP_WIDTH), BF16),
                        pltpu.VMEM((tq + KEY_STEPS, GROUP_WIDTH), BF16)],
        compiler_params=_params(3),
        name=f"attn_dil{dil}",
    )(q, k, k, v, v, bias)


def _t5_causal_bucket(distance):
    n = distance.astype(jnp.int32)
    max_exact = REL_BUCKETS // 2
    nf = jnp.maximum(n, 1).astype(F32)
    large = max_exact + (jnp.log(nf / max_exact) / math.log(REL_MAX_DISTANCE / max_exact)
                         * (REL_BUCKETS - max_exact)).astype(jnp.int32)
    large = jnp.minimum(large, REL_BUCKETS - 1)
    return jnp.where(n < max_exact, n, large)


def _band_bias(rel_bias, g, dil):
    steps = jnp.arange(KEY_STEPS + 1)
    per_step = rel_bias[_t5_causal_bucket(steps * dil)][:, g * HEADS_PER_GROUP:(g + 1) * HEADS_PER_GROUP].T
    k = KEY_STEPS
    pad = lambda w: jnp.full((HEADS_PER_GROUP, w), NEG_INF, F32)
    f = jnp.concatenate([pad(k - 1), per_step[:, ::-1].astype(F32), pad(k)], axis=1)
    width = f.shape[1]
    normal = jnp.tile(f, (1, k))[:, :k * (width - 1)].reshape(HEADS_PER_GROUP, k, width - 1)
    normal = normal[:, :, k - 1:k - 1 + 2 * k]
    before_start = np.arange(2 * k)[None, None, :] < k
    at_start = jnp.where(before_start, NEG_INF, normal)
    return jnp.stack([normal, at_start])


def _mixer_kernel(u_ref, halo_ref, o0_ref, o1_ref, o2_ref, l0_ref, l1_ref, l2_ref, gate_ref, x_ref,
                  dw_ref, dwb_ref, lng_ref, lnb_ref, wc_ref, bc_ref, wa_ref, wo_ref,
                  h_ref, uext, order_ref, shift_ref, conv_ref, *, tm, tiles_per_seq):
    d = x_ref.shape[1]

    def token_order(ref, dil, slot):
        if dil == 1:
            return ref[0].astype(F32)
        for r in range(dil):
            blk = ref[r].astype(F32)
            for s in range(GROUP_WIDTH // LANES):
                order_ref[slot, s, pl.ds(r, tm // dil, stride=dil), :] = blk[:, s * LANES:(s + 1) * LANES]
        return jnp.concatenate([order_ref[slot, s] for s in range(GROUP_WIDTH // LANES)], axis=1)

    at_seq_start = pl.program_id(0) % tiles_per_seq == 0
    uext[0:CONV_HALO, :] = jnp.where(at_seq_start, 0.0, halo_ref[...])
    uext[CONV_HALO:, :] = u_ref[...]
    shift_rows = shift_ref.shape[1]
    for b in range(1, SUBLANES):
        shift_ref[b - 1] = uext[b:b + shift_rows, :]

    def conv_rows(rc, carry):
        base = pl.multiple_of(rc * CONV_ROW_CHUNK, CONV_ROW_CHUNK)
        acc = jnp.broadcast_to(dwb_ref[...], (CONV_ROW_CHUNK, CONV_CH))
        for w in range(CONV_WIDTH):
            a, b = divmod(CONV_HALO - (CONV_WIDTH - 1) + w, SUBLANES)
            rows = pl.ds(base + a * SUBLANES, CONV_ROW_CHUNK)
            tap = uext[rows, :] if b == 0 else shift_ref[b - 1, rows, :]
            acc = acc + tap * dw_ref[w:w + 1, :]
        conv_ref[pl.ds(base, CONV_ROW_CHUNK), :] = acc
        return carry

    lax.fori_loop(0, tm // CONV_ROW_CHUNK, conv_rows, 0)

    dils = [dil for _, dil in ATTN_GROUPS]
    lses = [token_order(ref, dil, 2 * g) for g, (ref, dil) in enumerate(zip((l0_ref, l1_ref, l2_ref), dils))]
    outs = [token_order(ref, dil, 2 * g + 1) for g, (ref, dil) in enumerate(zip((o0_ref, o1_ref, o2_ref), dils))]

    for hf in range(2):
        rows = slice(hf * tm // 2, (hf + 1) * tm // 2)
        c = conv_ref[rows, :]
        mu = jnp.mean(c, axis=-1, keepdims=True)
        cc = c - mu
        var = jnp.mean(cc * cc, axis=-1, keepdims=True)
        c = cc * lax.rsqrt(var + EPS) * lng_ref[...] + lnb_ref[...]
        c = c * jax.nn.sigmoid(c)
        conv_out = jnp.dot(c.astype(BF16), wc_ref[...], preferred_element_type=F32) + bc_ref[...]

        lse_h = [l[rows] for l in lses]
        m = jnp.maximum(jnp.maximum(lse_h[0], lse_h[1]), lse_h[2])
        es = [jnp.exp(l - m) for l in lse_h]
        inv = 1.0 / (es[0] + es[1] + es[2])
        attn_out = jnp.zeros((tm // 2, d), F32)
        for g in range(3):
            og = (outs[g][rows] * (es[g] * inv)).astype(BF16)
            attn_out = attn_out + jnp.dot(og, wa_ref[g * GROUP_WIDTH:(g + 1) * GROUP_WIDTH, :],
                                          preferred_element_type=F32)

        merged = (gate_ref[rows, :d].astype(F32) * conv_out
                  + gate_ref[rows, d:].astype(F32) * attn_out)
        h_ref[rows, :] = x_ref[rows, :] + jnp.dot(merged.astype(BF16), wo_ref[...],
                                                  preferred_element_type=F32)


def _mixer(u, outs, lses, gates, x2, dw, dwb, lng, lnb, wc, bc, wa, wo, tm, seq):
    n, d = x2.shape
    row = lambda w: pl.BlockSpec((tm, w), lambda i: (i, 0))
    halo = pl.BlockSpec((CONV_HALO, CONV_CH),
                        lambda i: (jnp.maximum(i * (tm // CONV_HALO) - 1, 0), 0))
    tiles_per_seq = seq // tm
    grouped = lambda dil: pl.BlockSpec((None, dil, tm // dil, GROUP_WIDTH),
                                       lambda i: (i // tiles_per_seq, 0, i % tiles_per_seq, 0))
    group_specs = [grouped(dil) for _, dil in ATTN_GROUPS]
    return pl.pallas_call(
        functools.partial(_mixer_kernel, tm=tm, tiles_per_seq=tiles_per_seq),
        grid=(n // tm,),
        in_specs=[row(CONV_CH), halo] + group_specs * 2 + [row(2 * d), row(d)]
                 + [_full(a.shape) for a in (dw, dwb, lng, lnb, wc, bc, wa, wo)],
        out_specs=row(d),
        out_shape=jax.ShapeDtypeStruct((n, d), F32),
        scratch_shapes=[pltpu.VMEM((tm + CONV_HALO, CONV_CH), F32),
                        pltpu.VMEM((2 * len(ATTN_GROUPS), GROUP_WIDTH // LANES, tm, LANES), F32),
                        pltpu.VMEM((SUBLANES - 1, tm + CONV_HALO - SUBLANES, CONV_CH), F32),
                        pltpu.VMEM((tm, CONV_CH), F32)],
        compiler_params=_params(1),
        name="mixer_out",
    )(u, u, *outs, *lses, gates, x2, dw, dwb, lng, lnb, wc, bc, wa, wo)


_CAND_WIDTHS = tuple(PEER_TOPK // (r1 + 1) for r1 in range(PEER_TOPK))
_CAND_BLOCK_ROWS = 8
_UNRANKED = float(PEER_N_KEYS)


def _pack_rows(x):
    return pltpu.bitcast(x, jnp.uint32)


def _unpack_rows(x):
    return pltpu.bitcast(x, BF16)


def _col_reduce(x, op, reduce_fn):
    parts = [x[i:i + 8] for i in range(0, x.shape[0], 8)]
    while len(parts) > 1:
        nxt = [op(parts[i], parts[i + 1]) for i in range(0, len(parts) - 1, 2)]
        parts = nxt + ([parts[-1]] if len(parts) % 2 else [])
    return reduce_fn(parts[0], axis=0, keepdims=True)


def _col_max(x):
    return _col_reduce(x, jnp.maximum, jnp.max)


def _col_min(x):
    return _col_reduce(x, jnp.minimum, jnp.min)


def _col_sum(x):
    return _col_reduce(x, jnp.add, jnp.sum)


def _extract_top(s, key_id, on_round):
    for r in range(PEER_TOPK):
        m = _col_max(s)
        idx = _col_min(jnp.where(s == m, key_id, 1e9))
        sel = key_id == idx
        s = jnp.where(sel, -jnp.inf, s)
        on_round(r, m, idx, sel)


def _rank_keys(s, key_id, iota16):
    state = [jnp.zeros(iota16.shape, F32), jnp.full(s.shape, _UNRANKED, F32)]

    def on_round(r, m, idx, sel):
        state[0] = jnp.where(iota16 == float(r), m, state[0])
        state[1] = jnp.where(sel, float(r), state[1])

    _extract_top(s, key_id, on_round)
    return state[0], state[1]


_MARK_SCALE = 2.0 ** 100
_CAND_MARK = -_MARK_SCALE


def _top_values_unique(scores, iota16):
    scores = list(scores)
    tops = [jnp.zeros(iota16.shape, F32) for _ in scores]
    for r in range(PEER_TOPK):
        for i, s in enumerate(scores):
            m = _col_max(s)
            scores[i] = jnp.where(s == m, -_MARK_SCALE * (1.0 + r / 32.0), s)
            tops[i] = jnp.where(iota16 == float(r), m, tops[i])
    return scores, tops


def _decode_marks(s):
    marked = s < -0.5 * _MARK_SCALE
    rank = jnp.where(marked, (s * (-1.0 / _MARK_SCALE) - 1.0) * 32.0, _UNRANKED)
    return rank, _col_sum(marked.astype(F32))


def _candidate_sums(ss1, ss2, iota8):
    blocks = [ss1[0:1, :] + ss2]
    for r1 in range(1, _CAND_BLOCK_ROWS):
        blk = ss1[r1:r1 + 1, :] + ss2[0:_CAND_BLOCK_ROWS, :]
        blocks.append(jnp.where(iota8 < float(_CAND_WIDTHS[r1]), blk, -jnp.inf))
    blocks.append(ss1[_CAND_BLOCK_ROWS:, :] + ss2[0:1, :])
    return jnp.concatenate(blocks, axis=0)


def _peer_prep_kernel(h_ref, g2_ref, wqt_ref, keys_ref,
                      hnt_ref, rk2_ref, b_ref, cnt_ref, a_ref,
                      qt_ref, s1_ref, s2_ref, tk_ref, *, tt):
    h = h_ref[...]
    ms = jnp.mean(h * h, axis=-1, keepdims=True)
    hnt = (h * lax.rsqrt(ms + EPS) * g2_ref[...]).T.astype(BF16)
    hnt_ref[...] = _pack_rows(hnt)
    qt_ref[...] = jnp.dot(wqt_ref[...], hnt, preferred_element_type=F32)

    key_id = lax.broadcasted_iota(jnp.int32, (PEER_N_KEYS, LANES), 0).astype(F32)
    iota16 = lax.broadcasted_iota(jnp.int32, (PEER_TOPK, LANES), 0).astype(F32)
    iota8 = lax.broadcasted_iota(jnp.int32, (_CAND_BLOCK_ROWS, LANES), 0).astype(F32)
    cand_id = jnp.concatenate(
        [iota16]
        + [iota8 + float(r1 * PEER_TOPK) for r1 in range(1, _CAND_BLOCK_ROWS)]
        + [(iota8 + float(_CAND_BLOCK_ROWS)) * float(PEER_TOPK)], axis=0)

    def head_body(hd, carry):
        for p, dst_ref in ((0, s1_ref), (1, s2_ref)):
            base = pl.multiple_of((hd * 2 + p) * PEER_N_KEYS, PEER_N_KEYS)
            qhp = qt_ref[pl.ds(base, PEER_N_KEYS), :].astype(BF16)
            dst_ref[...] = jnp.dot(keys_ref[hd, p], qhp, preferred_element_type=F32)

        def lane_group_pair(cp, carry2):
            lane_sets = [pl.ds(pl.multiple_of((2 * cp + j) * LANES, LANES), LANES) for j in range(2)]
            k = PEER_TOPK
            row_ss1, row_ss2, row_best, row_cnt = (slice(i * k, (i + 1) * k) for i in range(4))
            row_rk1 = slice(4 * k, 4 * k + PEER_N_KEYS)
            row_rk2 = slice(4 * k + PEER_N_KEYS, 4 * k + 2 * PEER_N_KEYS)

            tops, ties = {}, []
            for half, (s_ref, row_ss, row_rk) in enumerate(((s1_ref, row_ss1, row_rk1),
                                                            (s2_ref, row_ss2, row_rk2))):
                marked, tops[half] = _top_values_unique([s_ref[:, l] for l in lane_sets], iota16)
                for j in range(2):
                    rk, n = _decode_marks(marked[j])
                    tk_ref[j, row_ss, :] = tops[half][j]
                    tk_ref[j, row_rk, :] = rk
                    ties.append(jnp.abs(n - k))
            cands = [_candidate_sums(tops[0][j], tops[1][j], iota8) for j in range(2)]
            bests = [jnp.zeros((k, LANES), F32) for _ in range(2)]
            for r in range(k):
                for j in range(2):
                    m = _col_max(cands[j])
                    cands[j] = jnp.where(cands[j] == m, _CAND_MARK, cands[j])
                    bests[j] = jnp.where(iota16 == float(r), m, bests[j])
            for j in range(2):
                picked = (cands[j] == _CAND_MARK).astype(F32)
                per_row = [jnp.sum(picked[0:k, :], axis=0, keepdims=True)]
                for r1 in range(1, _CAND_BLOCK_ROWS):
                    lo = k + (r1 - 1) * _CAND_BLOCK_ROWS
                    per_row.append(jnp.sum(picked[lo:lo + _CAND_BLOCK_ROWS, :], axis=0, keepdims=True))
                cnt16 = jnp.concatenate(per_row + [picked[k + 7 * _CAND_BLOCK_ROWS:, :]], axis=0)
                ties.append(jnp.abs(jnp.sum(cnt16, axis=0, keepdims=True) - k))
                tk_ref[j, row_best, :] = bests[j]
                tk_ref[j, row_cnt, :] = cnt16
            had_tie = jnp.max(functools.reduce(jnp.add, ties)) > 0.5

            @pl.when(had_tie)
            def _():
                for j in range(2):
                    ss1x, rk1x = _rank_keys(s1_ref[:, lane_sets[j]], key_id, iota16)
                    ss2x, rk2x = _rank_keys(s2_ref[:, lane_sets[j]], key_id, iota16)
                    state = [jnp.zeros((k, LANES), F32), jnp.zeros((k, LANES), F32)]

                    def on_round(r, m, idx, sel):
                        state[0] = jnp.where(iota16 == float(r), m, state[0])
                        row = jnp.floor(idx * (1.0 / k))
                        state[1] = state[1] + (iota16 == row).astype(F32)

                    _extract_top(_candidate_sums(ss1x, ss2x, iota8), cand_id, on_round)
                    tk_ref[j, row_ss1, :] = ss1x
                    tk_ref[j, row_ss2, :] = ss2x
                    tk_ref[j, row_best, :] = state[0]
                    tk_ref[j, row_cnt, :] = state[1]
                    tk_ref[j, row_rk1, :] = rk1x
                    tk_ref[j, row_rk2, :] = rk2x

            for j, lanes in enumerate(lane_sets):
                ss1, ss2, best, cnt16 = (tk_ref[j, rows, :]
                                         for rows in (row_ss1, row_ss2, row_best, row_cnt))
                rk1 = tk_ref[j, row_rk1, :]
                rk2 = tk_ref[j, row_rk2, :]
                z = jnp.sum(jnp.exp(best - best[0:1, :]), axis=0, keepdims=True)

                a_ref[hd, :, lanes] = jnp.exp(s1_ref[:, lanes] - ss1[0:1, :])
                b_ref[hd, :, lanes] = _pack_rows((jnp.exp(s2_ref[:, lanes] - ss2[0:1, :]) / z).astype(BF16))
                rk2_ref[hd, :, lanes] = _pack_rows(rk2.astype(BF16))
                cnt = jnp.zeros((PEER_N_KEYS, LANES), F32)
                for r1 in range(PEER_TOPK):
                    cnt = jnp.where(rk1 == float(r1), cnt16[r1:r1 + 1, :], cnt)
                cnt_ref[hd, :, lanes] = cnt
            return carry2

        lax.fori_loop(0, tt // (2 * LANES), lane_group_pair, 0)
        return carry

    lax.fori_loop(0, PEER_HEADS, head_body, 0)


def _peer_prep(h2, g2, wqt, keys, tt):
    n, d = h2.shape
    nt = n // tt
    state = lambda rows: pl.BlockSpec((None, PEER_HEADS, rows, tt), lambda i: (i, 0, 0, 0))
    shape = lambda rows, dt: jax.ShapeDtypeStruct((nt, PEER_HEADS, rows, tt), dt)
    half = PEER_N_KEYS // 2
    return pl.pallas_call(
        functools.partial(_peer_prep_kernel, tt=tt),
        grid=(nt,),
        in_specs=[pl.BlockSpec((tt, d), lambda i: (i, 0)), _full(g2.shape), _full(wqt.shape),
                  _full(keys.shape)],
        out_specs=[pl.BlockSpec((d // 2, tt), lambda i: (0, i)), state(half), state(half),
                   state(PEER_N_KEYS), state(PEER_N_KEYS)],
        out_shape=[jax.ShapeDtypeStruct((d // 2, n), jnp.uint32), shape(half, jnp.uint32),
                   shape(half, jnp.uint32), shape(PEER_N_KEYS, F32), shape(PEER_N_KEYS, F32)],
        scratch_shapes=[pltpu.VMEM((wqt.shape[0], tt), F32),
                        pltpu.VMEM((PEER_N_KEYS, tt), F32),
                        pltpu.VMEM((PEER_N_KEYS, tt), F32),
                        pltpu.VMEM((2, 4 * PEER_TOPK + 2 * PEER_N_KEYS, LANES), F32)],
        compiler_params=_params(1),
        name="peer_prep",
    )(h2, g2, wqt, keys)


def _peer_main_kernel(hnt_ref, rk2_ref, b_ref, cnt_ref, a_ref, u_ref, vt_ref, res_ref,
                      y_ref, yt_ref, ht0_ref, ht1_ref, act0_ref, act1_ref, rows_ref,
                      *, keys_per_chunk, tt, n_chunks, n_items):
    g = pl.program_id(0)
    item3 = g - 2
    first_of_tile = (item3 >= 0) & (item3 % n_chunks == 0)
    last_of_tile = (item3 >= 0) & (item3 % n_chunks == n_chunks - 1)

    @pl.when(g == 0)
    def _():
        for ref in (ht0_ref, ht1_ref):
            ref[...] = jnp.zeros(ref.shape, F32)
        for ref in (act0_ref, act1_ref):
            ref[...] = jnp.zeros(ref.shape, jnp.uint32)

    @pl.when((g == 0) | first_of_tile)
    def _():
        yt_ref[...] = jnp.zeros(yt_ref.shape, F32)

    half = PEER_N_KEYS // 2
    chunk2 = jnp.clip(g - 1, 0, n_items - 1) % n_chunks

    def gate_block(c, k0, ht_old, act_new):
        lanes = slice(c * LANES, (c + 1) * LANES)
        ks = range(k0, k0 + KEYS_PER_PASS)
        gates = {k: jnp.zeros((PEER_N_KEYS, LANES), BF16) for k in ks}
        for hd in range(PEER_HEADS):
            rk2 = _unpack_rows(rk2_ref[hd, :, lanes])
            b = _unpack_rows(b_ref[hd, :, lanes])
            for k in ks:
                cnt_row = rows_ref[k, hd:hd + 1, lanes].astype(BF16)
                a_row = rows_ref[k, PEER_HEADS + hd:PEER_HEADS + hd + 1, lanes].astype(BF16)
                gates[k] = gates[k] + jnp.where(rk2 < cnt_row, b * a_row, 0.0)
        for k in ks:
            x = ht_old[k * PEER_N_KEYS:(k + 1) * PEER_N_KEYS, lanes]
            gelu = 0.5 * x * (1.0 + lax.erf(x * math.sqrt(0.5)))
            act_new[k * half:(k + 1) * half, lanes] = _pack_rows(gelu.astype(BF16) * gates[k])

    def step(ht_new, ht_old, act_new, act_old):
        for k in range(keys_per_chunk):
            i1 = chunk2 * keys_per_chunk + k
            for hd in range(PEER_HEADS):
                rows_ref[k, hd:hd + 1, :] = cnt_ref[hd, pl.ds(i1, 1), :]
                rows_ref[k, PEER_HEADS + hd:PEER_HEADS + hd + 1, :] = a_ref[hd, pl.ds(i1, 1), :]

        def u_piece(piece, q):
            cols = slice(piece * MXU_COLS, (piece + 1) * MXU_COLS)
            rp = u_ref.shape[0] // ROW_SPLIT
            ht_new[2 * q * rp:2 * (q + 1) * rp, cols] = jnp.dot(
                _unpack_rows(u_ref[q * rp:(q + 1) * rp, :]), _unpack_rows(hnt_ref[:, cols]),
                preferred_element_type=F32)

        def v_piece(piece, q):
            cols = slice(piece * MXU_COLS, (piece + 1) * MXU_COLS)
            rp = vt_ref.shape[0] // ROW_SPLIT
            yt_ref[2 * q * rp:2 * (q + 1) * rp, cols] += jnp.dot(
                _unpack_rows(vt_ref[q * rp:(q + 1) * rp, :]), _unpack_rows(act_old[:, cols]),
                preferred_element_type=F32)

        mxu_work = [functools.partial(f, p, q) for p in range(tt // MXU_COLS)
                    for f in (u_piece, v_piece) for q in range(ROW_SPLIT)]
        vpu_work = [functools.partial(gate_block, c, k0, ht_old, act_new)
                    for c in range(tt // LANES) for k0 in range(0, keys_per_chunk, KEYS_PER_PASS)]
        done_m = done_v = 0
        while done_m < len(mxu_work) or done_v < len(vpu_work):
            behind = done_m * len(vpu_work) < done_v * len(mxu_work)
            if (behind and done_m < len(mxu_work)) or done_v == len(vpu_work):
                mxu_work[done_m]()
                done_m += 1
            else:
                vpu_work[done_v]()
                done_v += 1

    @pl.when(g % 2 == 0)
    def _():
        step(ht0_ref, ht1_ref, act1_ref, act0_ref)

    @pl.when(g % 2 == 1)
    def _():
        step(ht1_ref, ht0_ref, act0_ref, act1_ref)

    @pl.when(last_of_tile)
    def _():
        y_ref[...] = res_ref[...] + yt_ref[...].T


def _peer_main(hnt, rk2, b, cnt, a, u_packed, vt_packed, h2, tt, chunk):
    n, d = h2.shape
    n_chunks = 2 * u_packed.shape[0] // chunk
    n_items = (n // tt) * n_chunks
    kpc = chunk // PEER_N_KEYS
    assert kpc % KEYS_PER_PASS == 0 and chunk % (2 * ROW_SPLIT) == 0 and d % (2 * ROW_SPLIT) == 0
    item = lambda g, lag: jnp.clip(g - lag, 0, n_items - 1)
    tile = lambda g, lag: item(g, lag) // n_chunks
    chunk_of = lambda g, lag: item(g, lag) % n_chunks
    state = lambda arr: pl.BlockSpec((None,) + arr.shape[1:], lambda g: (tile(g, 1), 0, 0, 0))
    return pl.pallas_call(
        functools.partial(_peer_main_kernel, keys_per_chunk=kpc, tt=tt, n_chunks=n_chunks,
                          n_items=n_items),
        grid=(n_items + 2,),
        in_specs=[pl.BlockSpec((d // 2, tt), lambda g: (0, tile(g, 0))), state(rk2), state(b),
                  state(cnt), state(a),
                  pl.BlockSpec((chunk // 2, d), lambda g: (chunk_of(g, 0), 0)),
                  pl.BlockSpec((d // 2, chunk), lambda g: (0, chunk_of(g, 2))),
                  pl.BlockSpec((tt, d), lambda g: (tile(g, 2), 0))],
        out_specs=pl.BlockSpec((tt, d), lambda g: (tile(g, 2), 0)),
        out_shape=jax.ShapeDtypeStruct((n, d), F32),
        scratch_shapes=[pltpu.VMEM((d, tt), F32),
                        pltpu.VMEM((chunk, tt), F32),
                        pltpu.VMEM((chunk, tt), F32),
                        pltpu.VMEM((chunk // 2, tt), jnp.uint32),
                        pltpu.VMEM((chunk // 2, tt), jnp.uint32),
                        pltpu.VMEM((kpc, 2 * PEER_HEADS, tt), F32)],
        compiler_params=_params(1),
        name="peer_main",
    )(hnt, rk2, b, cnt, a, u_packed, vt_packed, h2)


def _pack_table_kernel(w_ref, o_ref, *, transpose):
    w = w_ref[...]
    o_ref[...] = _pack_rows((w.T if transpose else w).astype(BF16))


def _pack_table(w, transpose, rows=512):
    r, c = w.shape
    if transpose:
        out_shape, out_spec = (c // 2, r), pl.BlockSpec((c // 2, rows), lambda i: (0, i))
    else:
        out_shape, out_spec = (r // 2, c), pl.BlockSpec((rows // 2, c), lambda i: (i, 0))
    return pl.pallas_call(
        functools.partial(_pack_table_kernel, transpose=transpose),
        grid=(r // rows,),
        in_specs=[pl.BlockSpec((rows, c), lambda i: (i, 0))],
        out_specs=out_spec,
        out_shape=jax.ShapeDtypeStruct(out_shape, jnp.uint32),
        compiler_params=_params(1),
        name="pack_vt" if transpose else "pack_u",
    )(w)


def _layer(x2, batch, seq, w_in, conv_b_glu, conv_dw_w, conv_dw_b, conv_ln_g, conv_ln_b, conv_w_proj,
           conv_b_proj, q_norm_g, k_norm_g, rel_bias, attn_w_proj, mix_w_out, norm1_g, norm2_g,
           peer_w_q, peer_sub_keys, peer_u, peer_v):
    n, d = x2.shape
    row = lambda v: v.reshape(1, -1).astype(F32)
    c0 = 2 * CONV_CH
    cuts = [c0, c0 + ATTN_WIDTH, c0 + 2 * ATTN_WIDTH, c0 + 3 * ATTN_WIDTH]
    wglu, wq, wk, wv, wg = [w.astype(BF16) for w in jnp.split(w_in, cuts, axis=-1)]
    head_of = np.arange(ATTN_WIDTH) // HEAD_DIM
    pm = jnp.asarray((head_of[:, None] == head_of[None, :]) / HEAD_DIM, BF16)
    tile_heads = lambda g: jnp.tile(g.astype(F32), N_ATTN_HEADS).reshape(1, ATTN_WIDTH)

    u, (qs, ks, vs), gates = _inproj(x2, row(norm1_g), wglu, row(conv_b_glu), wq, wk, wv,
                                     tile_heads(q_norm_g), tile_heads(k_norm_g), pm, wg,
                                     tm=512, batch=batch, seq=seq)

    outs, lses = [], []
    for g, (_, dil) in enumerate(ATTN_GROUPS):
        o, lse = _attn_group(qs[g], ks[g], vs[g], _band_bias(rel_bias, g, dil), dil)
        outs.append(o)
        lses.append(lse)

    h2 = _mixer(u, outs, lses, gates, x2, conv_dw_w.reshape(CONV_WIDTH, CONV_CH).astype(F32),
                row(conv_dw_b), row(conv_ln_g), row(conv_ln_b), conv_w_proj.astype(BF16),
                row(conv_b_proj), attn_w_proj.astype(BF16), mix_w_out.astype(BF16),
                tm=min(512, seq), seq=seq)

    tt = min(512, n)
    hnt, rk2, b, cnt, a = _peer_prep(h2, row(norm2_g), peer_w_q.T.astype(BF16),
                                     peer_sub_keys.astype(BF16), tt)
    return _peer_main(hnt, rk2, b, cnt, a, _pack_table(peer_u, transpose=False),
                      _pack_table(peer_v, transpose=True), h2, tt, chunk=1024)


def kernel(x, w_in, conv_b_glu, conv_dw_w, conv_dw_b, conv_ln_g, conv_ln_b, conv_w_proj, conv_b_proj, q_norm_g, k_norm_g, rel_bias, attn_w_proj, mix_w_out, norm1_g, norm2_g, peer_w_q, peer_sub_keys, peer_u, peer_v):
    batch, seq, d = x.shape
    x2 = x.reshape(batch * seq, d)
    for l in range(w_in.shape[0]):
        x2 = _layer(x2, batch, seq, w_in[l], conv_b_glu[l], conv_dw_w[l], conv_dw_b[l], conv_ln_g[l],
                    conv_ln_b[l], conv_w_proj[l], conv_b_proj[l], q_norm_g[l], k_norm_g[l], rel_bias,
                    attn_w_proj[l], mix_w_out[l], norm1_g[l], norm2_g[l], peer_w_q[l],
                    peer_sub_keys[l], peer_u[l], peer_v[l])
    return x2.reshape(batch, seq, d)
```

```python
import functools
import math

import numpy as np
import jax
import jax.numpy as jnp
from jax import lax
from jax.experimental import pallas as pl
from jax.experimental.pallas import tpu as pltpu

F32 = jnp.float32
BF16 = jnp.bfloat16

HEAD_DIM = 64
HEADS_PER_GROUP = 4
ATTN_GROUPS = ((128, 1), (512, 4), (2048, 16))
N_ATTN_HEADS = HEADS_PER_GROUP * len(ATTN_GROUPS)
ATTN_WIDTH = N_ATTN_HEADS * HEAD_DIM
GROUP_WIDTH = HEADS_PER_GROUP * HEAD_DIM
KEY_STEPS = 128
NEG_INF = -1e30
CONV_CH = 512
CONV_WIDTH = 31
CONV_HALO = 32
REL_BUCKETS = 32
REL_MAX_DISTANCE = 2048
PEER_HEADS = 8
PEER_N_KEYS = 128
PEER_TOPK = 16
EPS = 1e-6

LANES = 128
SUBLANES = 8
CONV_ROW_CHUNK = 32
MXU_COLS = 256
ROW_SPLIT = 4
KEYS_PER_PASS = 4
VMEM_LIMIT = 56 * 1024 * 1024

_NT = (((1,), (1,)), ((), ()))


def _params(n_axes):
    return pltpu.CompilerParams(dimension_semantics=("arbitrary",) * n_axes,
                                vmem_limit_bytes=VMEM_LIMIT)


def _full(shape):
    n = len(shape)
    return pl.BlockSpec(shape, lambda *_: (0,) * n)


def _inproj_kernel(x_ref, g1_ref, wglu_ref, bglu_ref, wq_ref, wk_ref, wv_ref, qg_ref, kg_ref,
                   pm_ref, wg_ref, u_ref, *rest, tm):
    qkv_refs, gate_ref, stage = rest[:9], rest[9], rest[10]
    x = x_ref[...]
    ms = jnp.mean(x * x, axis=-1, keepdims=True)
    xn = (x * lax.rsqrt(ms + EPS) * g1_ref[...]).astype(BF16)

    glu = jnp.dot(xn, wglu_ref[...], preferred_element_type=F32) + bglu_ref[...]
    u_ref[...] = glu[:, :CONV_CH] * jax.nn.sigmoid(glu[:, CONV_CH:])

    def head_rmsnorm(w_ref, g_ref):
        y = jnp.dot(xn, w_ref[...], preferred_element_type=F32)
        msq = jnp.dot((y * y).astype(BF16), pm_ref[...], preferred_element_type=F32)
        return y * lax.rsqrt(msq + EPS) * g_ref[...]

    tensors = (head_rmsnorm(wq_ref, qg_ref) * (HEAD_DIM ** -0.5),
               head_rmsnorm(wk_ref, kg_ref),
               jnp.dot(xn, wv_ref[...], preferred_element_type=F32))
    for t, y in enumerate(tensors):
        for g, (_, dil) in enumerate(ATTN_GROUPS):
            out_ref = qkv_refs[t * len(ATTN_GROUPS) + g]
            yg = y[:, g * GROUP_WIDTH:(g + 1) * GROUP_WIDTH]
            if dil == 1:
                out_ref[0] = yg.astype(BF16)
                continue
            for s in range(GROUP_WIDTH // LANES):
                stage[s] = yg[:, s * LANES:(s + 1) * LANES]
            for r in range(dil):
                for s in range(GROUP_WIDTH // LANES):
                    out_ref[r, :, s * LANES:(s + 1) * LANES] = (
                        stage[s, pl.ds(r, tm // dil, stride=dil), :].astype(BF16))
    gate_ref[...] = jax.nn.sigmoid(
        jnp.dot(xn, wg_ref[...], preferred_element_type=F32)).astype(BF16)


def _inproj(x2, g1, wglu, bglu, wq, wk, wv, qg, kg, pm, wg, tm, batch, seq):
    n, d = x2.shape
    tiles_per_seq = seq // tm
    row = lambda w: pl.BlockSpec((tm, w), lambda i: (i, 0))
    grouped = lambda dil: pl.BlockSpec((None, dil, tm // dil, GROUP_WIDTH),
                                       lambda i: (i // tiles_per_seq, 0, i % tiles_per_seq, 0))
    grouped_shape = lambda dil: jax.ShapeDtypeStruct((batch, dil, seq // dil, GROUP_WIDTH), BF16)
    dils = [dil for _ in range(3) for _, dil in ATTN_GROUPS]
    outs = pl.pallas_call(
        functools.partial(_inproj_kernel, tm=tm),
        grid=(n // tm,),
        in_specs=[row(d), _full(g1.shape), _full(wglu.shape), _full(bglu.shape), _full(wq.shape),
                  _full(wk.shape), _full(wv.shape), _full(qg.shape), _full(kg.shape),
                  _full(pm.shape), _full(wg.shape)],
        out_specs=[row(CONV_CH)] + [grouped(dil) for dil in dils] + [row(2 * d)],
        out_shape=[jax.ShapeDtypeStruct((n, CONV_CH), F32)] + [grouped_shape(dil) for dil in dils]
                  + [jax.ShapeDtypeStruct((n, 2 * d), BF16)],
        scratch_shapes=[pltpu.VMEM((GROUP_WIDTH // LANES, tm, LANES), F32)],
        compiler_params=_params(1),
        name="inproj",
    )(x2, g1, wglu, bglu, wq, wk, wv, qg, kg, pm, wg)
    qkv = [list(outs[1 + 3 * t:4 + 3 * t]) for t in range(3)]
    return outs[0], qkv, outs[10]


def _attn_kernel(q_ref, kc_ref, kp_ref, vc_ref, vp_ref, bias_ref, o_ref, lse_ref, kbuf, vbuf,
                 *, nsub):
    first_blk = (pl.program_id(2) == 0).astype(jnp.int32)
    kbuf[0:KEY_STEPS, :] = kp_ref[...]
    kbuf[KEY_STEPS:, :] = kc_ref[...]
    vbuf[0:KEY_STEPS, :] = vp_ref[...]
    vbuf[KEY_STEPS:, :] = vc_ref[...]
    lane = lax.broadcasted_iota(jnp.int32, (1, LANES), 1)
    lo = lane < HEAD_DIM
    head_mask = (lo.astype(BF16), (~lo).astype(BF16))
    for pair in range(GROUP_WIDTH // LANES):
        cols = slice(pair * LANES, (pair + 1) * LANES)
        for sub in range(nsub):
            r0 = sub * KEY_STEPS
            qs = q_ref[r0:r0 + KEY_STEPS, cols]
            kw = kbuf[r0:r0 + 2 * KEY_STEPS, cols]
            vw = vbuf[r0:r0 + 2 * KEY_STEPS, cols]
            outs, lses = [], []
            for hh in range(2):
                s = lax.dot_general(qs * head_mask[hh], kw, _NT, preferred_element_type=F32)
                variant = first_blk if sub == 0 else 0
                s = s + bias_ref[variant, pair * 2 + hh]
                m = jnp.max(s, axis=1, keepdims=True)
                p = jnp.exp(s - m)
                l = jnp.sum(p, axis=1, keepdims=True)
                pv = jnp.dot(p.astype(BF16), vw, preferred_element_type=F32)
                outs.append(pv / l)
                lses.append(m + jnp.log(l))
            o_ref[r0:r0 + KEY_STEPS, cols] = jnp.where(lo, outs[0], outs[1]).astype(BF16)
            lse_ref[r0:r0 + KEY_STEPS, cols] = jnp.where(lo, lses[0], lses[1])


def _attn_group(q, k, v, bias, dil):
    batch, _, sub_len, _ = q.shape
    tq = min(1024, sub_len)
    nsub = tq // KEY_STEPS
    cur = pl.BlockSpec((None, None, tq, GROUP_WIDTH), lambda b, r, i: (b, r, i, 0))
    prev = pl.BlockSpec((None, None, KEY_STEPS, GROUP_WIDTH),
                        lambda b, r, i: (b, r, jnp.maximum(i * nsub - 1, 0), 0))
    return pl.pallas_call(
        functools.partial(_attn_kernel, nsub=nsub),
        grid=(batch, dil, sub_len // tq),
        in_specs=[cur, cur, prev, cur, prev, _full(bias.shape)],
        out_specs=[cur, cur],
        out_shape=[jax.ShapeDtypeStruct(q.shape, BF16), jax.ShapeDtypeStruct(q.shape, F32)],
        scratch_shapes=[pltpu.VMEM((tq + KEY_STEPS, GROUP_WIDTH), BF16),
                        pltpu.VMEM((tq + KEY_STEPS, GROUP_WIDTH), BF16)],
        compiler_params=_params(3),
        name=f"attn_dil{dil}",
    )(q, k, k, v, v, bias)


def _t5_causal_bucket(distance):
    n = distance.astype(jnp.int32)
    max_exact = REL_BUCKETS // 2
    nf = jnp.maximum(n, 1).astype(F32)
    large = max_exact + (jnp.log(nf / max_exact) / math.log(REL_MAX_DISTANCE / max_exact)
                         * (REL_BUCKETS - max_exact)).astype(jnp.int32)
    large = jnp.minimum(large, REL_BUCKETS - 1)
    return jnp.where(n < max_exact, n, large)


def _band_bias(rel_bias, g, dil):
    steps = jnp.arange(KEY_STEPS + 1)
    per_step = rel_bias[_t5_causal_bucket(steps * dil)][:, g * HEADS_PER_GROUP:(g + 1) * HEADS_PER_GROUP].T
    k = KEY_STEPS
    pad = lambda w: jnp.full((HEADS_PER_GROUP, w), NEG_INF, F32)
    f = jnp.concatenate([pad(k - 1), per_step[:, ::-1].astype(F32), pad(k)], axis=1)
    width = f.shape[1]
    normal = jnp.tile(f, (1, k))[:, :k * (width - 1)].reshape(HEADS_PER_GROUP, k, width - 1)
    normal = normal[:, :, k - 1:k - 1 + 2 * k]
    before_start = np.arange(2 * k)[None, None, :] < k
    at_start = jnp.where(before_start, NEG_INF, normal)
    return jnp.stack([normal, at_start])


def _mixer_kernel(u_ref, halo_ref, o0_ref, o1_ref, o2_ref, l0_ref, l1_ref, l2_ref, gate_ref, x_ref,
                  dw_ref, dwb_ref, lng_ref, lnb_ref, wc_ref, bc_ref, wa_ref, wo_ref,
                  h_ref, uext, order_ref, shift_ref, conv_ref, *, tm, tiles_per_seq):
    d = x_ref.shape[1]

    def token_order(ref, dil, slot):
        if dil == 1:
            return ref[0].astype(F32)
        for r in range(dil):
            blk = ref[r].astype(F32)
            for s in range(GROUP_WIDTH // LANES):
                order_ref[slot, s, pl.ds(r, tm // dil, stride=dil), :] = blk[:, s * LANES:(s + 1) * LANES]
        return jnp.concatenate([order_ref[slot, s] for s in range(GROUP_WIDTH // LANES)], axis=1)

    at_seq_start = pl.program_id(0) % tiles_per_seq == 0
    uext[0:CONV_HALO, :] = jnp.where(at_seq_start, 0.0, halo_ref[...])
    uext[CONV_HALO:, :] = u_ref[...]
    shift_rows = shift_ref.shape[1]
    for b in range(1, SUBLANES):
        shift_ref[b - 1] = uext[b:b + shift_rows, :]

    def conv_rows(rc, carry):
        base = pl.multiple_of(rc * CONV_ROW_CHUNK, CONV_ROW_CHUNK)
        acc = jnp.broadcast_to(dwb_ref[...], (CONV_ROW_CHUNK, CONV_CH))
        for w in range(CONV_WIDTH):
            a, b = divmod(CONV_HALO - (CONV_WIDTH - 1) + w, SUBLANES)
            rows = pl.ds(base + a * SUBLANES, CONV_ROW_CHUNK)
            tap = uext[rows, :] if b == 0 else shift_ref[b - 1, rows, :]
            acc = acc + tap * dw_ref[w:w + 1, :]
        conv_ref[pl.ds(base, CONV_ROW_CHUNK), :] = acc
        return carry

    lax.fori_loop(0, tm // CONV_ROW_CHUNK, conv_rows, 0)

    dils = [dil for _, dil in ATTN_GROUPS]
    lses = [token_order(ref, dil, 2 * g) for g, (ref, dil) in enumerate(zip((l0_ref, l1_ref, l2_ref), dils))]
    outs = [token_order(ref, dil, 2 * g + 1) for g, (ref, dil) in enumerate(zip((o0_ref, o1_ref, o2_ref), dils))]

    for hf in range(2):
        rows = slice(hf * tm // 2, (hf + 1) * tm // 2)
        c = conv_ref[rows, :]
        mu = jnp.mean(c, axis=-1, keepdims=True)
        cc = c - mu
        var = jnp.mean(cc * cc, axis=-1, keepdims=True)
        c = cc * lax.rsqrt(var + EPS) * lng_ref[...] + lnb_ref[...]
        c = c * jax.nn.sigmoid(c)
        conv_out = jnp.dot(c.astype(BF16), wc_ref[...], preferred_element_type=F32) + bc_ref[...]

        lse_h = [l[rows] for l in lses]
        m = jnp.maximum(jnp.maximum(lse_h[0], lse_h[1]), lse_h[2])
        es = [jnp.exp(l - m) for l in lse_h]
        inv = 1.0 / (es[0] + es[1] + es[2])
        attn_out = jnp.zeros((tm // 2, d), F32)
        for g in range(3):
            og = (outs[g][rows] * (es[g] * inv)).astype(BF16)
            attn_out = attn_out + jnp.dot(og, wa_ref[g * GROUP_WIDTH:(g + 1) * GROUP_WIDTH, :],
                                          preferred_element_type=F32)

        merged = (gate_ref[rows, :d].astype(F32) * conv_out
                  + gate_ref[rows, d:].astype(F32) * attn_out)
        h_ref[rows, :] = x_ref[rows, :] + jnp.dot(merged.astype(BF16), wo_ref[...],
                                                  preferred_element_type=F32)


def _mixer(u, outs, lses, gates, x2, dw, dwb, lng, lnb, wc, bc, wa, wo, tm, seq):
    n, d = x2.shape
    row = lambda w: pl.BlockSpec((tm, w), lambda i: (i, 0))
    halo = pl.BlockSpec((CONV_HALO, CONV_CH),
                        lambda i: (jnp.maximum(i * (tm // CONV_HALO) - 1, 0), 0))
    tiles_per_seq = seq // tm
    grouped = lambda dil: pl.BlockSpec((None, dil, tm // dil, GROUP_WIDTH),
                                       lambda i: (i // tiles_per_seq, 0, i % tiles_per_seq, 0))
    group_specs = [grouped(dil) for _, dil in ATTN_GROUPS]
    return pl.pallas_call(
        functools.partial(_mixer_kernel, tm=tm, tiles_per_seq=tiles_per_seq),
        grid=(n // tm,),
        in_specs=[row(CONV_CH), halo] + group_specs * 2 + [row(2 * d), row(d)]
                 + [_full(a.shape) for a in (dw, dwb, lng, lnb, wc, bc, wa, wo)],
        out_specs=row(d),
        out_shape=jax.ShapeDtypeStruct((n, d), F32),
        scratch_shapes=[pltpu.VMEM((tm + CONV_HALO, CONV_CH), F32),
                        pltpu.VMEM((2 * len(ATTN_GROUPS), GROUP_WIDTH // LANES, tm, LANES), F32),
                        pltpu.VMEM((SUBLANES - 1, tm + CONV_HALO - SUBLANES, CONV_CH), F32),
                        pltpu.VMEM((tm, CONV_CH), F32)],
        compiler_params=_params(1),
        name="mixer_out",
    )(u, u, *outs, *lses, gates, x2, dw, dwb, lng, lnb, wc, bc, wa, wo)


_CAND_WIDTHS = tuple(PEER_TOPK // (r1 + 1) for r1 in range(PEER_TOPK))
_CAND_BLOCK_ROWS = 8
_UNRANKED = float(PEER_N_KEYS)


def _pack_rows(x):
    return pltpu.bitcast(x, jnp.uint32)


def _unpack_rows(x):
    return pltpu.bitcast(x, BF16)


def _col_reduce(x, op, reduce_fn):
    parts = [x[i:i + 8] for i in range(0, x.shape[0], 8)]
    while len(parts) > 1:
        nxt = [op(parts[i], parts[i + 1]) for i in range(0, len(parts) - 1, 2)]
        parts = nxt + ([parts[-1]] if len(parts) % 2 else [])
    return reduce_fn(parts[0], axis=0, keepdims=True)


def _col_max(x):
    return _col_reduce(x, jnp.maximum, jnp.max)


def _col_min(x):
    return _col_reduce(x, jnp.minimum, jnp.min)


def _col_sum(x):
    return _col_reduce(x, jnp.add, jnp.sum)


def _extract_top(s, key_id, on_round):
    for r in range(PEER_TOPK):
        m = _col_max(s)
        idx = _col_min(jnp.where(s == m, key_id, 1e9))
        sel = key_id == idx
        s = jnp.where(sel, -jnp.inf, s)
        on_round(r, m, idx, sel)


def _rank_keys(s, key_id, iota16):
    state = [jnp.zeros(iota16.shape, F32), jnp.full(s.shape, _UNRANKED, F32)]

    def on_round(r, m, idx, sel):
        state[0] = jnp.where(iota16 == float(r), m, state[0])
        state[1] = jnp.where(sel, float(r), state[1])

    _extract_top(s, key_id, on_round)
    return state[0], state[1]


_MARK_SCALE = 2.0 ** 100
_CAND_MARK = -_MARK_SCALE


def _top_values_unique(scores, iota16):
    scores = list(scores)
    tops = [jnp.zeros(iota16.shape, F32) for _ in scores]
    for r in range(PEER_TOPK):
        for i, s in enumerate(scores):
            m = _col_max(s)
            scores[i] = jnp.where(s == m, -_MARK_SCALE * (1.0 + r / 32.0), s)
            tops[i] = jnp.where(iota16 == float(r), m, tops[i])
    return scores, tops


def _decode_marks(s):
    marked = s < -0.5 * _MARK_SCALE
    rank = jnp.where(marked, (s * (-1.0 / _MARK_SCALE) - 1.0) * 32.0, _UNRANKED)
    return rank, _col_sum(marked.astype(F32))


def _candidate_sums(ss1, ss2, iota8):
    blocks = [ss1[0:1, :] + ss2]
    for r1 in range(1, _CAND_BLOCK_ROWS):
        blk = ss1[r1:r1 + 1, :] + ss2[0:_CAND_BLOCK_ROWS, :]
        blocks.append(jnp.where(iota8 < float(_CAND_WIDTHS[r1]), blk, -jnp.inf))
    blocks.append(ss1[_CAND_BLOCK_ROWS:, :] + ss2[0:1, :])
    return jnp.concatenate(blocks, axis=0)


def _peer_prep_kernel(h_ref, g2_ref, wqt_ref, keys_ref,
                      hnt_ref, rk2_ref, b_ref, cnt_ref, a_ref,
                      qt_ref, s1_ref, s2_ref, *, tt):
    h = h_ref[...]
    ms = jnp.mean(h * h, axis=-1, keepdims=True)
    hnt = (h * lax.rsqrt(ms + EPS) * g2_ref[...]).T.astype(BF16)
    hnt_ref[...] = _pack_rows(hnt)
    qt_ref[...] = jnp.dot(wqt_ref[...], hnt, preferred_element_type=F32)

    key_id = lax.broadcasted_iota(jnp.int32, (PEER_N_KEYS, LANES), 0).astype(F32)
    iota16 = lax.broadcasted_iota(jnp.int32, (PEER_TOPK, LANES), 0).astype(F32)
    iota8 = lax.broadcasted_iota(jnp.int32, (_CAND_BLOCK_ROWS, LANES), 0).astype(F32)
    cand_id = jnp.concatenate(
        [iota16]
        + [iota8 + float(r1 * PEER_TOPK) for r1 in range(1, _CAND_BLOCK_ROWS)]
        + [(iota8 + float(_CAND_BLOCK_ROWS)) * float(PEER_TOPK)], axis=0)

    def head_body(hd, carry):
        for p, dst_ref in ((0, s1_ref), (1, s2_ref)):
            base = pl.multiple_of((hd * 2 + p) * PEER_N_KEYS, PEER_N_KEYS)
            qhp = qt_ref[pl.ds(base, PEER_N_KEYS), :].astype(BF16)
            dst_ref[...] = jnp.dot(keys_ref[hd, p], qhp, preferred_element_type=F32)

        k = PEER_TOPK

        def write_gate_state(lanes, ss1, ss2, best, cnt16, rk1, rk2):
            z = jnp.sum(jnp.exp(best - best[0:1, :]), axis=0, keepdims=True)
            a_ref[hd, :, lanes] = jnp.exp(s1_ref[:, lanes] - ss1[0:1, :])
            b_ref[hd, :, lanes] = _pack_rows((jnp.exp(s2_ref[:, lanes] - ss2[0:1, :]) / z).astype(BF16))
            rk2_ref[hd, :, lanes] = _pack_rows(rk2.astype(BF16))
            cnt = jnp.zeros((PEER_N_KEYS, LANES), F32)
            for r1 in range(PEER_TOPK):
                cnt = jnp.where(rk1 == float(r1), cnt16[r1:r1 + 1, :], cnt)
            cnt_ref[hd, :, lanes] = cnt

        def fast_pair(cp):
            lane_sets = [slice((2 * cp + j) * LANES, (2 * cp + j + 1) * LANES) for j in range(2)]
            tops, ranks, ties = {}, {}, []
            for half, s_ref in enumerate((s1_ref, s2_ref)):
                marked, tops[half] = _top_values_unique([s_ref[:, l] for l in lane_sets], iota16)
                for j in range(2):
                    ranks[half, j], n = _decode_marks(marked[j])
                    ties.append(jnp.abs(n - k))
            cands = [_candidate_sums(tops[0][j], tops[1][j], iota8) for j in range(2)]
            bests = [jnp.zeros((k, LANES), F32) for _ in range(2)]
            for r in range(k):
                for j in range(2):
                    m = _col_max(cands[j])
                    cands[j] = jnp.where(cands[j] == m, _CAND_MARK, cands[j])
                    bests[j] = jnp.where(iota16 == float(r), m, bests[j])
            for j in range(2):
                picked = (cands[j] == _CAND_MARK).astype(F32)
                per_row = [jnp.sum(picked[0:k, :], axis=0, keepdims=True)]
                for r1 in range(1, _CAND_BLOCK_ROWS):
                    lo = k + (r1 - 1) * _CAND_BLOCK_ROWS
                    per_row.append(jnp.sum(picked[lo:lo + _CAND_BLOCK_ROWS, :], axis=0, keepdims=True))
                cnt16 = jnp.concatenate(per_row + [picked[k + 7 * _CAND_BLOCK_ROWS:, :]], axis=0)
                ties.append(jnp.abs(jnp.sum(cnt16, axis=0, keepdims=True) - k))
                write_gate_state(lane_sets[j], tops[0][j], tops[1][j], bests[j], cnt16,
                                 ranks[0, j], ranks[1, j])
            return functools.reduce(jnp.add, ties)

        tie = functools.reduce(jnp.add, [fast_pair(cp) for cp in range(tt // (2 * LANES))])

        @pl.when(jnp.max(tie) > 0.5)
        def _():
            def redo(c, carry2):
                lanes = pl.ds(pl.multiple_of(c * LANES, LANES), LANES)
                ss1x, rk1x = _rank_keys(s1_ref[:, lanes], key_id, iota16)
                ss2x, rk2x = _rank_keys(s2_ref[:, lanes], key_id, iota16)
                state = [jnp.zeros((k, LANES), F32), jnp.zeros((k, LANES), F32)]

                def on_round(r, m, idx, sel):
                    state[0] = jnp.where(iota16 == float(r), m, state[0])
                    row = jnp.floor(idx * (1.0 / k))
                    state[1] = state[1] + (iota16 == row).astype(F32)

                _extract_top(_candidate_sums(ss1x, ss2x, iota8), cand_id, on_round)
                write_gate_state(lanes, ss1x, ss2x, state[0], state[1], rk1x, rk2x)
                return carry2

            lax.fori_loop(0, tt // LANES, redo, 0)

        return carry

    lax.fori_loop(0, PEER_HEADS, head_body, 0)


def _peer_prep(h2, g2, wqt, keys, tt):
    n, d = h2.shape
    nt = n // tt
    state = lambda rows: pl.BlockSpec((None, PEER_HEADS, rows, tt), lambda i: (i, 0, 0, 0))
    shape = lambda rows, dt: jax.ShapeDtypeStruct((nt, PEER_HEADS, rows, tt), dt)
    half = PEER_N_KEYS // 2
    return pl.pallas_call(
        functools.partial(_peer_prep_kernel, tt=tt),
        grid=(nt,),
        in_specs=[pl.BlockSpec((tt, d), lambda i: (i, 0)), _full(g2.shape), _full(wqt.shape),
                  _full(keys.shape)],
        out_specs=[pl.BlockSpec((d // 2, tt), lambda i: (0, i)), state(half), state(half),
                   state(PEER_N_KEYS), state(PEER_N_KEYS)],
        out_shape=[jax.ShapeDtypeStruct((d // 2, n), jnp.uint32), shape(half, jnp.uint32),
                   shape(half, jnp.uint32), shape(PEER_N_KEYS, F32), shape(PEER_N_KEYS, F32)],
        scratch_shapes=[pltpu.VMEM((wqt.shape[0], tt), F32),
                        pltpu.VMEM((PEER_N_KEYS, tt), F32),
                        pltpu.VMEM((PEER_N_KEYS, tt), F32)],
        compiler_params=_params(1),
        name="peer_prep",
    )(h2, g2, wqt, keys)


def _peer_main_kernel(hnt_ref, rk2_ref, b_ref, cnt_ref, a_ref, u_ref, vt_ref, res_ref,
                      y_ref, yt_ref, ht0_ref, ht1_ref, act0_ref, act1_ref, rows_ref,
                      *, keys_per_chunk, tt, n_chunks, n_items):
    g = pl.program_id(0)
    item3 = g - 2
    first_of_tile = (item3 >= 0) & (item3 % n_chunks == 0)
    last_of_tile = (item3 >= 0) & (item3 % n_chunks == n_chunks - 1)

    @pl.when(g == 0)
    def _():
        for ref in (ht0_ref, ht1_ref):
            ref[...] = jnp.zeros(ref.shape, F32)
        for ref in (act0_ref, act1_ref):
            ref[...] = jnp.zeros(ref.shape, jnp.uint32)

    @pl.when((g == 0) | first_of_tile)
    def _():
        yt_ref[...] = jnp.zeros(yt_ref.shape, F32)

    half = PEER_N_KEYS // 2
    chunk2 = jnp.clip(g - 1, 0, n_items - 1) % n_chunks

    def gate_block(c, k0, ht_old, act_new):
        lanes = slice(c * LANES, (c + 1) * LANES)
        ks = range(k0, k0 + KEYS_PER_PASS)
        gates = {k: jnp.zeros((PEER_N_KEYS, LANES), BF16) for k in ks}
        for hd in range(PEER_HEADS):
            rk2 = _unpack_rows(rk2_ref[hd, :, lanes])
            b = _unpack_rows(b_ref[hd, :, lanes])
            for k in ks:
                cnt_row = rows_ref[k, hd:hd + 1, lanes].astype(BF16)
                a_row = rows_ref[k, PEER_HEADS + hd:PEER_HEADS + hd + 1, lanes].astype(BF16)
                gates[k] = gates[k] + jnp.where(rk2 < cnt_row, b * a_row, 0.0)
        for k in ks:
            x = ht_old[k * PEER_N_KEYS:(k + 1) * PEER_N_KEYS, lanes]
            gelu = 0.5 * x * (1.0 + lax.erf(x * math.sqrt(0.5)))
            act_new[k * half:(k + 1) * half, lanes] = _pack_rows(gelu.astype(BF16) * gates[k])

    def step(ht_new, ht_old, act_new, act_old):
        for k in range(keys_per_chunk):
            i1 = chunk2 * keys_per_chunk + k
            for hd in range(PEER_HEADS):
                rows_ref[k, hd:hd + 1, :] = cnt_ref[hd, pl.ds(i1, 1), :]
                rows_ref[k, PEER_HEADS + hd:PEER_HEADS + hd + 1, :] = a_ref[hd, pl.ds(i1, 1), :]

        def u_piece(piece, q):
            cols = slice(piece * MXU_COLS, (piece + 1) * MXU_COLS)
            rp = u_ref.shape[0] // ROW_SPLIT
            ht_new[2 * q * rp:2 * (q + 1) * rp, cols] = jnp.dot(
                _unpack_rows(u_ref[q * rp:(q + 1) * rp, :]), _unpack_rows(hnt_ref[:, cols]),
                preferred_element_type=F32)

        def v_piece(piece, q):
            cols = slice(piece * MXU_COLS, (piece + 1) * MXU_COLS)
            rp = vt_ref.shape[0] // ROW_SPLIT
            yt_ref[2 * q * rp:2 * (q + 1) * rp, cols] += jnp.dot(
                _unpack_rows(vt_ref[q * rp:(q + 1) * rp, :]), _unpack_rows(act_old[:, cols]),
                preferred_element_type=F32)

        mxu_work = [functools.partial(f, p, q) for p in range(tt // MXU_COLS)
                    for f in (u_piece, v_piece) for q in range(ROW_SPLIT)]
        vpu_work = [functools.partial(gate_block, c, k0, ht_old, act_new)
                    for c in range(tt // LANES) for k0 in range(0, keys_per_chunk, KEYS_PER_PASS)]
        done_m = done_v = 0
        while done_m < len(mxu_work) or done_v < len(vpu_work):
            behind = done_m * len(vpu_work) < done_v * len(mxu_work)
            if (behind and done_m < len(mxu_work)) or done_v == len(vpu_work):
                mxu_work[done_m]()
                done_m += 1
            else:
                vpu_work[done_v]()
                done_v += 1

    @pl.when(g % 2 == 0)
    def _():
        step(ht0_ref, ht1_ref, act1_ref, act0_ref)

    @pl.when(g % 2 == 1)
    def _():
        step(ht1_ref, ht0_ref, act0_ref, act1_ref)

    @pl.when(last_of_tile)
    def _():
        y_ref[...] = res_ref[...] + yt_ref[...].T


def _peer_main(hnt, rk2, b, cnt, a, u_packed, vt_packed, h2, tt, chunk):
    n, d = h2.shape
    n_chunks = 2 * u_packed.shape[0] // chunk
    n_items = (n // tt) * n_chunks
    kpc = chunk // PEER_N_KEYS
    assert kpc % KEYS_PER_PASS == 0 and chunk % (2 * ROW_SPLIT) == 0 and d % (2 * ROW_SPLIT) == 0
    item = lambda g, lag: jnp.clip(g - lag, 0, n_items - 1)
    tile = lambda g, lag: item(g, lag) // n_chunks
    chunk_of = lambda g, lag: item(g, lag) % n_chunks
    state = lambda arr: pl.BlockSpec((None,) + arr.shape[1:], lambda g: (tile(g, 1), 0, 0, 0))
    return pl.pallas_call(
        functools.partial(_peer_main_kernel, keys_per_chunk=kpc, tt=tt, n_chunks=n_chunks,
                          n_items=n_items),
        grid=(n_items + 2,),
        in_specs=[pl.BlockSpec((d // 2, tt), lambda g: (0, tile(g, 0))), state(rk2), state(b),
                  state(cnt), state(a),
                  pl.BlockSpec((chunk // 2, d), lambda g: (chunk_of(g, 0), 0)),
                  pl.BlockSpec((d // 2, chunk), lambda g: (0, chunk_of(g, 2))),
                  pl.BlockSpec((tt, d), lambda g: (tile(g, 2), 0))],
        out_specs=pl.BlockSpec((tt, d), lambda g: (tile(g, 2), 0)),
        out_shape=jax.ShapeDtypeStruct((n, d), F32),
        scratch_shapes=[pltpu.VMEM((d, tt), F32),
                        pltpu.VMEM((chunk, tt), F32),
                        pltpu.VMEM((chunk, tt), F32),
                        pltpu.VMEM((chunk // 2, tt), jnp.uint32),
                        pltpu.VMEM((chunk // 2, tt), jnp.uint32),
                        pltpu.VMEM((kpc, 2 * PEER_HEADS, tt), F32)],
        compiler_params=_params(1),
        name="peer_main",
    )(hnt, rk2, b, cnt, a, u_packed, vt_packed, h2)


def _pack_table_kernel(w_ref, o_ref, *, transpose):
    w = w_ref[...]
    o_ref[...] = _pack_rows((w.T if transpose else w).astype(BF16))


def _pack_table(w, transpose, rows=512):
    r, c = w.shape
    if transpose:
        out_shape, out_spec = (c // 2, r), pl.BlockSpec((c // 2, rows), lambda i: (0, i))
    else:
        out_shape, out_spec = (r // 2, c), pl.BlockSpec((rows // 2, c), lambda i: (i, 0))
    return pl.pallas_call(
        functools.partial(_pack_table_kernel, transpose=transpose),
        grid=(r // rows,),
        in_specs=[pl.BlockSpec((rows, c), lambda i: (i, 0))],
        out_specs=out_spec,
        out_shape=jax.ShapeDtypeStruct(out_shape, jnp.uint32),
        compiler_params=_params(1),
        name="pack_vt" if transpose else "pack_u",
    )(w)


def _layer(x2, batch, seq, w_in, conv_b_glu, conv_dw_w, conv_dw_b, conv_ln_g, conv_ln_b, conv_w_proj,
           conv_b_proj, q_norm_g, k_norm_g, rel_bias, attn_w_proj, mix_w_out, norm1_g, norm2_g,
           peer_w_q, peer_sub_keys, peer_u, peer_v):
    n, d = x2.shape
    row = lambda v: v.reshape(1, -1).astype(F32)
    c0 = 2 * CONV_CH
    cuts = [c0, c0 + ATTN_WIDTH, c0 + 2 * ATTN_WIDTH, c0 + 3 * ATTN_WIDTH]
    wglu, wq, wk, wv, wg = [w.astype(BF16) for w in jnp.split(w_in, cuts, axis=-1)]
    head_of = np.arange(ATTN_WIDTH) // HEAD_DIM
    pm = jnp.asarray((head_of[:, None] == head_of[None, :]) / HEAD_DIM, BF16)
    tile_heads = lambda g: jnp.tile(g.astype(F32), N_ATTN_HEADS).reshape(1, ATTN_WIDTH)

    u, (qs, ks, vs), gates = _inproj(x2, row(norm1_g), wglu, row(conv_b_glu), wq, wk, wv,
                                     tile_heads(q_norm_g), tile_heads(k_norm_g), pm, wg,
                                     tm=512, batch=batch, seq=seq)

    outs, lses = [], []
    for g, (_, dil) in enumerate(ATTN_GROUPS):
        o, lse = _attn_group(qs[g], ks[g], vs[g], _band_bias(rel_bias, g, dil), dil)
        outs.append(o)
        lses.append(lse)

    h2 = _mixer(u, outs, lses, gates, x2, conv_dw_w.reshape(CONV_WIDTH, CONV_CH).astype(F32),
                row(conv_dw_b), row(conv_ln_g), row(conv_ln_b), conv_w_proj.astype(BF16),
                row(conv_b_proj), attn_w_proj.astype(BF16), mix_w_out.astype(BF16),
                tm=min(512, seq), seq=seq)

    tt = min(512, n)
    hnt, rk2, b, cnt, a = _peer_prep(h2, row(norm2_g), peer_w_q.T.astype(BF16),
                                     peer_sub_keys.astype(BF16), tt)
    return _peer_main(hnt, rk2, b, cnt, a, _pack_table(peer_u, transpose=False),
                      _pack_table(peer_v, transpose=True), h2, tt, chunk=1024)


def kernel(x, w_in, conv_b_glu, conv_dw_w, conv_dw_b, conv_ln_g, conv_ln_b, conv_w_proj, conv_b_proj, q_norm_g, k_norm_g, rel_bias, attn_w_proj, mix_w_out, norm1_g, norm2_g, peer_w_q, peer_sub_keys, peer_u, peer_v):
    batch, seq, d = x.shape
    x2 = x.reshape(batch * seq, d)
    for l in range(w_in.shape[0]):
        x2 = _layer(x2, batch, seq, w_in[l], conv_b_glu[l], conv_dw_w[l], conv_dw_b[l], conv_ln_g[l],
                    conv_ln_b[l], conv_w_proj[l], conv_b_proj[l], q_norm_g[l], k_norm_g[l], rel_bias,
                    attn_w_proj[l], mix_w_out[l], norm1_g[l], norm2_g[l], peer_w_q[l],
                    peer_sub_keys[l], peer_u[l], peer_v[l])
    return x2.reshape(batch, seq, d)
```
